```python
import math
import jax, jax.numpy as jnp
from jax import lax
import numpy as np

D_MODEL = 1024
BATCH = 32
SEQ = 256
DEPTH = 4
DEC_BATCH = 4
DEC_SEQ = 2048
PAST_LEN = 512

GRID_W = 64
N_MIXERS = 3
NA_HEADS = 16
NA_HEAD_DIM = D_MODEL // NA_HEADS
WIN_R = 8
WIN_C = 16
DIFF_HEADS = 8
DIFF_HEAD_DIM = D_MODEL // (2 * DIFF_HEADS)
MLA_HEADS = 16
MLA_Q_RANK = 256
MLA_KV_RANK = 128
MLA_NOPE = 64
MLA_ROPE = 32
MLA_V = 64
N_EXPERTS = 32
TOP_K = 4
D_EXPERT = D_MODEL
SWIGLU_LIMIT = 7.0
SWIGLU_ALPHA = 1.702
ROPE_BASE = 10000.0
Q_BLOCK = 128
LN_EPS = 1e-5
RMS_EPS = 1e-6
NEG_INF = -1e30
DEEPNORM_ALPHA = (2 * DEPTH) ** 0.25
DEEPNORM_BETA = (8 * DEPTH) ** -0.25
N_NA = len(range(0, DEPTH, N_MIXERS))
N_DIFF = len(range(1, DEPTH, N_MIXERS))
N_MLA = len(range(2, DEPTH, N_MIXERS))

kernel_name = 'hybrid_dit_na_diff_mla_moe_step'


def _layernorm(x, g, b):
    xf = x.astype(jnp.float32)
    mu = jnp.mean(xf, axis=-1, keepdims=True)
    var = jnp.mean(jnp.square(xf - mu), axis=-1, keepdims=True)
    return ((xf - mu) * lax.rsqrt(var + LN_EPS) * g + b).astype(x.dtype)


def _rmsnorm(x, g):
    xf = x.astype(jnp.float32)
    return (xf * lax.rsqrt(jnp.mean(jnp.square(xf), axis=-1, keepdims=True) + RMS_EPS) * g).astype(x.dtype)


def _rope_1d(x, pos):
    nf = x.shape[-1] // 2
    inv = ROPE_BASE ** (-jnp.arange(nf, dtype=jnp.float32) / nf)
    ang = pos.astype(jnp.float32)[:, None] * inv
    shape = (pos.shape[0],) + (1,) * (x.ndim - 3) + (nf,)
    cos = jnp.cos(ang).reshape(shape).astype(x.dtype)
    sin = jnp.sin(ang).reshape(shape).astype(x.dtype)
    x1, x2 = x[..., :nf], x[..., nf:]
    return jnp.concatenate([x1 * cos - x2 * sin, x2 * cos + x1 * sin], axis=-1)


def _axial_rope(x):
    t = jnp.arange(x.shape[1])
    half = x.shape[-1] // 2
    return jnp.concatenate([_rope_1d(x[..., :half], t // GRID_W),
                            _rope_1d(x[..., half:], t % GRID_W)], axis=-1)


def _over_query_blocks(fn, *qs):
    b, s = qs[0].shape[:2]
    nb = s // Q_BLOCK
    blocks = tuple(jnp.moveaxis(a.reshape(b, nb, Q_BLOCK, *a.shape[2:]), 1, 0) for a in qs)
    out = lax.map(lambda t: fn(*t), blocks)
    out = jnp.moveaxis(out, 0, 1)
    return out.reshape(b, s, *out.shape[3:])


def _dense_attn(q, k, v):
    s = jnp.einsum('bqhd,bkhd->bhqk', q, k).astype(jnp.float32) * (q.shape[-1] ** -0.5)
    p = jax.nn.softmax(s, axis=-1).astype(v.dtype)
    return jnp.einsum('bhqk,bkhd->bqhd', p, v)


def _na_latent(q, k, v, k_ctx, v_ctx, rpb):
    b, s, nh, dh = q.shape
    rows_n = s // GRID_W
    kr = min(WIN_R, rows_n)
    rows = jnp.arange(rows_n)
    r0 = jnp.clip(rows - kr // 2, 0, rows_n - kr)
    row_idx = r0[:, None] + jnp.arange(kr)
    cols = jnp.arange(GRID_W)
    c0 = jnp.clip(cols - WIN_C // 2, 0, GRID_W - WIN_C)
    in_win = (cols[None, :] >= c0[:, None]) & (cols[None, :] < c0[:, None] + WIN_C)
    qg = q.reshape(b, rows_n, GRID_W, nh, dh)
    kg = k.reshape(b, rows_n, GRID_W, nh, dh)[:, row_idx]
    vg = v.reshape(b, rows_n, GRID_W, nh, dh)[:, row_idx]
    scale = dh ** -0.5
    row_off = row_idx - rows[:, None] + (WIN_R - 1)
    col_off = jnp.clip(cols[None, :] - cols[:, None] + (WIN_C - 1), 0, 2 * WIN_C - 2)
    bias = rpb[:, row_off[:, None, :, None], col_off[None, :, None, :]].astype(jnp.float32)
    s_loc = jnp.einsum('brchd,brjkhd->bhrcjk', qg, kg).astype(jnp.float32) * scale + bias
    s_loc = jnp.where(in_win[:, None, :], s_loc, NEG_INF)
    s_ctx = jnp.einsum('brchd,bphd->bhrcp', qg, k_ctx).astype(jnp.float32) * scale
    n_loc = kr * GRID_W
    sc = jnp.concatenate([s_loc.reshape(b, nh, rows_n, GRID_W, n_loc), s_ctx], axis=-1)
    p = jax.nn.softmax(sc, axis=-1).astype(v.dtype)
    p_loc = p[..., :n_loc].reshape(b, nh, rows_n, GRID_W, kr, GRID_W)
    p_ctx = p[..., n_loc:]
    out = (jnp.einsum('bhrcjk,brjkhd->brchd', p_loc, vg)
           + jnp.einsum('bhrcp,bphd->brchd', p_ctx, v_ctx))
    return out.reshape(b, s, nh, dh)


def _na_mixer(h, w_qkv, rpb, w_o, ctx):
    b, s, _ = h.shape
    qkv = (h @ w_qkv).reshape(b, s, 3, NA_HEADS, NA_HEAD_DIM)
    q, k, v = qkv[:, :, 0], qkv[:, :, 1], qkv[:, :, 2]
    if ctx is None:
        o = _over_query_blocks(lambda qi: _dense_attn(qi, k, v), q)
        state = (k, v)
    else:
        o = _na_latent(q, k, v, ctx[0], ctx[1], rpb)
        state = None
    return o.reshape(b, s, D_MODEL) @ w_o, state


def _diff_mixer(h, w_qkv, lam_p, subln, w_o, lam_init, ctx):
    b, s, _ = h.shape
    hd = DIFF_HEAD_DIM
    qkv = (h @ w_qkv).reshape(b, s, 3, DIFF_HEADS, 2 * hd)
    q, k, v = qkv[:, :, 0], qkv[:, :, 1], qkv[:, :, 2]
    if ctx is None:
        k_all, v_all, state = k, v, (k, v)
    else:
        q = _axial_rope(q.reshape(b, s, 2 * DIFF_HEADS, hd)).reshape(b, s, DIFF_HEADS, 2 * hd)
        k = _axial_rope(k.reshape(b, s, 2 * DIFF_HEADS, hd)).reshape(b, s, DIFF_HEADS, 2 * hd)
        k_all = jnp.concatenate([k, ctx[0]], axis=1)
        v_all = jnp.concatenate([v, ctx[1]], axis=1)
        state = None
    lp = lam_p.astype(jnp.float32)
    lam = jnp.exp(jnp.sum(lp[0] * lp[1])) - jnp.exp(jnp.sum(lp[2] * lp[3])) + lam_init
    k1, k2 = k_all[..., :hd], k_all[..., hd:]
    scale = hd ** -0.5

    def blk(q1, q2):
        s1 = jnp.einsum('bqhd,bkhd->bhqk', q1, k1).astype(jnp.float32) * scale
        s2 = jnp.einsum('bqhd,bkhd->bhqk', q2, k2).astype(jnp.float32) * scale
        p = jax.nn.softmax(s1, axis=-1) - lam * jax.nn.softmax(s2, axis=-1)
        return jnp.einsum('bhqk,bkhe->bqhe', p.astype(v_all.dtype), v_all)

    o = _over_query_blocks(blk, q[..., :hd], q[..., hd:])
    o = _rmsnorm(o, subln) * (1.0 - lam_init)
    return o.reshape(b, s, D_MODEL) @ w_o, state


def _mla_mixer(h, w_in, q_norm, w_uq, kv_norm, w_ukv, w_o, ctx):
    b, s, _ = h.shape
    proj = h @ w_in
    cq = _rmsnorm(proj[..., :MLA_Q_RANK], q_norm)
    ckv = _rmsnorm(proj[..., MLA_Q_RANK:MLA_Q_RANK + MLA_KV_RANK], kv_norm)
    kr = proj[..., MLA_Q_RANK + MLA_KV_RANK:]
    q = (cq @ w_uq).reshape(b, s, MLA_HEADS, MLA_NOPE + MLA_ROPE)
    qn, qr = q[..., :MLA_NOPE], q[..., MLA_NOPE:]
    if ctx is None:
        ckv_all, kr_all, state = ckv, kr, (ckv, kr)
    else:
        qr = _axial_rope(qr)
        kr = _axial_rope(kr)
        ckv_all = jnp.concatenate([ckv, ctx[0]], axis=1)
        kr_all = jnp.concatenate([kr, ctx[1]], axis=1)
        state = None
    kv = (ckv_all @ w_ukv).reshape(b, ckv_all.shape[1], MLA_HEADS, MLA_NOPE + MLA_V)
    kn, v = kv[..., :MLA_NOPE], kv[..., MLA_NOPE:]
    scale = (MLA_NOPE + MLA_ROPE) ** -0.5

    def blk(qni, qri):
        sc = (jnp.einsum('bqhd,bkhd->bhqk', qni, kn)
              + jnp.einsum('bqhr,bkr->bhqk', qri, kr_all)).astype(jnp.float32) * scale
        p = jax.nn.softmax(sc, axis=-1).astype(v.dtype)
        return jnp.einsum('bhqk,bkhd->bqhd', p, v)

    o = _over_query_blocks(blk, qn, qr)
    return o.reshape(b, s, MLA_HEADS * MLA_V) @ w_o, state


def _moe(h, rw, rb, w1, b1, w2, b2):
    shp = h.shape
    x = h.reshape(-1, shp[-1])
    logits = (x @ rw + rb).astype(jnp.float32)
    top_v, top_i = lax.top_k(logits, TOP_K)
    w = jax.nn.softmax(top_v, axis=-1)
    gates = jnp.sum(jax.nn.one_hot(top_i, N_EXPERTS, dtype=jnp.float32) * w[..., None], axis=1)
    y = jnp.zeros_like(x)
    for e in range(N_EXPERTS):
        gu = x @ w1[e] + b1[e]
        g = jnp.minimum(gu[:, :D_EXPERT], SWIGLU_LIMIT)
        u = jnp.clip(gu[:, D_EXPERT:], -SWIGLU_LIMIT, SWIGLU_LIMIT)
        act = (u + 1.0) * (g * jax.nn.sigmoid(SWIGLU_ALPHA * g))
        y = y + gates[:, e:e + 1].astype(x.dtype) * (act @ w2[e] + b2[e])
    return y.reshape(shp)


def _adaln(cond, w, b):
    m = jax.nn.silu(cond) @ w + b
    return [t[:, None, :] for t in jnp.split(m, 6, axis=-1)]


def _modulate(x, shift, scale):
    return x * (1.0 + scale) + shift


def setup_inputs(seed: int = 0) -> dict:
    key = jax.random.key(seed)
    ks = iter(jax.random.split(key, 40))
    nrm = lambda shape, std: std * jax.random.normal(next(ks), shape, jnp.float32)
    D = D_MODEL
    hd = DIFF_HEAD_DIM
    return {
        'x_prompt': nrm((BATCH, SEQ, D), 1.0),
        'x_sample': nrm((DEC_BATCH, DEC_SEQ, D), 1.0),
        'cache_na_k': nrm((DEC_BATCH, N_NA, PAST_LEN, NA_HEADS, NA_HEAD_DIM), 1.0),
        'cache_na_v': nrm((DEC_BATCH, N_NA, PAST_LEN, NA_HEADS, NA_HEAD_DIM), 1.0),
        'cache_diff_k': nrm((DEC_BATCH, N_DIFF, PAST_LEN, DIFF_HEADS, 2 * hd), 1.0),
        'cache_diff_v': nrm((DEC_BATCH, N_DIFF, PAST_LEN, DIFF_HEADS, 2 * hd), 1.0),
        'cache_mla_ckv': nrm((DEC_BATCH, N_MLA, PAST_LEN, MLA_KV_RANK), 1.0),
        'cache_mla_kr': nrm((DEC_BATCH, N_MLA, PAST_LEN, MLA_ROPE), 1.0),
        'c': nrm((DEC_BATCH, D), 1.0),
        'c_ctx': nrm((D,), 1.0),
        'ada_w': nrm((DEPTH, D, 6 * D), 0.5 * D ** -0.5),
        'ada_b': nrm((DEPTH, 6 * D), 0.01),
        'ln_g': 1.0 + nrm((DEPTH, 2, D), 0.01),
        'ln_b': nrm((DEPTH, 2, D), 0.01),
        'na_w_qkv': nrm((N_NA, D, 3 * D), D ** -0.5),
        'na_rpb': nrm((N_NA, NA_HEADS, 2 * WIN_R - 1, 2 * WIN_C - 1), 0.1),
        'na_w_o': nrm((N_NA, D, D), DEEPNORM_BETA * D ** -0.5),
        'diff_w_qkv': nrm((N_DIFF, D, 3 * DIFF_HEADS * 2 * hd), D ** -0.5),
        'diff_lambda': nrm((N_DIFF, 4, hd), 0.1),
        'diff_subln': 1.0 + nrm((N_DIFF, 2 * hd), 0.01),
        'diff_w_o': nrm((N_DIFF, D, D), DEEPNORM_BETA * D ** -0.5),
        'mla_w_in': nrm((N_MLA, D, MLA_Q_RANK + MLA_KV_RANK + MLA_ROPE), D ** -0.5),
        'mla_q_norm': 1.0 + nrm((N_MLA, MLA_Q_RANK), 0.01),
        'mla_w_uq': nrm((N_MLA, MLA_Q_RANK, MLA_HEADS * (MLA_NOPE + MLA_ROPE)), MLA_Q_RANK ** -0.5),
        'mla_kv_norm': 1.0 + nrm((N_MLA, MLA_KV_RANK), 0.01),
        'mla_w_ukv': nrm((N_MLA, MLA_KV_RANK, MLA_HEADS * (MLA_NOPE + MLA_V)), MLA_KV_RANK ** -0.5),
        'mla_w_o': nrm((N_MLA, MLA_HEADS * MLA_V, D), DEEPNORM_BETA * (MLA_HEADS * MLA_V) ** -0.5),
        'router_w': nrm((DEPTH, D, N_EXPERTS), D ** -0.5),
        'router_b': nrm((DEPTH, N_EXPERTS), 0.01),
        'moe_w1': nrm((DEPTH, N_EXPERTS, D, 2 * D_EXPERT), D ** -0.5),
        'moe_b1': nrm((DEPTH, N_EXPERTS, 2 * D_EXPERT), 0.01),
        'moe_w2': nrm((DEPTH, N_EXPERTS, D_EXPERT, D), DEEPNORM_BETA * D_EXPERT ** -0.5),
        'moe_b2': nrm((DEPTH, N_EXPERTS, D), 0.01),
    }


def reference(x_prompt, x_sample, cache_na_k, cache_na_v, cache_diff_k, cache_diff_v,
              cache_mla_ckv, cache_mla_kr, c, c_ctx, ada_w, ada_b, ln_g, ln_b,
              na_w_qkv, na_rpb, na_w_o, diff_w_qkv, diff_lambda, diff_subln, diff_w_o,
              mla_w_in, mla_q_norm, mla_w_uq, mla_kv_norm, mla_w_ukv, mla_w_o,
              router_w, router_b, moe_w1, moe_b1, moe_w2, moe_b2):
    xp, xs = x_prompt, x_sample
    st_na_k, st_na_v, st_diff_k, st_diff_v, st_mla_ckv, st_mla_kr = [], [], [], [], [], []
    for i in range(DEPTH):
        kind, j = i % N_MIXERS, i // N_MIXERS
        mp = _adaln(c_ctx[None, :], ada_w[i], ada_b[i])
        ms = _adaln(c, ada_w[i], ada_b[i])
        hp = _modulate(xp, mp[0], mp[1])
        hs = _modulate(xs, ms[0], ms[1])
        if kind == 0:
            op, (sk, sv) = _na_mixer(hp, na_w_qkv[j], na_rpb[j], na_w_o[j], None)
            st_na_k.append(sk)
            st_na_v.append(sv)
            os_, _ = _na_mixer(hs, na_w_qkv[j], na_rpb[j], na_w_o[j],
                               (cache_na_k[:, j], cache_na_v[:, j]))
        elif kind == 1:
            lam_init = 0.8 - 0.6 * math.exp(-0.3 * i)
            op, (sk, sv) = _diff_mixer(hp, diff_w_qkv[j], diff_lambda[j], diff_subln[j],
                                       diff_w_o[j], lam_init, None)
            st_diff_k.append(sk)
            st_diff_v.append(sv)
            os_, _ = _diff_mixer(hs, diff_w_qkv[j], diff_lambda[j], diff_subln[j],
                                 diff_w_o[j], lam_init, (cache_diff_k[:, j], cache_diff_v[:, j]))
        else:
            op, (sc, sr) = _mla_mixer(hp, mla_w_in[j], mla_q_norm[j], mla_w_uq[j],
                                      mla_kv_norm[j], mla_w_ukv[j], mla_w_o[j], None)
            st_mla_ckv.append(sc)
            st_mla_kr.append(sr)
            os_, _ = _mla_mixer(hs, mla_w_in[j], mla_q_norm[j], mla_w_uq[j], mla_kv_norm[j],
                                mla_w_ukv[j], mla_w_o[j], (cache_mla_ckv[:, j], cache_mla_kr[:, j]))
        xp = _layernorm(DEEPNORM_ALPHA * xp + mp[2] * op, ln_g[i, 0], ln_b[i, 0])
        xs = _layernorm(DEEPNORM_ALPHA * xs + ms[2] * os_, ln_g[i, 0], ln_b[i, 0])
        fp = _moe(_modulate(xp, mp[3], mp[4]), router_w[i], router_b[i],
                  moe_w1[i], moe_b1[i], moe_w2[i], moe_b2[i])
        fs = _moe(_modulate(xs, ms[3], ms[4]), router_w[i], router_b[i],
                  moe_w1[i], moe_b1[i], moe_w2[i], moe_b2[i])
        xp = _layernorm(DEEPNORM_ALPHA * xp + mp[5] * fp, ln_g[i, 1], ln_b[i, 1])
        xs = _layernorm(DEEPNORM_ALPHA * xs + ms[5] * fs, ln_g[i, 1], ln_b[i, 1])
    y_prompt, y_sample = xp, xs
    state_na_k = jnp.stack(st_na_k, axis=1)
    state_na_v = jnp.stack(st_na_v, axis=1)
    state_diff_k = jnp.stack(st_diff_k, axis=1)
    state_diff_v = jnp.stack(st_diff_v, axis=1)
    state_mla_ckv = jnp.stack(st_mla_ckv, axis=1)
    state_mla_kr = jnp.stack(st_mla_kr, axis=1)
    return (y_prompt, y_sample, state_na_k, state_na_v, state_diff_k, state_diff_v, state_mla_ckv, state_mla_kr)
```

```python
import functools
import math

import jax
import jax.numpy as jnp
from jax import lax
from jax.experimental import pallas as pl
from jax.experimental.pallas import tpu as pltpu

F32 = jnp.float32
BF16 = jnp.bfloat16
HIGHEST = lax.Precision.HIGHEST

D_MODEL = 1024
BATCH = 32
SEQ = 256
DEPTH = 4
DEC_BATCH = 4
DEC_SEQ = 2048
PAST_LEN = 512
GRID_W = 64
N_MIXERS = 3
NA_HEADS = 16
NA_HEAD_DIM = 64
WIN_R = 8
WIN_C = 16
DIFF_HEADS = 8
DIFF_HEAD_DIM = 64
MLA_HEADS = 16
MLA_Q_RANK = 256
MLA_KV_RANK = 128
MLA_NOPE = 64
MLA_ROPE = 32
MLA_V = 64
N_EXPERTS = 32
TOP_K = 4
D_EXPERT = 1024
SWIGLU_LIMIT = 7.0
SWIGLU_ALPHA = 1.702
ROPE_BASE = 10000.0
LN_EPS = 1e-5
RMS_EPS = 1e-6
NEG_INF = -1e30
DEEPNORM_ALPHA = (2 * DEPTH) ** 0.25

LANES = 128
TM = 256
N_PROMPT = BATCH * SEQ
N_SAMPLE = DEC_BATCH * DEC_SEQ
N_TOK = N_PROMPT + N_SAMPLE
PROMPT_TILES = N_PROMPT // TM
SAMPLE_TILES = N_SAMPLE // TM
TILES_PER_DEC = DEC_SEQ // TM
N_MOD_ROWS = 8
NA_QROWS = 4
NA_KROWS = NA_QROWS + WIN_R - 1
MOE_TM = 512
VMEM_LIMIT = 56 * 1024 * 1024


def _cparams(n_axes, vmem=VMEM_LIMIT):
    return pltpu.CompilerParams(dimension_semantics=("arbitrary",) * n_axes, vmem_limit_bytes=vmem)


def _mod_row(t):
    return jnp.where(t < PROMPT_TILES, 0, 1 + (t - PROMPT_TILES) // TILES_PER_DEC)


def _mod_spec(layer, chunk, tile_off, tile_axis):
    def index(*g):
        row = _mod_row(g[tile_axis] + tile_off)
        return ((layer * N_MOD_ROWS + row) * 6 + chunk, 0, 0)
    return pl.BlockSpec((None, 1, D_MODEL), index)


def _sigmoid(x):
    return 1.0 / (1.0 + jnp.exp(-x))


def _layernorm(z, g, b):
    mu = jnp.mean(z, axis=-1, keepdims=True)
    zc = z - mu
    var = jnp.mean(zc * zc, axis=-1, keepdims=True)
    return zc * lax.rsqrt(var + LN_EPS) * g + b


def _mod_kernel(cond_ref, w_ref, b_ref, o_ref):
    cnd = cond_ref[...]
    s = cnd * _sigmoid(cnd)
    o_ref[...] = jnp.dot(s, w_ref[...], precision=HIGHEST, preferred_element_type=F32) + b_ref[...]


def _modulation(cond, ada_w, ada_b):
    out = pl.pallas_call(
        _mod_kernel,
        grid=(DEPTH, 6),
        in_specs=[
            pl.BlockSpec((N_MOD_ROWS, D_MODEL), lambda l, j: (0, 0)),
            pl.BlockSpec((None, D_MODEL, D_MODEL), lambda l, j: (l, 0, j)),
            pl.BlockSpec((None, 1, D_MODEL), lambda l, j: (l, 0, j)),
        ],
        out_specs=pl.BlockSpec((None, N_MOD_ROWS, D_MODEL), lambda l, j: (l, 0, j)),
        out_shape=jax.ShapeDtypeStruct((DEPTH, N_MOD_ROWS, 6 * D_MODEL), F32),
        compiler_params=_cparams(2),
        name="adaln_modulation",
    )(cond, ada_w, ada_b.reshape(DEPTH, 1, 6 * D_MODEL))
    return out.reshape(DEPTH * N_MOD_ROWS * 6, 1, D_MODEL)


def _rope_swap(y, g):
    lane = lax.broadcasted_iota(jnp.int32, y.shape, 1)
    first = (lane % (2 * g)) < g
    return jnp.where(first, pltpu.roll(y, LANES - g, 1), pltpu.roll(y, g, 1))


def _proj_kernel(*refs, use_mod, rope_g):
    it = iter(refs)
    x_ref = next(it)
    x = x_ref[...]
    if use_mod:
        shift_ref, scale_ref = next(it), next(it)
        x = x * (1.0 + scale_ref[...]) + shift_ref[...]
    w_ref = next(it)
    y = jnp.dot(x.astype(BF16), w_ref[...], preferred_element_type=F32)
    if rope_g:
        cos_ref, sin_ref = next(it), next(it)
        o_ref = next(it)
        cos, sin = cos_ref[...], sin_ref[...]
        for j in range(y.shape[1] // LANES):
            yb = y[:, j * LANES:(j + 1) * LANES]
            o_ref[:, j * LANES:(j + 1) * LANES] = yb * cos + _rope_swap(yb, rope_g) * sin
    else:
        o_ref = next(it)
        o_ref[...] = y


def _project(x, w_bf16, *, tile_off=0, n_tiles, mods=None, layer=0, chunks=(0, 1), rope=None, name):
    k_dim, n = w_bf16.shape
    tn = min(n, 1024)
    grid = (n // tn, n_tiles)
    in_specs = [pl.BlockSpec((TM, k_dim), lambda j, i: (i + tile_off, 0))]
    args = [x]
    if mods is not None:
        in_specs += [_mod_spec(layer, chunks[0], tile_off, 1), _mod_spec(layer, chunks[1], tile_off, 1)]
        args += [mods, mods]
    in_specs.append(pl.BlockSpec((k_dim, tn), lambda j, i: (0, j)))
    args.append(w_bf16)
    rope_g = 0
    if rope is not None:
        rope_g, cos, sin = rope
        tab = pl.BlockSpec((TM, LANES), lambda j, i: (i % TILES_PER_DEC, 0))
        in_specs += [tab, tab]
        args += [cos, sin]
    return pl.pallas_call(
        functools.partial(_proj_kernel, use_mod=mods is not None, rope_g=rope_g),
        grid=grid,
        in_specs=in_specs,
        out_specs=pl.BlockSpec((TM, tn), lambda j, i: (i, j)),
        out_shape=jax.ShapeDtypeStruct((n_tiles * TM, n), F32),
        compiler_params=_cparams(2),
        name=name,
    )(*args)


def _axial_tables(group_lanes, nf, lane_lo, lane_hi):
    del group_lanes
    t = jnp.arange(DEC_SEQ)
    lane = jnp.arange(LANES)
    rel = lane - lane_lo
    active = (lane >= lane_lo) & (lane < lane_hi)
    grp = rel // (2 * nf)
    pos = jnp.where((grp % 2 == 0)[None, :], (t // GRID_W)[:, None], (t % GRID_W)[:, None])
    inv = ROPE_BASE ** (-jnp.arange(nf, dtype=F32) / nf)
    ang = pos.astype(F32) * inv[rel % nf][None, :]
    cos = jnp.cos(ang)
    sin = jnp.sin(ang)
    sign = jnp.where((rel % (2 * nf)) < nf, -1.0, 1.0)[None, :]
    cos = jnp.where(active[None, :], cos, 1.0).astype(F32)
    sin = jnp.where(active[None, :], sin * sign, 0.0).astype(F32)
    return cos, sin


def _nt_dot(a, b):
    return lax.dot_general(a, b, (((1,), (1,)), ((), ())), preferred_element_type=F32)


def _joint_softmax_pv(scores, values):
    m = scores[0].max(axis=-1, keepdims=True)
    for s in scores[1:]:
        m = jnp.maximum(m, s.max(axis=-1, keepdims=True))
    acc, den = None, None
    for s, v in zip(scores, values):
        e = jnp.exp(s - m)
        d = e.sum(axis=-1, keepdims=True)
        o = jnp.dot(e.astype(BF16), v, preferred_element_type=F32)
        acc = o if acc is None else acc + o
        den = d if den is None else den + d
    return acc, den


def _joint_softmax(scores):
    m = scores[0].max(axis=-1, keepdims=True)
    for s in scores[1:]:
        m = jnp.maximum(m, s.max(axis=-1, keepdims=True))
    es = [jnp.exp(s - m) for s in scores]
    den = es[0].sum(axis=-1, keepdims=True)
    for e in es[1:]:
        den = den + e.sum(axis=-1, keepdims=True)
    return [e / den for e in es]


def _attn_kernel(*refs, mode, has_ctx, scale, lam_init):
    it = iter(refs)
    q_ref, k_ref, v_ref = next(it), next(it), next(it)
    kc_ref = vc_ref = None
    if has_ctx:
        kc_ref, vc_ref = next(it), next(it)
    if mode == "diff":
        lam_ref, subln_ref = next(it), next(it)
    o_ref = next(it)

    q = q_ref[...].astype(BF16)
    ks = [k_ref[...].astype(BF16)]
    vs = [v_ref[...].astype(BF16)]
    if has_ctx:
        ks.append(kc_ref[...].astype(BF16))
        vs.append(vc_ref[...].astype(BF16))
    tq = q.shape[0]
    lane_o = lax.broadcasted_iota(jnp.int32, (tq, LANES), 1)

    if mode == "mla":
        halves = [(q[:, :LANES], [k[:, :LANES] for k in ks]), (q[:, LANES:], [k[:, LANES:] for k in ks])]
    else:
        lane_q = lax.broadcasted_iota(jnp.int32, q.shape, 1)
        zero = jnp.zeros_like(q)
        halves = [(jnp.where(lane_q < 64, q, zero), ks), (jnp.where(lane_q >= 64, q, zero), ks)]

    if mode == "diff":
        lp = lam_ref[...]
        lam = (jnp.exp(jnp.sum(lp[0:1] * lp[1:2], axis=-1, keepdims=True))
               - jnp.exp(jnp.sum(lp[2:3] * lp[3:4], axis=-1, keepdims=True)) + lam_init)
        p1 = _joint_softmax([_nt_dot(halves[0][0], k) * scale for k in halves[0][1]])
        p2 = _joint_softmax([_nt_dot(halves[1][0], k) * scale for k in halves[1][1]])
        o = None
        for a, b, v in zip(p1, p2, vs):
            part = jnp.dot((a - lam * b).astype(BF16), v, preferred_element_type=F32)
            o = part if o is None else o + part
        o = o * lax.rsqrt(jnp.mean(o * o, axis=-1, keepdims=True) + RMS_EPS) * subln_ref[...]
        o_ref[...] = o * (1.0 - lam_init)
    else:
        outs = []
        for qh, kh in halves:
            acc, den = _joint_softmax_pv([_nt_dot(qh, k) * scale for k in kh], vs)
            outs.append(acc / den)
        o_ref[...] = jnp.where(lane_o < 64, outs[0], outs[1])


def _attention(q_arr, k_arr, v_arr, *, mode, n_batch, seq, tq, q_col, k_col, v_col, row_off=0, v_row_off=None,
               ctx=None, scale, lam=None, subln=None, lam_init=0.0, name):
    n_blocks = 8
    wq = 2 * LANES if mode == "mla" else LANES
    nqb = seq // tq
    qoff = row_off * nqb
    if v_row_off is None:
        v_row_off = row_off
    in_specs = [
        pl.BlockSpec((tq, wq), lambda b, h, i: (qoff + b * nqb + i, q_col * LANES // wq + h)),
        pl.BlockSpec((seq, wq), lambda b, h, i: (row_off + b, k_col * LANES // wq + h)),
        pl.BlockSpec((seq, LANES), lambda b, h, i: (v_row_off + b, v_col + h)),
    ]
    args = [q_arr, k_arr, v_arr]
    if ctx is not None:
        kc, vc, j, kc_col, vc_col = ctx
        in_specs += [
            pl.BlockSpec((None, None, PAST_LEN, wq), lambda b, h, i: (b, j, 0, kc_col * LANES // wq + h)),
            pl.BlockSpec((None, None, PAST_LEN, LANES), lambda b, h, i: (b, j, 0, vc_col + h)),
        ]
        args += [kc, vc]
    if mode == "diff":
        in_specs += [pl.BlockSpec((4, DIFF_HEAD_DIM), lambda b, h, i: (0, 0)),
                     pl.BlockSpec((1, LANES), lambda b, h, i: (0, 0))]
        args += [lam, subln]
    return pl.pallas_call(
        functools.partial(_attn_kernel, mode=mode, has_ctx=ctx is not None, scale=scale, lam_init=lam_init),
        grid=(n_batch, n_blocks, nqb),
        in_specs=in_specs,
        out_specs=pl.BlockSpec((tq, LANES), lambda b, h, i: (b * nqb + i, h)),
        out_shape=jax.ShapeDtypeStruct((n_batch * seq, D_MODEL), F32),
        compiler_params=_cparams(3),
        name=name,
    )(*args)


def _na_bias_tables(rpb):
    rows_n = DEC_SEQ // GRID_W
    tabs = []
    for q_row0, k_row0 in ((0, 0), (NA_QROWS, 0), (rows_n - NA_QROWS, rows_n - NA_KROWS)):
        r = q_row0 + jnp.arange(NA_QROWS)
        r0 = jnp.clip(r - WIN_R // 2, 0, rows_n - WIN_R)
        kr = k_row0 + jnp.arange(NA_KROWS)
        row_ok = (kr[None, :] >= r0[:, None]) & (kr[None, :] < r0[:, None] + WIN_R)
        row_off = jnp.clip(kr[None, :] - r[:, None] + (WIN_R - 1), 0, 2 * WIN_R - 2)
        cols = jnp.arange(GRID_W)
        c0 = jnp.clip(cols - WIN_C // 2, 0, GRID_W - WIN_C)
        col_ok = (cols[None, :] >= c0[:, None]) & (cols[None, :] < c0[:, None] + WIN_C)
        col_off = jnp.clip(cols[None, :] - cols[:, None] + (WIN_C - 1), 0, 2 * WIN_C - 2)
        b = rpb[:, row_off[:, None, :, None], col_off[None, :, None, :]]
        ok = row_ok[:, None, :, None] & col_ok[None, :, None, :]
        b = jnp.where(ok[None], b, NEG_INF)
        tabs.append(b.reshape(NA_HEADS, NA_QROWS * GRID_W, NA_KROWS * GRID_W))
    return jnp.stack(tabs).astype(F32)


def _na_latent_kernel(q_ref, k_ref, v_ref, kc_ref, vc_ref, bias_ref, o_ref, *, scale):
    rows_n = DEC_SEQ // GRID_W
    kc = kc_ref[...].astype(BF16)
    vc = vc_ref[...].astype(BF16)
    tq = NA_QROWS * GRID_W
    nk = NA_KROWS * GRID_W
    lane = lax.broadcasted_iota(jnp.int32, (tq, LANES), 1)
    for blk in range(rows_n // NA_QROWS):
        q_row0 = blk * NA_QROWS
        if blk == 0:
            kind, k_row0 = 0, 0
        elif blk == rows_n // NA_QROWS - 1:
            kind, k_row0 = 2, rows_n - NA_KROWS
        else:
            kind, k_row0 = 1, q_row0 - WIN_R // 2
        q = q_ref[q_row0 * GRID_W:q_row0 * GRID_W + tq, :].astype(BF16)
        k = k_ref[k_row0 * GRID_W:k_row0 * GRID_W + nk, :].astype(BF16)
        v = v_ref[k_row0 * GRID_W:k_row0 * GRID_W + nk, :].astype(BF16)
        zero = jnp.zeros_like(q)
        outs = []
        for half in range(2):
            qh = jnp.where((lane < 64) if half == 0 else (lane >= 64), q, zero)
            s_loc = _nt_dot(qh, k) * scale + bias_ref[kind, half]
            s_ctx = _nt_dot(qh, kc) * scale
            acc, den = _joint_softmax_pv([s_loc, s_ctx], [v, vc])
            outs.append(acc / den)
        o_ref[q_row0 * GRID_W:q_row0 * GRID_W + tq, :] = jnp.where(lane < 64, outs[0], outs[1])


def _na_latent_attention(qkv, cache_k, cache_v, j, bias):
    off = N_PROMPT // DEC_SEQ
    nb = D_MODEL // LANES
    return pl.pallas_call(
        functools.partial(_na_latent_kernel, scale=NA_HEAD_DIM ** -0.5),
        grid=(nb, DEC_BATCH),
        in_specs=[
            pl.BlockSpec((DEC_SEQ, LANES), lambda h, b: (off + b, h)),
            pl.BlockSpec((DEC_SEQ, LANES), lambda h, b: (off + b, nb + h)),
            pl.BlockSpec((DEC_SEQ, LANES), lambda h, b: (off + b, 2 * nb + h)),
            pl.BlockSpec((None, None, PAST_LEN, LANES), lambda h, b: (b, j, 0, h)),
            pl.BlockSpec((None, None, PAST_LEN, LANES), lambda h, b: (b, j, 0, h)),
            pl.BlockSpec((3, 2, NA_QROWS * GRID_W, NA_KROWS * GRID_W), lambda h, b: (0, h, 0, 0)),
        ],
        out_specs=pl.BlockSpec((DEC_SEQ, LANES), lambda h, b: (b, h)),
        out_shape=jax.ShapeDtypeStruct((N_SAMPLE, D_MODEL), F32),
        compiler_params=_cparams(2),
        name="na_latent_attention",
    )(qkv, qkv, qkv, cache_k, cache_v, bias)


def _out_ln_kernel(o_ref, w_ref, x_ref, gate_ref, g_ref, b_ref, xo_ref):
    y = jnp.dot(o_ref[...].astype(BF16), w_ref[...], preferred_element_type=F32)
    z = DEEPNORM_ALPHA * x_ref[...] + gate_ref[...] * y
    xo_ref[...] = _layernorm(z, g_ref[...], b_ref[...])


def _out_proj_ln(o, w_bf16, x, mods, layer, ln_g, ln_b):
    k_dim = w_bf16.shape[0]
    row = pl.BlockSpec((1, D_MODEL), lambda i: (0, 0))
    return pl.pallas_call(
        _out_ln_kernel,
        grid=(N_TOK // TM,),
        in_specs=[
            pl.BlockSpec((TM, k_dim), lambda i: (i, 0)),
            pl.BlockSpec((k_dim, D_MODEL), lambda i: (0, 0)),
            pl.BlockSpec((TM, D_MODEL), lambda i: (i, 0)),
            _mod_spec(layer, 2, 0, 0),
            row, row,
        ],
        out_specs=pl.BlockSpec((TM, D_MODEL), lambda i: (i, 0)),
        out_shape=jax.ShapeDtypeStruct((N_TOK, D_MODEL), F32),
        compiler_params=_cparams(1),
        name="out_proj_layernorm",
    )(o, w_bf16, x, mods, ln_g, ln_b)


def _router_kernel(x_ref, shift_ref, scale_ref, rw_ref, rb_ref, gates_ref):
    h = x_ref[...] * (1.0 + scale_ref[...]) + shift_ref[...]
    logits = jnp.dot(h, rw_ref[...], precision=HIGHEST, preferred_element_type=F32) + rb_ref[...]
    lane = lax.broadcasted_iota(jnp.int32, logits.shape, 1)
    work = logits
    top_v, top_hot = [], []
    for _ in range(TOP_K):
        m = work.max(axis=-1, keepdims=True)
        first = jnp.where(work == m, lane, N_EXPERTS).min(axis=-1, keepdims=True)
        hot = lane == first
        top_v.append(m)
        top_hot.append(hot)
        work = jnp.where(hot, -jnp.inf, work)
    es = [jnp.exp(v - top_v[0]) for v in top_v]
    den = es[0] + es[1] + es[2] + es[3]
    gates = jnp.zeros_like(logits)
    for e, hot in zip(es, top_hot):
        gates = gates + jnp.where(hot, e / den, 0.0)
    gates_ref[...] = gates


def _router(x, mods, layer, rw, rb):
    return pl.pallas_call(
        _router_kernel,
        grid=(N_TOK // TM,),
        in_specs=[
            pl.BlockSpec((TM, D_MODEL), lambda i: (i, 0)),
            _mod_spec(layer, 3, 0, 0),
            _mod_spec(layer, 4, 0, 0),
            pl.BlockSpec((None, D_MODEL, N_EXPERTS), lambda i: (layer, 0, 0)),
            pl.BlockSpec((None, 1, N_EXPERTS), lambda i: (layer, 0, 0)),
        ],
        out_specs=pl.BlockSpec((TM, N_EXPERTS), lambda i: (i, 0)),
        out_shape=jax.ShapeDtypeStruct((N_TOK, N_EXPERTS), F32),
        compiler_params=_cparams(1),
        name="router",
    )(x, mods, mods, rw, rb.reshape(DEPTH, 1, N_EXPERTS))


def _moe_dense_kernel(x_ref, shift_ref, scale_ref, gate_ref, gates_ref, w1_ref, b1_ref, w2_ref, b2_ref,
                      g_ref, b_ref, xo_ref, h_sc, acc_sc):
    e = pl.program_id(1)

    @pl.when(e == 0)
    def _():
        h_sc[...] = (x_ref[...] * (1.0 + scale_ref[...]) + shift_ref[...]).astype(BF16)
        acc_sc[...] = jnp.zeros_like(acc_sc)

    gu = jnp.dot(h_sc[...], w1_ref[...].astype(BF16), preferred_element_type=F32) + b1_ref[...]
    g = jnp.minimum(gu[:, :D_EXPERT], SWIGLU_LIMIT)
    u = jnp.clip(gu[:, D_EXPERT:], -SWIGLU_LIMIT, SWIGLU_LIMIT)
    act = (u + 1.0) * (g * _sigmoid(SWIGLU_ALPHA * g))
    y = jnp.dot(act.astype(BF16), w2_ref[...].astype(BF16), preferred_element_type=F32) + b2_ref[...]
    gates = gates_ref[...]
    lane = lax.broadcasted_iota(jnp.int32, gates.shape, 1)
    col = jnp.sum(jnp.where(lane == e, gates, 0.0), axis=-1, keepdims=True)
    acc_sc[...] += col * y

    @pl.when(e == N_EXPERTS - 1)
    def _():
        z = DEEPNORM_ALPHA * x_ref[...] + gate_ref[...] * acc_sc[...]
        xo_ref[...] = _layernorm(z, g_ref[...], b_ref[...])


def _moe_dense(x, gates, mods, layer, w1, b1, w2, b2, ln_g, ln_b):
    tpm = MOE_TM // TM

    def mspec(chunk):
        def index(i, e):
            return ((layer * N_MOD_ROWS + _mod_row(i * tpm)) * 6 + chunk, 0, 0)
        return pl.BlockSpec((None, 1, D_MODEL), index)

    row = pl.BlockSpec((1, D_MODEL), lambda i, e: (0, 0))
    return pl.pallas_call(
        _moe_dense_kernel,
        grid=(N_TOK // MOE_TM, N_EXPERTS),
        in_specs=[
            pl.BlockSpec((MOE_TM, D_MODEL), lambda i, e: (i, 0)),
            mspec(3), mspec(4), mspec(5),
            pl.BlockSpec((MOE_TM, N_EXPERTS), lambda i, e: (i, 0)),
            pl.BlockSpec((None, None, D_MODEL, 2 * D_EXPERT), lambda i, e: (layer, e, 0, 0)),
            pl.BlockSpec((None, None, 1, 2 * D_EXPERT), lambda i, e: (layer, e, 0, 0)),
            pl.BlockSpec((None, None, D_EXPERT, D_MODEL), lambda i, e: (layer, e, 0, 0)),
            pl.BlockSpec((None, None, 1, D_MODEL), lambda i, e: (layer, e, 0, 0)),
            row, row,
        ],
        out_specs=pl.BlockSpec((MOE_TM, D_MODEL), lambda i, e: (i, 0)),
        out_shape=jax.ShapeDtypeStruct((N_TOK, D_MODEL), F32),
        scratch_shapes=[pltpu.VMEM((MOE_TM, D_MODEL), BF16), pltpu.VMEM((MOE_TM, D_MODEL), F32)],
        compiler_params=_cparams(2),
        name="moe_dense",
    )(x, mods, mods, mods, gates, w1, b1.reshape(DEPTH, N_EXPERTS, 1, 2 * D_EXPERT),
      w2, b2.reshape(DEPTH, N_EXPERTS, 1, D_MODEL), ln_g, ln_b)


def _mla_norm_kernel(p_ref, qn_ref, kvn_ref, cq_ref, ckvkr_ref):
    p = p_ref[...]
    cq = p[:, :MLA_Q_RANK]
    cq_ref[...] = cq * lax.rsqrt(jnp.mean(cq * cq, axis=-1, keepdims=True) + RMS_EPS) * qn_ref[...]
    ckv = p[:, MLA_Q_RANK:MLA_Q_RANK + MLA_KV_RANK]
    ckvkr_ref[:, :MLA_KV_RANK] = (ckv * lax.rsqrt(jnp.mean(ckv * ckv, axis=-1, keepdims=True) + RMS_EPS)
                                  * kvn_ref[...])
    ckvkr_ref[:, MLA_KV_RANK:] = p[:, MLA_Q_RANK + MLA_KV_RANK:]


def _mla_norm(proj, q_norm, kv_norm):
    return pl.pallas_call(
        _mla_norm_kernel,
        grid=(N_TOK // TM,),
        in_specs=[
            pl.BlockSpec((TM, 512), lambda i: (i, 0)),
            pl.BlockSpec((1, MLA_Q_RANK), lambda i: (0, 0)),
            pl.BlockSpec((1, MLA_KV_RANK), lambda i: (0, 0)),
        ],
        out_specs=[pl.BlockSpec((TM, MLA_Q_RANK), lambda i: (i, 0)),
                   pl.BlockSpec((TM, 2 * MLA_KV_RANK), lambda i: (i, 0))],
        out_shape=[jax.ShapeDtypeStruct((N_TOK, MLA_Q_RANK), F32),
                   jax.ShapeDtypeStruct((N_TOK, 2 * MLA_KV_RANK), F32)],
        compiler_params=_cparams(1),
        name="mla_norm",
    )(proj, q_norm, kv_norm)


def _mla_weights(w_in, w_uq, w_ukv):
    pad = 512 - w_in.shape[1]
    w_in_p = jnp.pad(w_in, ((0, 0), (0, pad)))
    wq = w_uq.reshape(MLA_Q_RANK, MLA_HEADS, MLA_NOPE + MLA_ROPE)
    wq = jnp.pad(wq, ((0, 0), (0, 0), (0, LANES - MLA_NOPE - MLA_ROPE))).reshape(MLA_Q_RANK, MLA_HEADS * LANES)
    wkv = w_ukv.reshape(MLA_KV_RANK, MLA_HEADS, MLA_NOPE + MLA_V)
    wkn = jnp.pad(wkv[:, :, :MLA_NOPE], ((0, 0), (0, 0), (0, LANES - MLA_NOPE)))
    eye = jnp.pad(jnp.eye(MLA_ROPE, dtype=F32), ((0, 0), (MLA_NOPE, LANES - MLA_NOPE - MLA_ROPE)))
    wkr = jnp.broadcast_to(eye[:, None, :], (MLA_ROPE, MLA_HEADS, LANES))
    zeros = jnp.zeros((2 * MLA_KV_RANK - MLA_KV_RANK - MLA_ROPE, MLA_HEADS, LANES), F32)
    wk = jnp.concatenate([wkn, wkr, zeros], axis=0).reshape(2 * MLA_KV_RANK, MLA_HEADS * LANES)
    wv = jnp.pad(wkv[:, :, MLA_NOPE:].reshape(MLA_KV_RANK, MLA_HEADS * MLA_V), ((0, MLA_KV_RANK), (0, 0)))
    return w_in_p.astype(BF16), wq.astype(BF16), wk.astype(BF16), wv.astype(BF16)


def kernel(x_prompt, x_sample, cache_na_k, cache_na_v, cache_diff_k, cache_diff_v, cache_mla_ckv, cache_mla_kr,
           c, c_ctx, ada_w, ada_b, ln_g, ln_b, na_w_qkv, na_rpb, na_w_o, diff_w_qkv, diff_lambda, diff_subln,
           diff_w_o, mla_w_in, mla_q_norm, mla_w_uq, mla_kv_norm, mla_w_ukv, mla_w_o, router_w, router_b,
           moe_w1, moe_b1, moe_w2, moe_b2):
    x = jnp.concatenate([x_prompt.reshape(N_PROMPT, D_MODEL), x_sample.reshape(N_SAMPLE, D_MODEL)], axis=0)
    cond = jnp.concatenate([c_ctx[None, :], c, jnp.zeros((N_MOD_ROWS - 1 - DEC_BATCH, D_MODEL), F32)], axis=0)
    mods = _modulation(cond, ada_w, ada_b)

    n_na = na_w_qkv.shape[0]
    n_diff = diff_w_qkv.shape[0]
    n_mla = mla_w_in.shape[0]
    cache_na_k = cache_na_k.reshape(DEC_BATCH, n_na, PAST_LEN, D_MODEL)
    cache_na_v = cache_na_v.reshape(DEC_BATCH, n_na, PAST_LEN, D_MODEL)
    cache_diff_k = cache_diff_k.reshape(DEC_BATCH, n_diff, PAST_LEN, D_MODEL)
    cache_diff_v = cache_diff_v.reshape(DEC_BATCH, n_diff, PAST_LEN, D_MODEL)
    all_tiles = N_TOK // TM
    nb = D_MODEL // LANES
    dec_off = N_PROMPT // DEC_SEQ

    st_na_k, st_na_v, st_diff_k, st_diff_v, st_mla_ckv, st_mla_kr = [], [], [], [], [], []
    for i in range(DEPTH):
        kind, j = i % N_MIXERS, i // N_MIXERS
        if kind == 0:
            qkv = _project(x, na_w_qkv[j].astype(BF16), n_tiles=all_tiles, mods=mods, layer=i, name="na_qkv")
            st_na_k.append(qkv[:N_PROMPT, D_MODEL:2 * D_MODEL])
            st_na_v.append(qkv[:N_PROMPT, 2 * D_MODEL:])
            o_p = _attention(qkv, qkv, qkv, mode="pair", n_batch=BATCH, seq=SEQ, tq=SEQ, q_col=0, k_col=nb,
                             v_col=2 * nb, scale=NA_HEAD_DIM ** -0.5, name="na_context_attention")
            o_s = _na_latent_attention(qkv, cache_na_k, cache_na_v, j, _na_bias_tables(na_rpb[j]))
            o = jnp.concatenate([o_p, o_s], axis=0)
            w_o = na_w_o[j]
        elif kind == 1:
            lam_init = 0.8 - 0.6 * math.exp(-0.3 * i)
            w = diff_w_qkv[j].astype(BF16)
            cos, sin = _axial_tables(LANES, DIFF_HEAD_DIM // 4, 0, LANES)
            qkv_p = _project(x, w, n_tiles=PROMPT_TILES, mods=mods, layer=i, name="diff_qkv_context")
            qk_s = _project(x, w[:, :2 * D_MODEL], tile_off=PROMPT_TILES, n_tiles=SAMPLE_TILES, mods=mods,
                            layer=i, rope=(DIFF_HEAD_DIM // 4, cos, sin), name="diff_qk_latent")
            v_s = _project(x, w[:, 2 * D_MODEL:], tile_off=PROMPT_TILES, n_tiles=SAMPLE_TILES, mods=mods,
                           layer=i, name="diff_v_latent")
            st_diff_k.append(qkv_p[:, D_MODEL:2 * D_MODEL])
            st_diff_v.append(qkv_p[:, 2 * D_MODEL:])
            common = dict(mode="diff", scale=DIFF_HEAD_DIM ** -0.5, lam=diff_lambda[j],
                          subln=diff_subln[j][None, :], lam_init=lam_init)
            o_p = _attention(qkv_p, qkv_p, qkv_p, n_batch=BATCH, seq=SEQ, tq=SEQ, q_col=0, k_col=nb,
                             v_col=2 * nb, name="diff_context_attention", **common)
            o_s = _attention(qk_s, qk_s, v_s, n_batch=DEC_BATCH, seq=DEC_SEQ, tq=TM, q_col=0, k_col=nb, v_col=0,
                             ctx=(cache_diff_k, cache_diff_v, j, 0, 0), name="diff_latent_attention", **common)
            o = jnp.concatenate([o_p, o_s], axis=0)
            w_o = diff_w_o[j]
        else:
            w_in, wq, wk, wv = _mla_weights(mla_w_in[j], mla_w_uq[j], mla_w_ukv[j])
            cos, sin = _axial_tables(LANES, MLA_ROPE // 4, MLA_NOPE, MLA_NOPE + MLA_ROPE)
            rope = (MLA_ROPE // 4, cos, sin)
            proj = _project(x, w_in, n_tiles=all_tiles, mods=mods, layer=i, name="mla_down")
            cq, ckvkr = _mla_norm(proj, mla_q_norm[j][None, :], mla_kv_norm[j][None, :])
            st_mla_ckv.append(ckvkr[:N_PROMPT, :MLA_KV_RANK])
            st_mla_kr.append(ckvkr[:N_PROMPT, MLA_KV_RANK:MLA_KV_RANK + MLA_ROPE])
            q_p = _project(cq, wq, n_tiles=PROMPT_TILES, name="mla_q_context")
            k_p = _project(ckvkr, wk, n_tiles=PROMPT_TILES, name="mla_k_context")
            q_s = _project(cq, wq, tile_off=PROMPT_TILES, n_tiles=SAMPLE_TILES, rope=rope, name="mla_q_latent")
            k_s = _project(ckvkr, wk, tile_off=PROMPT_TILES, n_tiles=SAMPLE_TILES, rope=rope, name="mla_k_latent")
            v_all = _project(ckvkr, wv, n_tiles=all_tiles, name="mla_v")
            ctx_in = jnp.concatenate(
                [cache_mla_ckv[:, j], cache_mla_kr[:, j],
                 jnp.zeros((DEC_BATCH, PAST_LEN, 2 * MLA_KV_RANK - MLA_KV_RANK - MLA_ROPE), F32)],
                axis=-1).reshape(DEC_BATCH * PAST_LEN, 2 * MLA_KV_RANK)
            n_ctx_tiles = DEC_BATCH * PAST_LEN // TM
            k_c = _project(ctx_in, wk, n_tiles=n_ctx_tiles, name="mla_k_cache")
            v_c = _project(ctx_in, wv, n_tiles=n_ctx_tiles, name="mla_v_cache")
            k_c = k_c.reshape(DEC_BATCH, 1, PAST_LEN, MLA_HEADS * LANES)
            v_c = v_c.reshape(DEC_BATCH, 1, PAST_LEN, D_MODEL)
            scale = (MLA_NOPE + MLA_ROPE) ** -0.5
            o_p = _attention(q_p, k_p, v_all, mode="mla", n_batch=BATCH, seq=SEQ, tq=SEQ, q_col=0, k_col=0,
                             v_col=0, scale=scale, name="mla_context_attention")
            o_s = _attention(q_s, k_s, v_all, mode="mla", n_batch=DEC_BATCH, seq=DEC_SEQ, tq=TM, q_col=0, k_col=0,
                             v_col=0, v_row_off=dec_off, ctx=(k_c, v_c, 0, 0, 0), scale=scale,
                             name="mla_latent_attention")
            o = jnp.concatenate([o_p, o_s], axis=0)
            w_o = mla_w_o[j]

        x = _out_proj_ln(o, w_o.astype(BF16), x, mods, i, ln_g[i, 0][None, :], ln_b[i, 0][None, :])
        gates = _router(x, mods, i, router_w, router_b)
        x = _moe_dense(x, gates, mods, i, moe_w1, moe_b1, moe_w2, moe_b2, ln_g[i, 1][None, :], ln_b[i, 1][None, :])

    y_prompt = x[:N_PROMPT].reshape(BATCH, SEQ, D_MODEL)
    y_sample = x[N_PROMPT:].reshape(DEC_BATCH, DEC_SEQ, D_MODEL)

    def stack(parts, *tail):
        return jnp.stack([p.reshape(BATCH, SEQ, *tail) for p in parts], axis=1)

    return (y_prompt, y_sample,
            stack(st_na_k, NA_HEADS, NA_HEAD_DIM), stack(st_na_v, NA_HEADS, NA_HEAD_DIM),
            stack(st_diff_k, DIFF_HEADS, 2 * DIFF_HEAD_DIM), stack(st_diff_v, DIFF_HEADS, 2 * DIFF_HEAD_DIM),
            stack(st_mla_ckv, MLA_KV_RANK), stack(st_mla_kr, MLA_ROPE))
```

```python
import functools
import math

import jax
import jax.numpy as jnp
from jax import lax
from jax.experimental import pallas as pl
from jax.experimental.pallas import tpu as pltpu

F32 = jnp.float32
BF16 = jnp.bfloat16
HIGHEST = lax.Precision.HIGHEST

D_MODEL = 1024
BATCH = 32
SEQ = 256
DEPTH = 4
DEC_BATCH = 4
DEC_SEQ = 2048
PAST_LEN = 512
GRID_W = 64
N_MIXERS = 3
NA_HEADS = 16
NA_HEAD_DIM = 64
WIN_R = 8
WIN_C = 16
DIFF_HEADS = 8
DIFF_HEAD_DIM = 64
MLA_HEADS = 16
MLA_Q_RANK = 256
MLA_KV_RANK = 128
MLA_NOPE = 64
MLA_ROPE = 32
MLA_V = 64
N_EXPERTS = 32
TOP_K = 4
D_EXPERT = 1024
SWIGLU_LIMIT = 7.0
SWIGLU_ALPHA = 1.702
ROPE_BASE = 10000.0
LN_EPS = 1e-5
RMS_EPS = 1e-6
NEG_INF = -1e30
DEEPNORM_ALPHA = (2 * DEPTH) ** 0.25

LANES = 128
TM = 256
N_PROMPT = BATCH * SEQ
N_SAMPLE = DEC_BATCH * DEC_SEQ
N_TOK = N_PROMPT + N_SAMPLE
PROMPT_TILES = N_PROMPT // TM
SAMPLE_TILES = N_SAMPLE // TM
TILES_PER_DEC = DEC_SEQ // TM
N_MOD_ROWS = 8
NA_QROWS = 4
NA_KROWS = NA_QROWS + WIN_R - 1
N_CHUNKS = N_TOK // TM
MOE_PIECE = 8
MAX_CHUNK_PIECES = (TM * TOP_K + N_EXPERTS * (MOE_PIECE - 1)) // MOE_PIECE
MOE_CHUNK_PIECES = 160
MOE_CHUNK_ROWS = MOE_CHUNK_PIECES * MOE_PIECE
EXP_TM = 256
EXP_PPT = EXP_TM // MOE_PIECE
MAX_TILES = N_CHUNKS * MAX_CHUNK_PIECES // EXP_PPT + N_EXPERTS
VMEM_LIMIT = 56 * 1024 * 1024


def _cparams(n_axes, vmem=VMEM_LIMIT):
    return pltpu.CompilerParams(dimension_semantics=("arbitrary",) * n_axes, vmem_limit_bytes=vmem)


def _mod_row(t):
    return jnp.where(t < PROMPT_TILES, 0, 1 + (t - PROMPT_TILES) // TILES_PER_DEC)


def _mod_spec(layer, chunk, tile_off, tile_axis):
    def index(*g):
        row = _mod_row(g[tile_axis] + tile_off)
        return ((layer * N_MOD_ROWS + row) * 6 + chunk, 0, 0)
    return pl.BlockSpec((None, 1, D_MODEL), index)


def _sigmoid(x):
    return 1.0 / (1.0 + jnp.exp(-x))


def _layernorm(z, g, b):
    mu = jnp.mean(z, axis=-1, keepdims=True)
    zc = z - mu
    var = jnp.mean(zc * zc, axis=-1, keepdims=True)
    return zc * lax.rsqrt(var + LN_EPS) * g + b


def _mod_kernel(cond_ref, w_ref, b_ref, o_ref):
    cnd = cond_ref[...]
    s = cnd * _sigmoid(cnd)
    o_ref[...] = jnp.dot(s, w_ref[...], precision=HIGHEST, preferred_element_type=F32) + b_ref[...]


def _modulation(cond, ada_w, ada_b):
    out = pl.pallas_call(
        _mod_kernel,
        grid=(DEPTH, 6),
        in_specs=[
            pl.BlockSpec((N_MOD_ROWS, D_MODEL), lambda l, j: (0, 0)),
            pl.BlockSpec((None, D_MODEL, D_MODEL), lambda l, j: (l, 0, j)),
            pl.BlockSpec((None, 1, D_MODEL), lambda l, j: (l, 0, j)),
        ],
        out_specs=pl.BlockSpec((None, N_MOD_ROWS, D_MODEL), lambda l, j: (l, 0, j)),
        out_shape=jax.ShapeDtypeStruct((DEPTH, N_MOD_ROWS, 6 * D_MODEL), F32),
        compiler_params=_cparams(2),
        name="adaln_modulation",
    )(cond, ada_w, ada_b.reshape(DEPTH, 1, 6 * D_MODEL))
    return out.reshape(DEPTH * N_MOD_ROWS * 6, 1, D_MODEL)


def _rope_swap(y, g):
    lane = lax.broadcasted_iota(jnp.int32, y.shape, 1)
    first = (lane % (2 * g)) < g
    return jnp.where(first, pltpu.roll(y, LANES - g, 1), pltpu.roll(y, g, 1))


def _proj_kernel(*refs, use_mod, rope_g):
    it = iter(refs)
    x_ref = next(it)
    x = x_ref[...]
    if use_mod:
        shift_ref, scale_ref = next(it), next(it)
        x = x * (1.0 + scale_ref[...]) + shift_ref[...]
    w_ref = next(it)
    y = jnp.dot(x.astype(BF16), w_ref[...], preferred_element_type=F32)
    if rope_g:
        cos_ref, sin_ref = next(it), next(it)
        o_ref = next(it)
        cos, sin = cos_ref[...], sin_ref[...]
        for j in range(y.shape[1] // LANES):
            yb = y[:, j * LANES:(j + 1) * LANES]
            o_ref[:, j * LANES:(j + 1) * LANES] = yb * cos + _rope_swap(yb, rope_g) * sin
    else:
        o_ref = next(it)
        o_ref[...] = y


def _project(x, w_bf16, *, tile_off=0, n_tiles, mods=None, layer=0, chunks=(0, 1), rope=None, name):
    k_dim, n = w_bf16.shape
    tn = min(n, 1024)
    grid = (n // tn, n_tiles)
    in_specs = [pl.BlockSpec((TM, k_dim), lambda j, i: (i + tile_off, 0))]
    args = [x]
    if mods is not None:
        in_specs += [_mod_spec(layer, chunks[0], tile_off, 1), _mod_spec(layer, chunks[1], tile_off, 1)]
        args += [mods, mods]
    in_specs.append(pl.BlockSpec((k_dim, tn), lambda j, i: (0, j)))
    args.append(w_bf16)
    rope_g = 0
    if rope is not None:
        rope_g, cos, sin = rope
        tab = pl.BlockSpec((TM, LANES), lambda j, i: (i % TILES_PER_DEC, 0))
        in_specs += [tab, tab]
        args += [cos, sin]
    return pl.pallas_call(
        functools.partial(_proj_kernel, use_mod=mods is not None, rope_g=rope_g),
        grid=grid,
        in_specs=in_specs,
        out_specs=pl.BlockSpec((TM, tn), lambda j, i: (i, j)),
        out_shape=jax.ShapeDtypeStruct((n_tiles * TM, n), F32),
        compiler_params=_cparams(2),
        name=name,
    )(*args)


def _axial_tables(group_lanes, nf, lane_lo, lane_hi):
    del group_lanes
    t = jnp.arange(DEC_SEQ)
    lane = jnp.arange(LANES)
    rel = lane - lane_lo
    active = (lane >= lane_lo) & (lane < lane_hi)
    grp = rel // (2 * nf)
    pos = jnp.where((grp % 2 == 0)[None, :], (t // GRID_W)[:, None], (t % GRID_W)[:, None])
    inv = ROPE_BASE ** (-jnp.arange(nf, dtype=F32) / nf)
    ang = pos.astype(F32) * inv[rel % nf][None, :]
    cos = jnp.cos(ang)
    sin = jnp.sin(ang)
    sign = jnp.where((rel % (2 * nf)) < nf, -1.0, 1.0)[None, :]
    cos = jnp.where(active[None, :], cos, 1.0).astype(F32)
    sin = jnp.where(active[None, :], sin * sign, 0.0).astype(F32)
    return cos, sin


def _nt_dot(a, b):
    return lax.dot_general(a, b, (((1,), (1,)), ((), ())), preferred_element_type=F32)


def _joint_softmax_pv(scores, values):
    m = scores[0].max(axis=-1, keepdims=True)
    for s in scores[1:]:
        m = jnp.maximum(m, s.max(axis=-1, keepdims=True))
    acc, den = None, None
    for s, v in zip(scores, values):
        e = jnp.exp(s - m)
        d = e.sum(axis=-1, keepdims=True)
        o = jnp.dot(e.astype(BF16), v, preferred_element_type=F32)
        acc = o if acc is None else acc + o
        den = d if den is None else den + d
    return acc, den


def _joint_softmax(scores):
    m = scores[0].max(axis=-1, keepdims=True)
    for s in scores[1:]:
        m = jnp.maximum(m, s.max(axis=-1, keepdims=True))
    es = [jnp.exp(s - m) for s in scores]
    den = es[0].sum(axis=-1, keepdims=True)
    for e in es[1:]:
        den = den + e.sum(axis=-1, keepdims=True)
    return [e / den for e in es]


def _attn_kernel(*refs, mode, has_ctx, scale, lam_init):
    it = iter(refs)
    q_ref, k_ref, v_ref = next(it), next(it), next(it)
    kc_ref = vc_ref = None
    if has_ctx:
        kc_ref, vc_ref = next(it), next(it)
    if mode == "diff":
        lam_ref, subln_ref = next(it), next(it)
    o_ref = next(it)

    q = q_ref[...].astype(BF16)
    ks = [k_ref[...].astype(BF16)]
    vs = [v_ref[...].astype(BF16)]
    if has_ctx:
        ks.append(kc_ref[...].astype(BF16))
        vs.append(vc_ref[...].astype(BF16))
    tq = q.shape[0]
    lane_o = lax.broadcasted_iota(jnp.int32, (tq, LANES), 1)

    if mode == "mla":
        halves = [(q[:, :LANES], [k[:, :LANES] for k in ks]), (q[:, LANES:], [k[:, LANES:] for k in ks])]
    else:
        lane_q = lax.broadcasted_iota(jnp.int32, q.shape, 1)
        zero = jnp.zeros_like(q)
        halves = [(jnp.where(lane_q < 64, q, zero), ks), (jnp.where(lane_q >= 64, q, zero), ks)]

    if mode == "diff":
        lp = lam_ref[...]
        lam = (jnp.exp(jnp.sum(lp[0:1] * lp[1:2], axis=-1, keepdims=True))
               - jnp.exp(jnp.sum(lp[2:3] * lp[3:4], axis=-1, keepdims=True)) + lam_init)
        p1 = _joint_softmax([_nt_dot(halves[0][0], k) * scale for k in halves[0][1]])
        p2 = _joint_softmax([_nt_dot(halves[1][0], k) * scale for k in halves[1][1]])
        o = None
        for a, b, v in zip(p1, p2, vs):
            part = jnp.dot((a - lam * b).astype(BF16), v, preferred_element_type=F32)
            o = part if o is None else o + part
        o = o * lax.rsqrt(jnp.mean(o * o, axis=-1, keepdims=True) + RMS_EPS) * subln_ref[...]
        o_ref[...] = o * (1.0 - lam_init)
    else:
        outs = []
        for qh, kh in halves:
            acc, den = _joint_softmax_pv([_nt_dot(qh, k) * scale for k in kh], vs)
            outs.append(acc / den)
        o_ref[...] = jnp.where(lane_o < 64, outs[0], outs[1])


def _attention(q_arr, k_arr, v_arr, *, mode, n_batch, seq, tq, q_col, k_col, v_col, row_off=0, v_row_off=None,
               ctx=None, scale, lam=None, subln=None, lam_init=0.0, name):
    n_blocks = 8
    wq = 2 * LANES if mode == "mla" else LANES
    nqb = seq // tq
    qoff = row_off * nqb
    if v_row_off is None:
        v_row_off = row_off
    in_specs = [
        pl.BlockSpec((tq, wq), lambda b, h, i: (qoff + b * nqb + i, q_col * LANES // wq + h)),
        pl.BlockSpec((seq, wq), lambda b, h, i: (row_off + b, k_col * LANES // wq + h)),
        pl.BlockSpec((seq, LANES), lambda b, h, i: (v_row_off + b, v_col + h)),
    ]
    args = [q_arr, k_arr, v_arr]
    if ctx is not None:
        kc, vc, j, kc_col, vc_col = ctx
        in_specs += [
            pl.BlockSpec((None, None, PAST_LEN, wq), lambda b, h, i: (b, j, 0, kc_col * LANES // wq + h)),
            pl.BlockSpec((None, None, PAST_LEN, LANES), lambda b, h, i: (b, j, 0, vc_col + h)),
        ]
        args += [kc, vc]
    if mode == "diff":
        in_specs += [pl.BlockSpec((4, DIFF_HEAD_DIM), lambda b, h, i: (0, 0)),
                     pl.BlockSpec((1, LANES), lambda b, h, i: (0, 0))]
        args += [lam, subln]
    return pl.pallas_call(
        functools.partial(_attn_kernel, mode=mode, has_ctx=ctx is not None, scale=scale, lam_init=lam_init),
        grid=(n_batch, n_blocks, nqb),
        in_specs=in_specs,
        out_specs=pl.BlockSpec((tq, LANES), lambda b, h, i: (b * nqb + i, h)),
        out_shape=jax.ShapeDtypeStruct((n_batch * seq, D_MODEL), F32),
        compiler_params=_cparams(3),
        name=name,
    )(*args)


def _na_bias_tables(rpb):
    rows_n = DEC_SEQ // GRID_W
    tabs = []
    for q_row0, k_row0 in ((0, 0), (NA_QROWS, 0), (rows_n - NA_QROWS, rows_n - NA_KROWS)):
        r = q_row0 + jnp.arange(NA_QROWS)
        r0 = jnp.clip(r - WIN_R // 2, 0, rows_n - WIN_R)
        kr = k_row0 + jnp.arange(NA_KROWS)
        row_ok = (kr[None, :] >= r0[:, None]) & (kr[None, :] < r0[:, None] + WIN_R)
        row_off = jnp.clip(kr[None, :] - r[:, None] + (WIN_R - 1), 0, 2 * WIN_R - 2)
        cols = jnp.arange(GRID_W)
        c0 = jnp.clip(cols - WIN_C // 2, 0, GRID_W - WIN_C)
        col_ok = (cols[None, :] >= c0[:, None]) & (cols[None, :] < c0[:, None] + WIN_C)
        col_off = jnp.clip(cols[None, :] - cols[:, None] + (WIN_C - 1), 0, 2 * WIN_C - 2)
        hot_r = (row_off[:, :, None] == jnp.arange(2 * WIN_R - 1)).astype(F32)
        hot_c = (col_off[:, :, None] == jnp.arange(2 * WIN_C - 1)).astype(F32)
        t1 = jnp.einsum("ija,hab->hijb", hot_r, rpb, precision=HIGHEST)
        b = jnp.einsum("hijb,ckb->hicjk", t1, hot_c, precision=HIGHEST)
        ok = row_ok[:, None, :, None] & col_ok[None, :, None, :]
        b = jnp.where(ok[None], b, NEG_INF)
        tabs.append(b.reshape(NA_HEADS, NA_QROWS * GRID_W, NA_KROWS * GRID_W))
    return jnp.stack(tabs).astype(F32)


def _na_latent_kernel(q_ref, k_ref, v_ref, kc_ref, vc_ref, bias_ref, o_ref, *, scale):
    rows_n = DEC_SEQ // GRID_W
    kc = kc_ref[...].astype(BF16)
    vc = vc_ref[...].astype(BF16)
    tq = NA_QROWS * GRID_W
    nk = NA_KROWS * GRID_W
    lane = lax.broadcasted_iota(jnp.int32, (tq, LANES), 1)
    for blk in range(rows_n // NA_QROWS):
        q_row0 = blk * NA_QROWS
        if blk == 0:
            kind, k_row0 = 0, 0
        elif blk == rows_n // NA_QROWS - 1:
            kind, k_row0 = 2, rows_n - NA_KROWS
        else:
            kind, k_row0 = 1, q_row0 - WIN_R // 2
        q = q_ref[q_row0 * GRID_W:q_row0 * GRID_W + tq, :].astype(BF16)
        k = k_ref[k_row0 * GRID_W:k_row0 * GRID_W + nk, :].astype(BF16)
        v = v_ref[k_row0 * GRID_W:k_row0 * GRID_W + nk, :].astype(BF16)
        zero = jnp.zeros_like(q)
        outs = []
        for half in range(2):
            qh = jnp.where((lane < 64) if half == 0 else (lane >= 64), q, zero)
            s_loc = _nt_dot(qh, k) * scale + bias_ref[kind, half]
            s_ctx = _nt_dot(qh, kc) * scale
            acc, den = _joint_softmax_pv([s_loc, s_ctx], [v, vc])
            outs.append(acc / den)
        o_ref[q_row0 * GRID_W:q_row0 * GRID_W + tq, :] = jnp.where(lane < 64, outs[0], outs[1])


def _na_latent_attention(qkv, cache_k, cache_v, j, bias):
    off = N_PROMPT // DEC_SEQ
    nb = D_MODEL // LANES
    return pl.pallas_call(
        functools.partial(_na_latent_kernel, scale=NA_HEAD_DIM ** -0.5),
        grid=(nb, DEC_BATCH),
        in_specs=[
            pl.BlockSpec((DEC_SEQ, LANES), lambda h, b: (off + b, h)),
            pl.BlockSpec((DEC_SEQ, LANES), lambda h, b: (off + b, nb + h)),
            pl.BlockSpec((DEC_SEQ, LANES), lambda h, b: (off + b, 2 * nb + h)),
            pl.BlockSpec((None, None, PAST_LEN, LANES), lambda h, b: (b, j, 0, h)),
            pl.BlockSpec((None, None, PAST_LEN, LANES), lambda h, b: (b, j, 0, h)),
            pl.BlockSpec((3, 2, NA_QROWS * GRID_W, NA_KROWS * GRID_W), lambda h, b: (0, h, 0, 0)),
        ],
        out_specs=pl.BlockSpec((DEC_SEQ, LANES), lambda h, b: (b, h)),
        out_shape=jax.ShapeDtypeStruct((N_SAMPLE, D_MODEL), F32),
        compiler_params=_cparams(2),
        name="na_latent_attention",
    )(qkv, qkv, qkv, cache_k, cache_v, bias)


def _out_ln_kernel(o_ref, w_ref, x_ref, gate_ref, g_ref, b_ref, xo_ref):
    y = jnp.dot(o_ref[...].astype(BF16), w_ref[...], preferred_element_type=F32)
    z = DEEPNORM_ALPHA * x_ref[...] + gate_ref[...] * y
    xo_ref[...] = _layernorm(z, g_ref[...], b_ref[...])


def _out_proj_ln(o, w_bf16, x, mods, layer, ln_g, ln_b):
    k_dim = w_bf16.shape[0]
    row = pl.BlockSpec((1, D_MODEL), lambda i: (0, 0))
    return pl.pallas_call(
        _out_ln_kernel,
        grid=(N_TOK // TM,),
        in_specs=[
            pl.BlockSpec((TM, k_dim), lambda i: (i, 0)),
            pl.BlockSpec((k_dim, D_MODEL), lambda i: (0, 0)),
            pl.BlockSpec((TM, D_MODEL), lambda i: (i, 0)),
            _mod_spec(layer, 2, 0, 0),
            row, row,
        ],
        out_specs=pl.BlockSpec((TM, D_MODEL), lambda i: (i, 0)),
        out_shape=jax.ShapeDtypeStruct((N_TOK, D_MODEL), F32),
        compiler_params=_cparams(1),
        name="out_proj_layernorm",
    )(o, w_bf16, x, mods, ln_g, ln_b)


def _route_kernel(x_ref, shift_ref, scale_ref, rw_ref, rb_ref, xs_ref, meta_ref, cnt_ref):
    h = x_ref[...] * (1.0 + scale_ref[...]) + shift_ref[...]
    logits = jnp.dot(h, rw_ref[...], precision=HIGHEST, preferred_element_type=F32) + rb_ref[...]
    lane = lax.broadcasted_iota(jnp.int32, logits.shape, 1)
    work = logits
    top_v, hots = [], []
    for _ in range(TOP_K):
        m = work.max(axis=-1, keepdims=True)
        first = jnp.where(work == m, lane, N_EXPERTS).min(axis=-1, keepdims=True)
        hot = lane == first
        top_v.append(m)
        hots.append(jnp.where(hot, 1.0, 0.0))
        work = jnp.where(hot, -jnp.inf, work)
    es = [jnp.exp(v - top_v[0]) for v in top_v]
    den = es[0] + es[1] + es[2] + es[3]
    gates = [e / den for e in es]

    cnts = [hf.sum(axis=0, keepdims=True) for hf in hots]
    n_e = cnts[0] + cnts[1] + cnts[2] + cnts[3]
    pieces = jnp.floor((n_e + (MOE_PIECE - 1)) * (1.0 / MOE_PIECE))
    er = lax.broadcasted_iota(jnp.int32, (N_EXPERTS, N_EXPERTS), 0)
    ec = lax.broadcasted_iota(jnp.int32, (N_EXPERTS, N_EXPERTS), 1)
    upper = jnp.where(er < ec, 1.0, 0.0).astype(BF16)
    poff = jnp.dot(jnp.broadcast_to(pieces, (8, N_EXPERTS)).astype(BF16), upper,
                   preferred_element_type=F32)[0:1]

    tr = lax.broadcasted_iota(jnp.int32, (TM, TM), 0)
    tc = lax.broadcasted_iota(jnp.int32, (TM, TM), 1)
    lower = jnp.where(tc < tr, 1.0, 0.0).astype(BF16)
    run = poff * MOE_PIECE
    slots = []
    for k in range(TOP_K):
        prefix = jnp.dot(lower, hots[k].astype(BF16), preferred_element_type=F32)
        slots.append(jnp.sum(hots[k] * (run + prefix), axis=-1, keepdims=True))
        run = run + cnts[k]

    lane_w = lax.broadcasted_iota(jnp.int32, (TM, LANES), 1)
    meta = jnp.full((TM, LANES), -1.0, F32)
    for k in range(TOP_K):
        meta = jnp.where(lane_w == k, slots[k], meta)
        meta = jnp.where(lane_w == TOP_K + k, gates[k], meta)
    meta_ref[...] = meta[:, :2 * TOP_K]
    cnt_ref[...] = pieces

    slot_rows = meta.T
    srow = lax.broadcasted_iota(jnp.int32, (MOE_CHUNK_ROWS, TM), 0).astype(F32)
    p = jnp.zeros((MOE_CHUNK_ROWS, TM), F32)
    for k in range(TOP_K):
        p = p + jnp.where(srow == slot_rows[k:k + 1, :], 1.0, 0.0)
    xs_ref[...] = jnp.dot(p.astype(BF16), h.astype(BF16), preferred_element_type=F32)


def _route(x, mods, layer, rw, rb):
    return pl.pallas_call(
        _route_kernel,
        grid=(N_CHUNKS,),
        in_specs=[
            pl.BlockSpec((TM, D_MODEL), lambda i: (i, 0)),
            _mod_spec(layer, 3, 0, 0),
            _mod_spec(layer, 4, 0, 0),
            pl.BlockSpec((None, D_MODEL, N_EXPERTS), lambda i: (layer, 0, 0)),
            pl.BlockSpec((None, 1, N_EXPERTS), lambda i: (layer, 0, 0)),
        ],
        out_specs=[
            pl.BlockSpec((MOE_CHUNK_ROWS, D_MODEL), lambda i: (i, 0)),
            pl.BlockSpec((TM, 2 * TOP_K), lambda i: (i, 0)),
            pl.BlockSpec((None, 1, N_EXPERTS), lambda i: (i, 0, 0)),
        ],
        out_shape=[
            jax.ShapeDtypeStruct((N_CHUNKS * MOE_CHUNK_ROWS, D_MODEL), F32),
            jax.ShapeDtypeStruct((N_TOK, 2 * TOP_K), F32),
            jax.ShapeDtypeStruct((N_CHUNKS, 1, N_EXPERTS), F32),
        ],
        compiler_params=_cparams(1),
        name="moe_route",
    )(x, mods, mods, rw, rb.reshape(DEPTH, 1, N_EXPERTS))


def _moe_schedule(pieces):
    npe = pieces.sum(axis=0)
    tiles_e = (npe + EXP_PPT - 1) // EXP_PPT
    tile_end = jnp.cumsum(tiles_e)
    tile_start = tile_end - tiles_e
    g = jnp.arange(MAX_TILES, dtype=jnp.int32)
    te = jnp.minimum(jnp.searchsorted(tile_end, g, side="right"), N_EXPERTS - 1).astype(jnp.int32)
    nv = jnp.clip(npe[te] - (g - tile_start[te]) * EXP_PPT, 0, EXP_PPT).astype(jnp.int32)
    seg_len = pieces.T
    seg_base = tile_start[:, None] * EXP_PPT + jnp.cumsum(seg_len, axis=1) - seg_len
    poff = jnp.cumsum(pieces, axis=1) - pieces
    seg_pos = (jnp.arange(N_CHUNKS, dtype=jnp.int32)[:, None] * MOE_CHUNK_PIECES + poff).T
    fb, fl, fp = seg_base.reshape(-1), seg_len.reshape(-1), seg_pos.reshape(-1)
    pos = jnp.arange(MAX_TILES * EXP_PPT, dtype=jnp.int32)
    idx = jnp.clip(jnp.searchsorted(fb, pos, side="right") - 1, 0, fb.shape[0] - 1)
    off = pos - fb[idx]
    valid = (off < fl[idx]).reshape(MAX_TILES, EXP_PPT)
    src = jnp.where(valid, (fp[idx] + off).reshape(MAX_TILES, EXP_PPT), 0)
    src = jnp.where(valid, src, src[:, :1]).reshape(-1).astype(jnp.int32)
    return te, nv, tile_end[-1:].astype(jnp.int32), src


def _expert_kernel(te_ref, nv_ref, nt_ref, src_ref, xs_hbm, w1_ref, b1_ref, w2_ref, b2_ref, out_hbm,
                   xbuf, ybuf, w1s, w2s, sem_in, sem_out):
    g = pl.program_id(0)
    nt = nt_ref[0]
    slot = g % 2

    def in_copy(tile, i, s):
        row = pl.multiple_of(src_ref[tile * EXP_PPT + i] * MOE_PIECE, MOE_PIECE)
        return pltpu.make_async_copy(xs_hbm.at[pl.ds(row, MOE_PIECE), :],
                                     xbuf.at[s, pl.ds(i * MOE_PIECE, MOE_PIECE), :], sem_in.at[s])

    def out_copy(tile, i, s):
        row = pl.multiple_of(src_ref[tile * EXP_PPT + i] * MOE_PIECE, MOE_PIECE)
        return pltpu.make_async_copy(ybuf.at[s, pl.ds(i * MOE_PIECE, MOE_PIECE), :],
                                     out_hbm.at[pl.ds(row, MOE_PIECE), :], sem_out.at[s])

    def gather(tile, s, wait):
        for i in range(EXP_PPT):
            cp = in_copy(tile, i, s)
            cp.wait() if wait else cp.start()

    def scatter(tile, s, wait):
        for i in range(EXP_PPT):
            @pl.when(i < nv_ref[tile])
            def _():
                cp = out_copy(tile, i, s)
                cp.wait() if wait else cp.start()

    @pl.when(g < nt)
    def _():
        @pl.when(g == 0)
        def _():
            gather(0, 0, False)

        @pl.when(g + 1 < nt)
        def _():
            gather(g + 1, 1 - slot, False)

        gather(g, slot, True)

        @pl.when(g >= 2)
        def _():
            scatter(g - 2, slot, True)

        @pl.when((g == 0) | (te_ref[g] != te_ref[jnp.maximum(g - 1, 0)]))
        def _():
            w1s[...] = w1_ref[...].astype(BF16)
            w2s[...] = w2_ref[...].astype(BF16)

        gu = jnp.dot(xbuf[slot].astype(BF16), w1s[...], preferred_element_type=F32) + b1_ref[...]
        gl = jnp.minimum(gu[:, :D_EXPERT], SWIGLU_LIMIT)
        ul = jnp.clip(gu[:, D_EXPERT:], -SWIGLU_LIMIT, SWIGLU_LIMIT)
        act = (ul + 1.0) * (gl * _sigmoid(SWIGLU_ALPHA * gl))
        ybuf[slot] = jnp.dot(act.astype(BF16), w2s[...], preferred_element_type=F32) + b2_ref[...]
        scatter(g, slot, False)

        @pl.when(g == nt - 1)
        def _():
            @pl.when(g >= 1)
            def _():
                scatter(g - 1, 1 - slot, True)
            scatter(g, slot, True)


def _experts(xs, te, nv, nt, src, layer, w1, b1, w2, b2):
    grid_spec = pltpu.PrefetchScalarGridSpec(
        num_scalar_prefetch=4,
        grid=(MAX_TILES,),
        in_specs=[
            pl.BlockSpec(memory_space=pl.ANY),
            pl.BlockSpec((None, None, D_MODEL, 2 * D_EXPERT), lambda g, te, nv, nt, src: (layer, te[g], 0, 0)),
            pl.BlockSpec((None, None, 1, 2 * D_EXPERT), lambda g, te, nv, nt, src: (layer, te[g], 0, 0)),
            pl.BlockSpec((None, None, D_EXPERT, D_MODEL), lambda g, te, nv, nt, src: (layer, te[g], 0, 0)),
            pl.BlockSpec((None, None, 1, D_MODEL), lambda g, te, nv, nt, src: (layer, te[g], 0, 0)),
        ],
        out_specs=pl.BlockSpec(memory_space=pl.ANY),
        scratch_shapes=[
            pltpu.VMEM((2, EXP_TM, D_MODEL), F32),
            pltpu.VMEM((2, EXP_TM, D_MODEL), F32),
            pltpu.VMEM((D_MODEL, 2 * D_EXPERT), BF16),
            pltpu.VMEM((D_EXPERT, D_MODEL), BF16),
            pltpu.SemaphoreType.DMA((2,)),
            pltpu.SemaphoreType.DMA((2,)),
        ],
    )
    return pl.pallas_call(
        _expert_kernel,
        grid_spec=grid_spec,
        out_shape=jax.ShapeDtypeStruct(xs.shape, F32),
        input_output_aliases={4: 0},
        compiler_params=_cparams(1),
        name="moe_experts",
    )(te, nv, nt, src, xs, w1, b1.reshape(DEPTH, N_EXPERTS, 1, 2 * D_EXPERT),
      w2, b2.reshape(DEPTH, N_EXPERTS, 1, D_MODEL))


def _combine_kernel(ys_ref, meta_ref, x_ref, gate_ref, g_ref, b_ref, xo_ref):
    meta = meta_ref[...]
    col = lax.broadcasted_iota(jnp.int32, (TM, MOE_CHUNK_ROWS), 1).astype(F32)
    gmat = jnp.zeros((TM, MOE_CHUNK_ROWS), F32)
    for k in range(TOP_K):
        gmat = gmat + jnp.where(col == meta[:, k:k + 1], meta[:, TOP_K + k:TOP_K + k + 1], 0.0)
    y = jnp.dot(gmat.astype(BF16), ys_ref[...].astype(BF16), preferred_element_type=F32)
    z = DEEPNORM_ALPHA * x_ref[...] + gate_ref[...] * y
    xo_ref[...] = _layernorm(z, g_ref[...], b_ref[...])


def _combine_ln(ys, meta, x, mods, layer, ln_g, ln_b):
    row = pl.BlockSpec((1, D_MODEL), lambda i: (0, 0))
    return pl.pallas_call(
        _combine_kernel,
        grid=(N_CHUNKS,),
        in_specs=[
            pl.BlockSpec((MOE_CHUNK_ROWS, D_MODEL), lambda i: (i, 0)),
            pl.BlockSpec((TM, 2 * TOP_K), lambda i: (i, 0)),
            pl.BlockSpec((TM, D_MODEL), lambda i: (i, 0)),
            _mod_spec(layer, 5, 0, 0),
            row, row,
        ],
        out_specs=pl.BlockSpec((TM, D_MODEL), lambda i: (i, 0)),
        out_shape=jax.ShapeDtypeStruct((N_TOK, D_MODEL), F32),
        compiler_params=_cparams(1),
        name="moe_combine_layernorm",
    )(ys, meta, x, mods, ln_g, ln_b)


def _moe(x, mods, layer, rw, rb, w1, b1, w2, b2, ln_g, ln_b):
    xs, meta, pieces = _route(x, mods, layer, rw, rb)
    te, nv, nt, src = _moe_schedule(pieces.reshape(N_CHUNKS, N_EXPERTS).astype(jnp.int32))
    ys = _experts(xs, te, nv, nt, src, layer, w1, b1, w2, b2)
    return _combine_ln(ys, meta, x, mods, layer, ln_g, ln_b)


def _mla_norm_kernel(p_ref, qn_ref, kvn_ref, cq_ref, ckvkr_ref):
    p = p_ref[...]
    cq = p[:, :MLA_Q_RANK]
    cq_ref[...] = cq * lax.rsqrt(jnp.mean(cq * cq, axis=-1, keepdims=True) + RMS_EPS) * qn_ref[...]
    ckv = p[:, MLA_Q_RANK:MLA_Q_RANK + MLA_KV_RANK]
    ckvkr_ref[:, :MLA_KV_RANK] = (ckv * lax.rsqrt(jnp.mean(ckv * ckv, axis=-1, keepdims=True) + RMS_EPS)
                                  * kvn_ref[...])
    ckvkr_ref[:, MLA_KV_RANK:] = p[:, MLA_Q_RANK + MLA_KV_RANK:]


def _mla_norm(proj, q_norm, kv_norm):
    return pl.pallas_call(
        _mla_norm_kernel,
        grid=(N_TOK // TM,),
        in_specs=[
            pl.BlockSpec((TM, 512), lambda i: (i, 0)),
            pl.BlockSpec((1, MLA_Q_RANK), lambda i: (0, 0)),
            pl.BlockSpec((1, MLA_KV_RANK), lambda i: (0, 0)),
        ],
        out_specs=[pl.BlockSpec((TM, MLA_Q_RANK), lambda i: (i, 0)),
                   pl.BlockSpec((TM, 2 * MLA_KV_RANK), lambda i: (i, 0))],
        out_shape=[jax.ShapeDtypeStruct((N_TOK, MLA_Q_RANK), F32),
                   jax.ShapeDtypeStruct((N_TOK, 2 * MLA_KV_RANK), F32)],
        compiler_params=_cparams(1),
        name="mla_norm",
    )(proj, q_norm, kv_norm)


def _mla_weights(w_in, w_uq, w_ukv):
    pad = 512 - w_in.shape[1]
    w_in_p = jnp.pad(w_in, ((0, 0), (0, pad)))
    wq = w_uq.reshape(MLA_Q_RANK, MLA_HEADS, MLA_NOPE + MLA_ROPE)
    wq = jnp.pad(wq, ((0, 0), (0, 0), (0, LANES - MLA_NOPE - MLA_ROPE))).reshape(MLA_Q_RANK, MLA_HEADS * LANES)
    wkv = w_ukv.reshape(MLA_KV_RANK, MLA_HEADS, MLA_NOPE + MLA_V)
    wkn = jnp.pad(wkv[:, :, :MLA_NOPE], ((0, 0), (0, 0), (0, LANES - MLA_NOPE)))
    eye = jnp.pad(jnp.eye(MLA_ROPE, dtype=F32), ((0, 0), (MLA_NOPE, LANES - MLA_NOPE - MLA_ROPE)))
    wkr = jnp.broadcast_to(eye[:, None, :], (MLA_ROPE, MLA_HEADS, LANES))
    zeros = jnp.zeros((2 * MLA_KV_RANK - MLA_KV_RANK - MLA_ROPE, MLA_HEADS, LANES), F32)
    wk = jnp.concatenate([wkn, wkr, zeros], axis=0).reshape(2 * MLA_KV_RANK, MLA_HEADS * LANES)
    wv = jnp.pad(wkv[:, :, MLA_NOPE:].reshape(MLA_KV_RANK, MLA_HEADS * MLA_V), ((0, MLA_KV_RANK), (0, 0)))
    return w_in_p.astype(BF16), wq.astype(BF16), wk.astype(BF16), wv.astype(BF16)


def kernel(x_prompt, x_sample, cache_na_k, cache_na_v, cache_diff_k, cache_diff_v, cache_mla_ckv, cache_mla_kr,
           c, c_ctx, ada_w, ada_b, ln_g, ln_b, na_w_qkv, na_rpb, na_w_o, diff_w_qkv, diff_lambda, diff_subln,
           diff_w_o, mla_w_in, mla_q_norm, mla_w_uq, mla_kv_norm, mla_w_ukv, mla_w_o, router_w, router_b,
           moe_w1, moe_b1, moe_w2, moe_b2):
    x = jnp.concatenate([x_prompt.reshape(N_PROMPT, D_MODEL), x_sample.reshape(N_SAMPLE, D_MODEL)], axis=0)
    cond = jnp.concatenate([c_ctx[None, :], c, jnp.zeros((N_MOD_ROWS - 1 - DEC_BATCH, D_MODEL), F32)], axis=0)
    mods = _modulation(cond, ada_w, ada_b)

    n_na = na_w_qkv.shape[0]
    n_diff = diff_w_qkv.shape[0]
    n_mla = mla_w_in.shape[0]
    cache_na_k = cache_na_k.reshape(DEC_BATCH, n_na, PAST_LEN, D_MODEL)
    cache_na_v = cache_na_v.reshape(DEC_BATCH, n_na, PAST_LEN, D_MODEL)
    cache_diff_k = cache_diff_k.reshape(DEC_BATCH, n_diff, PAST_LEN, D_MODEL)
    cache_diff_v = cache_diff_v.reshape(DEC_BATCH, n_diff, PAST_LEN, D_MODEL)
    all_tiles = N_TOK // TM
    nb = D_MODEL // LANES
    dec_off = N_PROMPT // DEC_SEQ

    st_na_k, st_na_v, st_diff_k, st_diff_v, st_mla_ckv, st_mla_kr = [], [], [], [], [], []
    for i in range(DEPTH):
        kind, j = i % N_MIXERS, i // N_MIXERS
        if kind == 0:
            qkv = _project(x, na_w_qkv[j].astype(BF16), n_tiles=all_tiles, mods=mods, layer=i, name="na_qkv")
            st_na_k.append(qkv[:N_PROMPT, D_MODEL:2 * D_MODEL])
            st_na_v.append(qkv[:N_PROMPT, 2 * D_MODEL:])
            o_p = _attention(qkv, qkv, qkv, mode="pair", n_batch=BATCH, seq=SEQ, tq=SEQ, q_col=0, k_col=nb,
                             v_col=2 * nb, scale=NA_HEAD_DIM ** -0.5, name="na_context_attention")
            o_s = _na_latent_attention(qkv, cache_na_k, cache_na_v, j, _na_bias_tables(na_rpb[j]))
            o = jnp.concatenate([o_p, o_s], axis=0)
            w_o = na_w_o[j]
        elif kind == 1:
            lam_init = 0.8 - 0.6 * math.exp(-0.3 * i)
            w = diff_w_qkv[j].astype(BF16)
            cos, sin = _axial_tables(LANES, DIFF_HEAD_DIM // 4, 0, LANES)
            qkv_p = _project(x, w, n_tiles=PROMPT_TILES, mods=mods, layer=i, name="diff_qkv_context")
            qk_s = _project(x, w[:, :2 * D_MODEL], tile_off=PROMPT_TILES, n_tiles=SAMPLE_TILES, mods=mods,
                            layer=i, rope=(DIFF_HEAD_DIM // 4, cos, sin), name="diff_qk_latent")
            v_s = _project(x, w[:, 2 * D_MODEL:], tile_off=PROMPT_TILES, n_tiles=SAMPLE_TILES, mods=mods,
                           layer=i, name="diff_v_latent")
            st_diff_k.append(qkv_p[:, D_MODEL:2 * D_MODEL])
            st_diff_v.append(qkv_p[:, 2 * D_MODEL:])
            common = dict(mode="diff", scale=DIFF_HEAD_DIM ** -0.5, lam=diff_lambda[j],
                          subln=diff_subln[j][None, :], lam_init=lam_init)
            o_p = _attention(qkv_p, qkv_p, qkv_p, n_batch=BATCH, seq=SEQ, tq=SEQ, q_col=0, k_col=nb,
                             v_col=2 * nb, name="diff_context_attention", **common)
            o_s = _attention(qk_s, qk_s, v_s, n_batch=DEC_BATCH, seq=DEC_SEQ, tq=TM, q_col=0, k_col=nb, v_col=0,
                             ctx=(cache_diff_k, cache_diff_v, j, 0, 0), name="diff_latent_attention", **common)
            o = jnp.concatenate([o_p, o_s], axis=0)
            w_o = diff_w_o[j]
        else:
            w_in, wq, wk, wv = _mla_weights(mla_w_in[j], mla_w_uq[j], mla_w_ukv[j])
            cos, sin = _axial_tables(LANES, MLA_ROPE // 4, MLA_NOPE, MLA_NOPE + MLA_ROPE)
            rope = (MLA_ROPE // 4, cos, sin)
            proj = _project(x, w_in, n_tiles=all_tiles, mods=mods, layer=i, name="mla_down")
            cq, ckvkr = _mla_norm(proj, mla_q_norm[j][None, :], mla_kv_norm[j][None, :])
            st_mla_ckv.append(ckvkr[:N_PROMPT, :MLA_KV_RANK])
            st_mla_kr.append(ckvkr[:N_PROMPT, MLA_KV_RANK:MLA_KV_RANK + MLA_ROPE])
            q_p = _project(cq, wq, n_tiles=PROMPT_TILES, name="mla_q_context")
            k_p = _project(ckvkr, wk, n_tiles=PROMPT_TILES, name="mla_k_context")
            q_s = _project(cq, wq, tile_off=PROMPT_TILES, n_tiles=SAMPLE_TILES, rope=rope, name="mla_q_latent")
            k_s = _project(ckvkr, wk, tile_off=PROMPT_TILES, n_tiles=SAMPLE_TILES, rope=rope, name="mla_k_latent")
            v_all = _project(ckvkr, wv, n_tiles=all_tiles, name="mla_v")
            ctx_in = jnp.concatenate(
                [cache_mla_ckv[:, j], cache_mla_kr[:, j],
                 jnp.zeros((DEC_BATCH, PAST_LEN, 2 * MLA_KV_RANK - MLA_KV_RANK - MLA_ROPE), F32)],
                axis=-1).reshape(DEC_BATCH * PAST_LEN, 2 * MLA_KV_RANK)
            n_ctx_tiles = DEC_BATCH * PAST_LEN // TM
            k_c = _project(ctx_in, wk, n_tiles=n_ctx_tiles, name="mla_k_cache")
            v_c = _project(ctx_in, wv, n_tiles=n_ctx_tiles, name="mla_v_cache")
            k_c = k_c.reshape(DEC_BATCH, 1, PAST_LEN, MLA_HEADS * LANES)
            v_c = v_c.reshape(DEC_BATCH, 1, PAST_LEN, D_MODEL)
            scale = (MLA_NOPE + MLA_ROPE) ** -0.5
            o_p = _attention(q_p, k_p, v_all, mode="mla", n_batch=BATCH, seq=SEQ, tq=SEQ, q_col=0, k_col=0,
                             v_col=0, scale=scale, name="mla_context_attention")
            o_s = _attention(q_s, k_s, v_all, mode="mla", n_batch=DEC_BATCH, seq=DEC_SEQ, tq=TM, q_col=0, k_col=0,
                             v_col=0, v_row_off=dec_off, ctx=(k_c, v_c, 0, 0, 0), scale=scale,
                             name="mla_latent_attention")
            o = jnp.concatenate([o_p, o_s], axis=0)
            w_o = mla_w_o[j]

        x = _out_proj_ln(o, w_o.astype(BF16), x, mods, i, ln_g[i, 0][None, :], ln_b[i, 0][None, :])
        x = _moe(x, mods, i, router_w, router_b, moe_w1, moe_b1, moe_w2, moe_b2,
                 ln_g[i, 1][None, :], ln_b[i, 1][None, :])

    y_prompt = x[:N_PROMPT].reshape(BATCH, SEQ, D_MODEL)
    y_sample = x[N_PROMPT:].reshape(DEC_BATCH, DEC_SEQ, D_MODEL)

    def stack(parts, *tail):
        return jnp.stack([p.reshape(BATCH, SEQ, *tail) for p in parts], axis=1)

    return (y_prompt, y_sample,
            stack(st_na_k, NA_HEADS, NA_HEAD_DIM), stack(st_na_v, NA_HEADS, NA_HEAD_DIM),
            stack(st_diff_k, DIFF_HEADS, 2 * DIFF_HEAD_DIM), stack(st_diff_v, DIFF_HEADS, 2 * DIFF_HEAD_DIM),
            stack(st_mla_ckv, MLA_KV_RANK), stack(st_mla_kr, MLA_ROPE))
```

```python
import functools
import math

import jax
import jax.numpy as jnp
from jax import lax
from jax.experimental import pallas as pl
from jax.experimental.pallas import tpu as pltpu

F32 = jnp.float32
BF16 = jnp.bfloat16
HIGHEST = lax.Precision.HIGHEST

D_MODEL = 1024
BATCH = 32
SEQ = 256
DEPTH = 4
DEC_BATCH = 4
DEC_SEQ = 2048
PAST_LEN = 512
GRID_W = 64
N_MIXERS = 3
NA_HEADS = 16
NA_HEAD_DIM = 64
WIN_R = 8
WIN_C = 16
DIFF_HEADS = 8
DIFF_HEAD_DIM = 64
MLA_HEADS = 16
MLA_Q_RANK = 256
MLA_KV_RANK = 128
MLA_NOPE = 64
MLA_ROPE = 32
MLA_V = 64
N_EXPERTS = 32
TOP_K = 4
D_EXPERT = 1024
SWIGLU_LIMIT = 7.0
SWIGLU_ALPHA = 1.702
ROPE_BASE = 10000.0
LN_EPS = 1e-5
RMS_EPS = 1e-6
NEG_INF = -1e30
DEEPNORM_ALPHA = (2 * DEPTH) ** 0.25

LANES = 128
TM = 256
N_PROMPT = BATCH * SEQ
N_SAMPLE = DEC_BATCH * DEC_SEQ
N_TOK = N_PROMPT + N_SAMPLE
PROMPT_TILES = N_PROMPT // TM
SAMPLE_TILES = N_SAMPLE // TM
TILES_PER_DEC = DEC_SEQ // TM
N_MOD_ROWS = 8
NA_QROWS = 4
NA_KROWS = NA_QROWS + WIN_R - 1
NA_BIAS_PAD = NA_QROWS - 1
NA_BIAS_ROWS = WIN_R - 1 + NA_BIAS_PAD + NA_KROWS
N_CHUNKS = N_TOK // TM
MOE_PIECE = 8
MAX_CHUNK_PIECES = (TM * TOP_K + N_EXPERTS * (MOE_PIECE - 1)) // MOE_PIECE
MOE_CHUNK_PIECES = 160
MOE_CHUNK_ROWS = MOE_CHUNK_PIECES * MOE_PIECE
EXP_TM = 512
EXP_PPT = EXP_TM // MOE_PIECE
MAX_TILES = N_CHUNKS * MAX_CHUNK_PIECES // EXP_PPT + N_EXPERTS
VMEM_LIMIT = 56 * 1024 * 1024


def _cparams(n_axes, vmem=VMEM_LIMIT):
    return pltpu.CompilerParams(dimension_semantics=("arbitrary",) * n_axes, vmem_limit_bytes=vmem)


def _mod_row(t):
    return jnp.where(t < PROMPT_TILES, 0, 1 + (t - PROMPT_TILES) // TILES_PER_DEC)


def _mod_spec(layer, chunk, tile_off, tile_axis):
    def index(*g):
        row = _mod_row(g[tile_axis] + tile_off)
        return ((layer * N_MOD_ROWS + row) * 6 + chunk, 0, 0)
    return pl.BlockSpec((None, 1, D_MODEL), index)


def _sigmoid(x):
    return 1.0 / (1.0 + jnp.exp(-x))


def _layernorm(z, g, b):
    mu = jnp.mean(z, axis=-1, keepdims=True)
    zc = z - mu
    var = jnp.mean(zc * zc, axis=-1, keepdims=True)
    return zc * lax.rsqrt(var + LN_EPS) * g + b


def _mod_kernel(cond_ref, w_ref, b_ref, o_ref):
    cnd = cond_ref[...]
    s = cnd * _sigmoid(cnd)
    o_ref[...] = jnp.dot(s, w_ref[...], precision=HIGHEST, preferred_element_type=F32) + b_ref[...]


def _modulation(cond, ada_w, ada_b):
    out = pl.pallas_call(
        _mod_kernel,
        grid=(DEPTH, 6),
        in_specs=[
            pl.BlockSpec((N_MOD_ROWS, D_MODEL), lambda l, j: (0, 0)),
            pl.BlockSpec((None, D_MODEL, D_MODEL), lambda l, j: (l, 0, j)),
            pl.BlockSpec((None, 1, D_MODEL), lambda l, j: (l, 0, j)),
        ],
        out_specs=pl.BlockSpec((None, N_MOD_ROWS, D_MODEL), lambda l, j: (l, 0, j)),
        out_shape=jax.ShapeDtypeStruct((DEPTH, N_MOD_ROWS, 6 * D_MODEL), F32),
        compiler_params=_cparams(2),
        name="adaln_modulation",
    )(cond, ada_w, ada_b.reshape(DEPTH, 1, 6 * D_MODEL))
    return out.reshape(DEPTH * N_MOD_ROWS * 6, 1, D_MODEL)


def _rope_swap(y, g):
    lane = lax.broadcasted_iota(jnp.int32, y.shape, 1)
    first = (lane % (2 * g)) < g
    return jnp.where(first, pltpu.roll(y, LANES - g, 1), pltpu.roll(y, g, 1))


def _proj_kernel(*refs, use_mod, rope_g):
    it = iter(refs)
    x_ref = next(it)
    x = x_ref[...]
    if use_mod:
        shift_ref, scale_ref = next(it), next(it)
        x = x * (1.0 + scale_ref[...]) + shift_ref[...]
    w_ref = next(it)
    y = jnp.dot(x.astype(BF16), w_ref[...], preferred_element_type=F32)
    if rope_g:
        cos_ref, sin_ref = next(it), next(it)
        o_ref = next(it)
        cos, sin = cos_ref[...], sin_ref[...]
        for j in range(y.shape[1] // LANES):
            yb = y[:, j * LANES:(j + 1) * LANES]
            o_ref[:, j * LANES:(j + 1) * LANES] = yb * cos + _rope_swap(yb, rope_g) * sin
    else:
        o_ref = next(it)
        o_ref[...] = y


def _project(x, w_bf16, *, tile_off=0, n_tiles, mods=None, layer=0, chunks=(0, 1), rope=None, name):
    k_dim, n = w_bf16.shape
    tn = min(n, 1024)
    grid = (n // tn, n_tiles)
    in_specs = [pl.BlockSpec((TM, k_dim), lambda j, i: (i + tile_off, 0))]
    args = [x]
    if mods is not None:
        in_specs += [_mod_spec(layer, chunks[0], tile_off, 1), _mod_spec(layer, chunks[1], tile_off, 1)]
        args += [mods, mods]
    in_specs.append(pl.BlockSpec((k_dim, tn), lambda j, i: (0, j)))
    args.append(w_bf16)
    rope_g = 0
    if rope is not None:
        rope_g, cos, sin = rope
        tab = pl.BlockSpec((TM, LANES), lambda j, i: (i % TILES_PER_DEC, 0))
        in_specs += [tab, tab]
        args += [cos, sin]
    return pl.pallas_call(
        functools.partial(_proj_kernel, use_mod=mods is not None, rope_g=rope_g),
        grid=grid,
        in_specs=in_specs,
        out_specs=pl.BlockSpec((TM, tn), lambda j, i: (i, j)),
        out_shape=jax.ShapeDtypeStruct((n_tiles * TM, n), F32),
        compiler_params=_cparams(2),
        name=name,
    )(*args)


def _axial_tables(group_lanes, nf, lane_lo, lane_hi):
    del group_lanes
    t = jnp.arange(DEC_SEQ)
    lane = jnp.arange(LANES)
    rel = lane - lane_lo
    active = (lane >= lane_lo) & (lane < lane_hi)
    grp = rel // (2 * nf)
    pos = jnp.where((grp % 2 == 0)[None, :], (t // GRID_W)[:, None], (t % GRID_W)[:, None])
    inv = ROPE_BASE ** (-jnp.arange(nf, dtype=F32) / nf)
    ang = pos.astype(F32) * inv[rel % nf][None, :]
    cos = jnp.cos(ang)
    sin = jnp.sin(ang)
    sign = jnp.where((rel % (2 * nf)) < nf, -1.0, 1.0)[None, :]
    cos = jnp.where(active[None, :], cos, 1.0).astype(F32)
    sin = jnp.where(active[None, :], sin * sign, 0.0).astype(F32)
    return cos, sin


def _nt_dot(a, b):
    return lax.dot_general(a, b, (((1,), (1,)), ((), ())), preferred_element_type=F32)


def _joint_softmax_pv(scores, values):
    m = scores[0].max(axis=-1, keepdims=True)
    for s in scores[1:]:
        m = jnp.maximum(m, s.max(axis=-1, keepdims=True))
    acc, den = None, None
    for s, v in zip(scores, values):
        e = jnp.exp(s - m)
        d = e.sum(axis=-1, keepdims=True)
        o = jnp.dot(e.astype(BF16), v, preferred_element_type=F32)
        acc = o if acc is None else acc + o
        den = d if den is None else den + d
    return acc, den


def _joint_softmax(scores):
    m = scores[0].max(axis=-1, keepdims=True)
    for s in scores[1:]:
        m = jnp.maximum(m, s.max(axis=-1, keepdims=True))
    es = [jnp.exp(s - m) for s in scores]
    den = es[0].sum(axis=-1, keepdims=True)
    for e in es[1:]:
        den = den + e.sum(axis=-1, keepdims=True)
    return [e / den for e in es]


def _attn_kernel(*refs, mode, has_ctx, scale, lam_init, hb):
    it = iter(refs)
    q_ref, k_ref, v_ref = next(it), next(it), next(it)
    kc_ref = vc_ref = None
    if has_ctx:
        kc_ref, vc_ref = next(it), next(it)
    if mode == "diff":
        lam_ref, subln_ref = next(it), next(it)
    o_ref = next(it)

    wq = 2 * LANES if mode == "mla" else LANES
    tq = q_ref.shape[0]
    lane_o = lax.broadcasted_iota(jnp.int32, (tq, LANES), 1)
    if mode == "diff":
        lp = lam_ref[...]
        lam = (jnp.exp(jnp.sum(lp[0:1] * lp[1:2], axis=-1, keepdims=True))
               - jnp.exp(jnp.sum(lp[2:3] * lp[3:4], axis=-1, keepdims=True)) + lam_init)

    for hblk in range(hb):
        qk_cols = slice(hblk * wq, (hblk + 1) * wq)
        v_cols = slice(hblk * LANES, (hblk + 1) * LANES)
        q = q_ref[:, qk_cols].astype(BF16)
        ks = [k_ref[:, qk_cols].astype(BF16)]
        vs = [v_ref[:, v_cols].astype(BF16)]
        if has_ctx:
            ks.append(kc_ref[:, qk_cols].astype(BF16))
            vs.append(vc_ref[:, v_cols].astype(BF16))

        if mode == "mla":
            halves = [(q[:, :LANES], [k[:, :LANES] for k in ks]), (q[:, LANES:], [k[:, LANES:] for k in ks])]
        else:
            lane_q = lax.broadcasted_iota(jnp.int32, q.shape, 1)
            zero = jnp.zeros_like(q)
            halves = [(jnp.where(lane_q < 64, q, zero), ks), (jnp.where(lane_q >= 64, q, zero), ks)]

        if mode == "diff":
            p1 = _joint_softmax([_nt_dot(halves[0][0], k) * scale for k in halves[0][1]])
            p2 = _joint_softmax([_nt_dot(halves[1][0], k) * scale for k in halves[1][1]])
            o = None
            for a, b, v in zip(p1, p2, vs):
                part = jnp.dot((a - lam * b).astype(BF16), v, preferred_element_type=F32)
                o = part if o is None else o + part
            o = o * lax.rsqrt(jnp.mean(o * o, axis=-1, keepdims=True) + RMS_EPS) * subln_ref[...]
            o_ref[:, v_cols] = o * (1.0 - lam_init)
        else:
            outs = []
            for qh, kh in halves:
                acc, den = _joint_softmax_pv([_nt_dot(qh, k) * scale for k in kh], vs)
                outs.append(acc / den)
            o_ref[:, v_cols] = jnp.where(lane_o < 64, outs[0], outs[1])


def _attention(q_arr, k_arr, v_arr, *, mode, n_batch, seq, tq, q_col, k_col, v_col, row_off=0, v_row_off=None,
               ctx=None, scale, lam=None, subln=None, lam_init=0.0, hb=1, name):
    assert mode != "mla" or (q_col == 0 and k_col == 0)
    n_blocks = D_MODEL // LANES
    assert n_blocks % hb == 0 and q_col % hb == 0 and k_col % hb == 0 and v_col % hb == 0
    wq = 2 * LANES if mode == "mla" else LANES
    wqb, wvb = wq * hb, LANES * hb
    nqb = seq // tq
    qoff = row_off * nqb
    if v_row_off is None:
        v_row_off = row_off
    in_specs = [
        pl.BlockSpec((tq, wqb), lambda b, h, i: (qoff + b * nqb + i, q_col // hb + h)),
        pl.BlockSpec((seq, wqb), lambda b, h, i: (row_off + b, k_col // hb + h)),
        pl.BlockSpec((seq, wvb), lambda b, h, i: (v_row_off + b, v_col // hb + h)),
    ]
    args = [q_arr, k_arr, v_arr]
    if ctx is not None:
        kc, vc, j = ctx
        in_specs += [
            pl.BlockSpec((None, None, PAST_LEN, wqb), lambda b, h, i: (b, j, 0, h)),
            pl.BlockSpec((None, None, PAST_LEN, wvb), lambda b, h, i: (b, j, 0, h)),
        ]
        args += [kc, vc]
    if mode == "diff":
        in_specs += [pl.BlockSpec((4, DIFF_HEAD_DIM), lambda b, h, i: (0, 0)),
                     pl.BlockSpec((1, LANES), lambda b, h, i: (0, 0))]
        args += [lam, subln]
    return pl.pallas_call(
        functools.partial(_attn_kernel, mode=mode, has_ctx=ctx is not None, scale=scale, lam_init=lam_init,
                          hb=hb),
        grid=(n_batch, n_blocks // hb, nqb),
        in_specs=in_specs,
        out_specs=pl.BlockSpec((tq, wvb), lambda b, h, i: (b * nqb + i, h)),
        out_shape=jax.ShapeDtypeStruct((n_batch * seq, D_MODEL), F32),
        compiler_params=_cparams(3),
        name=name,
    )(*args)


def _na_bias_tables(rpb):
    cols = jnp.arange(GRID_W)
    c0 = jnp.clip(cols - WIN_C // 2, 0, GRID_W - WIN_C)
    col_ok = (cols[None, :] >= c0[:, None]) & (cols[None, :] < c0[:, None] + WIN_C)
    col_off = jnp.clip(cols[None, :] - cols[:, None] + (WIN_C - 1), 0, 2 * WIN_C - 2)
    hot_c = (col_off[:, :, None] == jnp.arange(2 * WIN_C - 1)).astype(F32)
    core = jnp.einsum("hab,ckb->hcak", rpb, hot_c, precision=HIGHEST)
    core = jnp.where(col_ok[None, :, None, :], core, NEG_INF)
    pad_hi = NA_BIAS_ROWS - NA_BIAS_PAD - (2 * WIN_R - 1)
    tab = jnp.pad(core, ((0, 0), (0, 0), (NA_BIAS_PAD, pad_hi), (0, 0)), constant_values=NEG_INF)
    tab = tab.reshape(NA_HEADS, GRID_W, NA_BIAS_ROWS * GRID_W)
    shifted = jnp.pad(tab[:, :, GRID_W:], ((0, 0), (0, 0), (0, GRID_W)), constant_values=NEG_INF)
    return jnp.stack([tab, shifted]).astype(F32)


def _na_block_bias(bias_ref, half, kind):
    nk = NA_KROWS * GRID_W
    lane = lax.broadcasted_iota(jnp.int32, (GRID_W, nk), 1)
    slabs = []
    for i in range(NA_QROWS):
        if kind == 0:
            ext0, lo, hi = WIN_R - 1 + NA_BIAS_PAD - i, 0, WIN_R
        elif kind == 1:
            ext0, lo, hi = WIN_R - 1 - WIN_R // 2 + NA_BIAS_PAD - i, i, i + WIN_R
        else:
            ext0, lo, hi = NA_BIAS_PAD - i, NA_KROWS - WIN_R, NA_KROWS
        start = (ext0 - ext0 % 2) * GRID_W
        slab = bias_ref[ext0 % 2, half, :, start:start + nk]
        slabs.append(jnp.where((lane >= lo * GRID_W) & (lane < hi * GRID_W), slab, NEG_INF))
    return jnp.concatenate(slabs, axis=0)


def _na_latent_kernel(q_ref, k_ref, v_ref, kc_ref, vc_ref, bias_ref, o_ref, *, scale):
    rows_n = DEC_SEQ // GRID_W
    kc = kc_ref[...].astype(BF16)
    vc = vc_ref[...].astype(BF16)
    tq = NA_QROWS * GRID_W
    nk = NA_KROWS * GRID_W
    lane = lax.broadcasted_iota(jnp.int32, (tq, LANES), 1)
    biases = {}
    for blk in range(rows_n // NA_QROWS):
        q_row0 = blk * NA_QROWS
        if blk == 0:
            kind, k_row0 = 0, 0
        elif blk == rows_n // NA_QROWS - 1:
            kind, k_row0 = 2, rows_n - NA_KROWS
        else:
            kind, k_row0 = 1, q_row0 - WIN_R // 2
        q = q_ref[q_row0 * GRID_W:q_row0 * GRID_W + tq, :].astype(BF16)
        k = k_ref[k_row0 * GRID_W:k_row0 * GRID_W + nk, :].astype(BF16)
        v = v_ref[k_row0 * GRID_W:k_row0 * GRID_W + nk, :].astype(BF16)
        zero = jnp.zeros_like(q)
        outs = []
        for half in range(2):
            qh = jnp.where((lane < 64) if half == 0 else (lane >= 64), q, zero)
            if (kind, half) not in biases:
                biases[kind, half] = _na_block_bias(bias_ref, half, kind)
            s_loc = _nt_dot(qh, k) * scale + biases[kind, half]
            s_ctx = _nt_dot(qh, kc) * scale
            acc, den = _joint_softmax_pv([s_loc, s_ctx], [v, vc])
            outs.append(acc / den)
        o_ref[q_row0 * GRID_W:q_row0 * GRID_W + tq, :] = jnp.where(lane < 64, outs[0], outs[1])


def _na_latent_attention(qkv, cache_k, cache_v, j, bias):
    off = N_PROMPT // DEC_SEQ
    nb = D_MODEL // LANES
    return pl.pallas_call(
        functools.partial(_na_latent_kernel, scale=NA_HEAD_DIM ** -0.5),
        grid=(nb, DEC_BATCH),
        in_specs=[
            pl.BlockSpec((DEC_SEQ, LANES), lambda h, b: (off + b, h)),
            pl.BlockSpec((DEC_SEQ, LANES), lambda h, b: (off + b, nb + h)),
            pl.BlockSpec((DEC_SEQ, LANES), lambda h, b: (off + b, 2 * nb + h)),
            pl.BlockSpec((None, None, PAST_LEN, LANES), lambda h, b: (b, j, 0, h)),
            pl.BlockSpec((None, None, PAST_LEN, LANES), lambda h, b: (b, j, 0, h)),
            pl.BlockSpec((2, 2, GRID_W, NA_BIAS_ROWS * GRID_W), lambda h, b: (0, h, 0, 0)),
        ],
        out_specs=pl.BlockSpec((DEC_SEQ, LANES), lambda h, b: (b, h)),
        out_shape=jax.ShapeDtypeStruct((N_SAMPLE, D_MODEL), F32),
        compiler_params=_cparams(2),
        name="na_latent_attention",
    )(qkv, qkv, qkv, cache_k, cache_v, bias)


def _out_ln_kernel(op_ref, os_ref, w_ref, x_ref, gate_ref, g_ref, b_ref, xo_ref):
    def finish(o_ref):
        y = jnp.dot(o_ref[...].astype(BF16), w_ref[...], preferred_element_type=F32)
        z = DEEPNORM_ALPHA * x_ref[...] + gate_ref[...] * y
        xo_ref[...] = _layernorm(z, g_ref[...], b_ref[...])

    @pl.when(pl.program_id(0) < PROMPT_TILES)
    def _():
        finish(op_ref)

    @pl.when(pl.program_id(0) >= PROMPT_TILES)
    def _():
        finish(os_ref)


def _out_proj_ln(o_p, o_s, w_bf16, x, mods, layer, ln_g, ln_b):
    k_dim = w_bf16.shape[0]
    row = pl.BlockSpec((1, D_MODEL), lambda i: (0, 0))
    return pl.pallas_call(
        _out_ln_kernel,
        grid=(N_TOK // TM,),
        in_specs=[
            pl.BlockSpec((TM, k_dim), lambda i: (jnp.minimum(i, PROMPT_TILES - 1), 0)),
            pl.BlockSpec((TM, k_dim), lambda i: (jnp.maximum(i - PROMPT_TILES, 0), 0)),
            pl.BlockSpec((k_dim, D_MODEL), lambda i: (0, 0)),
            pl.BlockSpec((TM, D_MODEL), lambda i: (i, 0)),
            _mod_spec(layer, 2, 0, 0),
            row, row,
        ],
        out_specs=pl.BlockSpec((TM, D_MODEL), lambda i: (i, 0)),
        out_shape=jax.ShapeDtypeStruct((N_TOK, D_MODEL), F32),
        compiler_params=_cparams(1),
        name="out_proj_layernorm",
    )(o_p, o_s, w_bf16, x, mods, ln_g, ln_b)


def _route_kernel(x_ref, shift_ref, scale_ref, rw_ref, rb_ref, xs_ref, meta_ref, cnt_ref):
    @pl.when(pl.program_id(0) < N_CHUNKS)
    def _():
        _route_chunk(x_ref, shift_ref, scale_ref, rw_ref, rb_ref, xs_ref, meta_ref, cnt_ref)

    @pl.when(pl.program_id(0) == N_CHUNKS)
    def _():
        xs_ref[...] = jnp.zeros_like(xs_ref)


def _route_chunk(x_ref, shift_ref, scale_ref, rw_ref, rb_ref, xs_ref, meta_ref, cnt_ref):
    h = x_ref[...] * (1.0 + scale_ref[...]) + shift_ref[...]
    logits = jnp.dot(h, rw_ref[...], precision=HIGHEST, preferred_element_type=F32) + rb_ref[...]
    lane = lax.broadcasted_iota(jnp.int32, logits.shape, 1)
    work = logits
    top_v, hots = [], []
    for _ in range(TOP_K):
        m = work.max(axis=-1, keepdims=True)
        first = jnp.where(work == m, lane, N_EXPERTS).min(axis=-1, keepdims=True)
        hot = lane == first
        top_v.append(m)
        hots.append(jnp.where(hot, 1.0, 0.0))
        work = jnp.where(hot, -jnp.inf, work)
    es = [jnp.exp(v - top_v[0]) for v in top_v]
    den = es[0] + es[1] + es[2] + es[3]
    gates = [e / den for e in es]

    cnts = [hf.sum(axis=0, keepdims=True) for hf in hots]
    n_e = cnts[0] + cnts[1] + cnts[2] + cnts[3]
    pieces = jnp.floor((n_e + (MOE_PIECE - 1)) * (1.0 / MOE_PIECE))
    er = lax.broadcasted_iota(jnp.int32, (N_EXPERTS, N_EXPERTS), 0)
    ec = lax.broadcasted_iota(jnp.int32, (N_EXPERTS, N_EXPERTS), 1)
    upper = jnp.where(er < ec, 1.0, 0.0).astype(BF16)
    poff = jnp.dot(jnp.broadcast_to(pieces, (8, N_EXPERTS)).astype(BF16), upper,
                   preferred_element_type=F32)[0:1]

    tr = lax.broadcasted_iota(jnp.int32, (TM, TM), 0)
    tc = lax.broadcasted_iota(jnp.int32, (TM, TM), 1)
    lower = jnp.where(tc < tr, 1.0, 0.0).astype(BF16)
    run = poff * MOE_PIECE
    slots = []
    for k in range(TOP_K):
        prefix = jnp.dot(lower, hots[k].astype(BF16), preferred_element_type=F32)
        slots.append(jnp.sum(hots[k] * (run + prefix), axis=-1, keepdims=True))
        run = run + cnts[k]

    lane_w = lax.broadcasted_iota(jnp.int32, (TM, LANES), 1)
    meta = jnp.full((TM, LANES), -1.0, F32)
    for k in range(TOP_K):
        meta = jnp.where(lane_w == k, slots[k], meta)
        meta = jnp.where(lane_w == TOP_K + k, gates[k], meta)
    meta_ref[...] = meta[:, :2 * TOP_K]
    cnt_ref[...] = pieces

    slot_rows = meta.T
    srow = lax.broadcasted_iota(jnp.int32, (MOE_CHUNK_ROWS, TM), 0).astype(F32)
    p = jnp.zeros((MOE_CHUNK_ROWS, TM), F32)
    for k in range(TOP_K):
        p = p + jnp.where(srow == slot_rows[k:k + 1, :], 1.0, 0.0)
    xs_ref[...] = jnp.dot(p.astype(BF16), h.astype(BF16), preferred_element_type=F32)


def _route(x, mods, layer, rw, rb):
    assert 2 * EXP_TM <= MOE_CHUNK_ROWS
    last = N_CHUNKS - 1
    return pl.pallas_call(
        _route_kernel,
        grid=(N_CHUNKS + 1,),
        in_specs=[
            pl.BlockSpec((TM, D_MODEL), lambda i: (jnp.minimum(i, last), 0)),
            _mod_spec(layer, 3, 0, 0),
            _mod_spec(layer, 4, 0, 0),
            pl.BlockSpec((None, D_MODEL, N_EXPERTS), lambda i: (layer, 0, 0)),
            pl.BlockSpec((None, 1, N_EXPERTS), lambda i: (layer, 0, 0)),
        ],
        out_specs=[
            pl.BlockSpec((MOE_CHUNK_ROWS, D_MODEL), lambda i: (i, 0)),
            pl.BlockSpec((TM, 2 * TOP_K), lambda i: (jnp.minimum(i, last), 0)),
            pl.BlockSpec((None, 1, N_EXPERTS), lambda i: (jnp.minimum(i, last), 0, 0)),
        ],
        out_shape=[
            jax.ShapeDtypeStruct(((N_CHUNKS + 1) * MOE_CHUNK_ROWS, D_MODEL), F32),
            jax.ShapeDtypeStruct((N_TOK, 2 * TOP_K), F32),
            jax.ShapeDtypeStruct((N_CHUNKS, 1, N_EXPERTS), F32),
        ],
        compiler_params=_cparams(1),
        name="moe_route",
    )(x, mods, mods, rw, rb.reshape(DEPTH, 1, N_EXPERTS))


def _moe_schedule(pieces):
    e_ids = jnp.arange(N_EXPERTS, dtype=jnp.int32)
    npe = pieces.sum(axis=0)
    tiles_e = (npe + EXP_PPT - 1) // EXP_PPT
    tile_end = jnp.cumsum(tiles_e)
    tile_start = tile_end - tiles_e
    g = jnp.arange(MAX_TILES, dtype=jnp.int32)
    last_e = jnp.max(jnp.where(tiles_e > 0, e_ids, 0))
    te = jnp.minimum(jnp.sum(tile_end[None, :] <= g[:, None], axis=1), last_e).astype(jnp.int32)
    seg_len = pieces.T
    seg_base = tile_start[:, None] * EXP_PPT + jnp.cumsum(seg_len, axis=1) - seg_len
    poff = jnp.cumsum(pieces, axis=1) - pieces
    seg_pos = (jnp.arange(N_CHUNKS, dtype=jnp.int32)[:, None] * MOE_CHUNK_PIECES + poff).T
    hot = (te[:, None] == e_ids[None, :])[:, :, None]
    base, length, first = (jnp.sum(jnp.where(hot, tab[None], 0), axis=1)
                           for tab in (seg_base, seg_len, seg_pos))
    lane = jnp.arange(EXP_PPT, dtype=jnp.int32)[None, :]
    pos = (g[:, None] * EXP_PPT + lane)[:, :, None]
    hit = (base[:, None, :] <= pos) & (pos < (base + length)[:, None, :])
    src = jnp.sum(jnp.where(hit, first[:, None, :] + pos - base[:, None, :], 0), axis=-1)
    valid = jnp.any(hit, axis=-1)
    scratch = N_CHUNKS * MOE_CHUNK_PIECES + (g[:, None] % 2) * EXP_PPT + lane
    dst = jnp.where(valid, src, scratch).reshape(-1).astype(jnp.int32)
    src = jnp.where(valid, src, src[:, :1]).reshape(-1).astype(jnp.int32)
    return te, tile_end[-1:].astype(jnp.int32), src, dst


def _expert_kernel(te_ref, nt_ref, src_ref, dst_ref, xs_hbm, w1_ref, b1_ref, w2_ref, b2_ref, out_hbm,
                   xbuf, ybuf, w1s, w2s, sem_in, sem_out):
    g = pl.program_id(0)
    nt = nt_ref[0]
    slot = g % 2

    def in_copy(tile, i, s):
        row = pl.multiple_of(src_ref[tile * EXP_PPT + i] * MOE_PIECE, MOE_PIECE)
        return pltpu.make_async_copy(xs_hbm.at[pl.ds(row, MOE_PIECE), :],
                                     xbuf.at[s, pl.ds(i * MOE_PIECE, MOE_PIECE), :], sem_in.at[s])

    def out_copy(tile, i, s):
        row = pl.multiple_of(dst_ref[tile * EXP_PPT + i] * MOE_PIECE, MOE_PIECE)
        return pltpu.make_async_copy(ybuf.at[s, pl.ds(i * MOE_PIECE, MOE_PIECE), :],
                                     out_hbm.at[pl.ds(row, MOE_PIECE), :], sem_out.at[s])

    def gather(tile, s, wait):
        for i in range(EXP_PPT):
            cp = in_copy(tile, i, s)
            cp.wait() if wait else cp.start()

    def scatter(tile, s, wait):
        for i in range(EXP_PPT):
            cp = out_copy(tile, i, s)
            cp.wait() if wait else cp.start()

    @pl.when(g < nt)
    def _():
        @pl.when(g == 0)
        def _():
            gather(0, 0, False)

        @pl.when(g + 1 < nt)
        def _():
            gather(g + 1, 1 - slot, False)

        gather(g, slot, True)

        @pl.when(g >= 2)
        def _():
            scatter(g - 2, slot, True)

        @pl.when((g == 0) | (te_ref[g] != te_ref[jnp.maximum(g - 1, 0)]))
        def _():
            w1s[...] = w1_ref[...].astype(BF16)
            w2s[...] = w2_ref[...].astype(BF16)

        gu = jnp.dot(xbuf[slot].astype(BF16), w1s[...], preferred_element_type=F32) + b1_ref[...]
        gl = jnp.minimum(gu[:, :D_EXPERT], SWIGLU_LIMIT)
        ul = jnp.clip(gu[:, D_EXPERT:], -SWIGLU_LIMIT, SWIGLU_LIMIT)
        act = (ul + 1.0) * (gl * _sigmoid(SWIGLU_ALPHA * gl))
        ybuf[slot] = jnp.dot(act.astype(BF16), w2s[...], preferred_element_type=F32) + b2_ref[...]
        scatter(g, slot, False)

        @pl.when(g == nt - 1)
        def _():
            @pl.when(g >= 1)
            def _():
                scatter(g - 1, 1 - slot, True)
            scatter(g, slot, True)


def _experts(xs, te, nt, src, dst, layer, w1, b1, w2, b2):
    def expert_block(g, te, nt, src, dst):
        return (layer, te[g], 0, 0)

    grid_spec = pltpu.PrefetchScalarGridSpec(
        num_scalar_prefetch=4,
        grid=(MAX_TILES,),
        in_specs=[
            pl.BlockSpec(memory_space=pl.ANY),
            pl.BlockSpec((None, None, D_MODEL, 2 * D_EXPERT), expert_block),
            pl.BlockSpec((None, None, 1, 2 * D_EXPERT), expert_block),
            pl.BlockSpec((None, None, D_EXPERT, D_MODEL), expert_block),
            pl.BlockSpec((None, None, 1, D_MODEL), expert_block),
        ],
        out_specs=pl.BlockSpec(memory_space=pl.ANY),
        scratch_shapes=[
            pltpu.VMEM((2, EXP_TM, D_MODEL), F32),
            pltpu.VMEM((2, EXP_TM, D_MODEL), F32),
            pltpu.VMEM((D_MODEL, 2 * D_EXPERT), BF16),
            pltpu.VMEM((D_EXPERT, D_MODEL), BF16),
            pltpu.SemaphoreType.DMA((2,)),
            pltpu.SemaphoreType.DMA((2,)),
        ],
    )
    return pl.pallas_call(
        _expert_kernel,
        grid_spec=grid_spec,
        out_shape=jax.ShapeDtypeStruct(xs.shape, F32),
        input_output_aliases={4: 0},
        compiler_params=_cparams(1),
        name="moe_experts",
    )(te, nt, src, dst, xs, w1, b1.reshape(DEPTH, N_EXPERTS, 1, 2 * D_EXPERT),
      w2, b2.reshape(DEPTH, N_EXPERTS, 1, D_MODEL))


def _combine_kernel(ys_ref, meta_ref, x_ref, gate_ref, g_ref, b_ref, xo_ref):
    meta = meta_ref[...]
    col = lax.broadcasted_iota(jnp.int32, (TM, MOE_CHUNK_ROWS), 1).astype(F32)
    gmat = jnp.zeros((TM, MOE_CHUNK_ROWS), F32)
    for k in range(TOP_K):
        gmat = gmat + jnp.where(col == meta[:, k:k + 1], meta[:, TOP_K + k:TOP_K + k + 1], 0.0)
    y = jnp.dot(gmat.astype(BF16), ys_ref[...].astype(BF16), preferred_element_type=F32)
    z = DEEPNORM_ALPHA * x_ref[...] + gate_ref[...] * y
    xo_ref[...] = _layernorm(z, g_ref[...], b_ref[...])


def _combine_ln(ys, meta, x, mods, layer, ln_g, ln_b):
    row = pl.BlockSpec((1, D_MODEL), lambda i: (0, 0))
    return pl.pallas_call(
        _combine_kernel,
        grid=(N_CHUNKS,),
        in_specs=[
            pl.BlockSpec((MOE_CHUNK_ROWS, D_MODEL), lambda i: (i, 0)),
            pl.BlockSpec((TM, 2 * TOP_K), lambda i: (i, 0)),
            pl.BlockSpec((TM, D_MODEL), lambda i: (i, 0)),
            _mod_spec(layer, 5, 0, 0),
            row, row,
        ],
        out_specs=pl.BlockSpec((TM, D_MODEL), lambda i: (i, 0)),
        out_shape=jax.ShapeDtypeStruct((N_TOK, D_MODEL), F32),
        compiler_params=_cparams(1),
        name="moe_combine_layernorm",
    )(ys, meta, x, mods, ln_g, ln_b)


def _moe(x, mods, layer, rw, rb, w1, b1, w2, b2, ln_g, ln_b):
    xs, meta, pieces = _route(x, mods, layer, rw, rb)
    te, nt, src, dst = _moe_schedule(pieces.reshape(N_CHUNKS, N_EXPERTS).astype(jnp.int32))
    ys = _experts(xs, te, nt, src, dst, layer, w1, b1, w2, b2)
    return _combine_ln(ys, meta, x, mods, layer, ln_g, ln_b)


def _mla_norm_kernel(p_ref, qn_ref, kvn_ref, cq_ref, ckvkr_ref):
    p = p_ref[...]
    cq = p[:, :MLA_Q_RANK]
    cq_ref[...] = cq * lax.rsqrt(jnp.mean(cq * cq, axis=-1, keepdims=True) + RMS_EPS) * qn_ref[...]
    ckv = p[:, MLA_Q_RANK:MLA_Q_RANK + MLA_KV_RANK]
    ckvkr_ref[:, :MLA_KV_RANK] = (ckv * lax.rsqrt(jnp.mean(ckv * ckv, axis=-1, keepdims=True) + RMS_EPS)
                                  * kvn_ref[...])
    ckvkr_ref[:, MLA_KV_RANK:] = p[:, MLA_Q_RANK + MLA_KV_RANK:]


def _mla_norm(proj, q_norm, kv_norm):
    return pl.pallas_call(
        _mla_norm_kernel,
        grid=(N_TOK // TM,),
        in_specs=[
            pl.BlockSpec((TM, 512), lambda i: (i, 0)),
            pl.BlockSpec((1, MLA_Q_RANK), lambda i: (0, 0)),
            pl.BlockSpec((1, MLA_KV_RANK), lambda i: (0, 0)),
        ],
        out_specs=[pl.BlockSpec((TM, MLA_Q_RANK), lambda i: (i, 0)),
                   pl.BlockSpec((TM, 2 * MLA_KV_RANK), lambda i: (i, 0))],
        out_shape=[jax.ShapeDtypeStruct((N_TOK, MLA_Q_RANK), F32),
                   jax.ShapeDtypeStruct((N_TOK, 2 * MLA_KV_RANK), F32)],
        compiler_params=_cparams(1),
        name="mla_norm",
    )(proj, q_norm, kv_norm)


def _mla_weights(w_in, w_uq, w_ukv):
    pad = 512 - w_in.shape[1]
    w_in_p = jnp.pad(w_in, ((0, 0), (0, pad)))
    wq = w_uq.reshape(MLA_Q_RANK, MLA_HEADS, MLA_NOPE + MLA_ROPE)
    wq = jnp.pad(wq, ((0, 0), (0, 0), (0, LANES - MLA_NOPE - MLA_ROPE))).reshape(MLA_Q_RANK, MLA_HEADS * LANES)
    wkv = w_ukv.reshape(MLA_KV_RANK, MLA_HEADS, MLA_NOPE + MLA_V)
    wkn = jnp.pad(wkv[:, :, :MLA_NOPE], ((0, 0), (0, 0), (0, LANES - MLA_NOPE)))
    eye = jnp.pad(jnp.eye(MLA_ROPE, dtype=F32), ((0, 0), (MLA_NOPE, LANES - MLA_NOPE - MLA_ROPE)))
    wkr = jnp.broadcast_to(eye[:, None, :], (MLA_ROPE, MLA_HEADS, LANES))
    zeros = jnp.zeros((2 * MLA_KV_RANK - MLA_KV_RANK - MLA_ROPE, MLA_HEADS, LANES), F32)
    wk = jnp.concatenate([wkn, wkr, zeros], axis=0).reshape(2 * MLA_KV_RANK, MLA_HEADS * LANES)
    wv = jnp.pad(wkv[:, :, MLA_NOPE:].reshape(MLA_KV_RANK, MLA_HEADS * MLA_V), ((0, MLA_KV_RANK), (0, 0)))
    return w_in_p.astype(BF16), wq.astype(BF16), wk.astype(BF16), wv.astype(BF16)


def kernel(x_prompt, x_sample, cache_na_k, cache_na_v, cache_diff_k, cache_diff_v, cache_mla_ckv, cache_mla_kr,
           c, c_ctx, ada_w, ada_b, ln_g, ln_b, na_w_qkv, na_rpb, na_w_o, diff_w_qkv, diff_lambda, diff_subln,
           diff_w_o, mla_w_in, mla_q_norm, mla_w_uq, mla_kv_norm, mla_w_ukv, mla_w_o, router_w, router_b,
           moe_w1, moe_b1, moe_w2, moe_b2):
    x = jnp.concatenate([x_prompt.reshape(N_PROMPT, D_MODEL), x_sample.reshape(N_SAMPLE, D_MODEL)], axis=0)
    cond = jnp.concatenate([c_ctx[None, :], c, jnp.zeros((N_MOD_ROWS - 1 - DEC_BATCH, D_MODEL), F32)], axis=0)
    mods = _modulation(cond, ada_w, ada_b)

    n_na = na_w_qkv.shape[0]
    n_diff = diff_w_qkv.shape[0]
    n_mla = mla_w_in.shape[0]
    cache_na_k = cache_na_k.reshape(DEC_BATCH, n_na, PAST_LEN, D_MODEL)
    cache_na_v = cache_na_v.reshape(DEC_BATCH, n_na, PAST_LEN, D_MODEL)
    cache_diff_k = cache_diff_k.reshape(DEC_BATCH, n_diff, PAST_LEN, D_MODEL)
    cache_diff_v = cache_diff_v.reshape(DEC_BATCH, n_diff, PAST_LEN, D_MODEL)
    all_tiles = N_TOK // TM
    nb = D_MODEL // LANES
    dec_off = N_PROMPT // DEC_SEQ

    st_na_k, st_na_v, st_diff_k, st_diff_v, st_mla_ckv, st_mla_kr = [], [], [], [], [], []
    for i in range(DEPTH):
        kind, j = i % N_MIXERS, i // N_MIXERS
        if kind == 0:
            qkv = _project(x, na_w_qkv[j].astype(BF16), n_tiles=all_tiles, mods=mods, layer=i, name="na_qkv")
            st_na_k.append(qkv[:N_PROMPT, D_MODEL:2 * D_MODEL])
            st_na_v.append(qkv[:N_PROMPT, 2 * D_MODEL:])
            o_p = _attention(qkv, qkv, qkv, mode="pair", n_batch=BATCH, seq=SEQ, tq=SEQ, q_col=0, k_col=nb,
                             v_col=2 * nb, scale=NA_HEAD_DIM ** -0.5, hb=nb, name="na_context_attention")
            o_s = _na_latent_attention(qkv, cache_na_k, cache_na_v, j, _na_bias_tables(na_rpb[j]))
            w_o = na_w_o[j]
        elif kind == 1:
            lam_init = 0.8 - 0.6 * math.exp(-0.3 * i)
            w = diff_w_qkv[j].astype(BF16)
            cos, sin = _axial_tables(LANES, DIFF_HEAD_DIM // 4, 0, LANES)
            qkv_p = _project(x, w, n_tiles=PROMPT_TILES, mods=mods, layer=i, name="diff_qkv_context")
            qk_s = _project(x, w[:, :2 * D_MODEL], tile_off=PROMPT_TILES, n_tiles=SAMPLE_TILES, mods=mods,
                            layer=i, rope=(DIFF_HEAD_DIM // 4, cos, sin), name="diff_qk_latent")
            v_s = _project(x, w[:, 2 * D_MODEL:], tile_off=PROMPT_TILES, n_tiles=SAMPLE_TILES, mods=mods,
                           layer=i, name="diff_v_latent")
            st_diff_k.append(qkv_p[:, D_MODEL:2 * D_MODEL])
            st_diff_v.append(qkv_p[:, 2 * D_MODEL:])
            common = dict(mode="diff", scale=DIFF_HEAD_DIM ** -0.5, lam=diff_lambda[j],
                          subln=diff_subln[j][None, :], lam_init=lam_init)
            o_p = _attention(qkv_p, qkv_p, qkv_p, n_batch=BATCH, seq=SEQ, tq=SEQ, q_col=0, k_col=nb,
                             v_col=2 * nb, hb=nb, name="diff_context_attention", **common)
            o_s = _attention(qk_s, qk_s, v_s, n_batch=DEC_BATCH, seq=DEC_SEQ, tq=TM, q_col=0, k_col=nb, v_col=0,
                             ctx=(cache_diff_k, cache_diff_v, j), hb=2, name="diff_latent_attention", **common)
            w_o = diff_w_o[j]
        else:
            w_in, wq, wk, wv = _mla_weights(mla_w_in[j], mla_w_uq[j], mla_w_ukv[j])
            cos, sin = _axial_tables(LANES, MLA_ROPE // 4, MLA_NOPE, MLA_NOPE + MLA_ROPE)
            rope = (MLA_ROPE // 4, cos, sin)
            proj = _project(x, w_in, n_tiles=all_tiles, mods=mods, layer=i, name="mla_down")
            cq, ckvkr = _mla_norm(proj, mla_q_norm[j][None, :], mla_kv_norm[j][None, :])
            st_mla_ckv.append(ckvkr[:N_PROMPT, :MLA_KV_RANK])
            st_mla_kr.append(ckvkr[:N_PROMPT, MLA_KV_RANK:MLA_KV_RANK + MLA_ROPE])
            q_p = _project(cq, wq, n_tiles=PROMPT_TILES, name="mla_q_context")
            k_p = _project(ckvkr, wk, n_tiles=PROMPT_TILES, name="mla_k_context")
            q_s = _project(cq, wq, tile_off=PROMPT_TILES, n_tiles=SAMPLE_TILES, rope=rope, name="mla_q_latent")
            k_s = _project(ckvkr, wk, tile_off=PROMPT_TILES, n_tiles=SAMPLE_TILES, rope=rope, name="mla_k_latent")
            v_all = _project(ckvkr, wv, n_tiles=all_tiles, name="mla_v")
            ctx_in = jnp.concatenate(
                [cache_mla_ckv[:, j], cache_mla_kr[:, j],
                 jnp.zeros((DEC_BATCH, PAST_LEN, 2 * MLA_KV_RANK - MLA_KV_RANK - MLA_ROPE), F32)],
                axis=-1).reshape(DEC_BATCH * PAST_LEN, 2 * MLA_KV_RANK)
            n_ctx_tiles = DEC_BATCH * PAST_LEN // TM
            k_c = _project(ctx_in, wk, n_tiles=n_ctx_tiles, name="mla_k_cache")
            v_c = _project(ctx_in, wv, n_tiles=n_ctx_tiles, name="mla_v_cache")
            k_c = k_c.reshape(DEC_BATCH, 1, PAST_LEN, MLA_HEADS * LANES)
            v_c = v_c.reshape(DEC_BATCH, 1, PAST_LEN, D_MODEL)
            scale = (MLA_NOPE + MLA_ROPE) ** -0.5
            o_p = _attention(q_p, k_p, v_all, mode="mla", n_batch=BATCH, seq=SEQ, tq=SEQ, q_col=0, k_col=0,
                             v_col=0, scale=scale, hb=nb, name="mla_context_attention")
            o_s = _attention(q_s, k_s, v_all, mode="mla", n_batch=DEC_BATCH, seq=DEC_SEQ, tq=TM, q_col=0, k_col=0,
                             v_col=0, v_row_off=dec_off, ctx=(k_c, v_c, 0), scale=scale, hb=2,
                             name="mla_latent_attention")
            w_o = mla_w_o[j]

        x = _out_proj_ln(o_p, o_s, w_o.astype(BF16), x, mods, i, ln_g[i, 0][None, :], ln_b[i, 0][None, :])
        x = _moe(x, mods, i, router_w, router_b, moe_w1, moe_b1, moe_w2, moe_b2,
                 ln_g[i, 1][None, :], ln_b[i, 1][None, :])

    y_prompt = x[:N_PROMPT].reshape(BATCH, SEQ, D_MODEL)
    y_sample = x[N_PROMPT:].reshape(DEC_BATCH, DEC_SEQ, D_MODEL)

    def stack(parts, *tail):
        return jnp.stack([p.reshape(BATCH, SEQ, *tail) for p in parts], axis=1)

    return (y_prompt, y_sample,
            stack(st_na_k, NA_HEADS, NA_HEAD_DIM), stack(st_na_v, NA_HEADS, NA_HEAD_DIM),
            stack(st_diff_k, DIFF_HEADS, 2 * DIFF_HEAD_DIM), stack(st_diff_v, DIFF_HEADS, 2 * DIFF_HEAD_DIM),
            stack(st_mla_ckv, MLA_KV_RANK), stack(st_mla_kr, MLA_ROPE))
```

```python
import functools
import math

import jax
import jax.numpy as jnp
from jax import lax
from jax.experimental import pallas as pl
from jax.experimental.pallas import tpu as pltpu

F32 = jnp.float32
BF16 = jnp.bfloat16
HIGHEST = lax.Precision.HIGHEST

D_MODEL = 1024
BATCH = 32
SEQ = 256
DEPTH = 4
DEC_BATCH = 4
DEC_SEQ = 2048
PAST_LEN = 512
GRID_W = 64
N_MIXERS = 3
NA_HEADS = 16
NA_HEAD_DIM = 64
WIN_R = 8
WIN_C = 16
DIFF_HEADS = 8
DIFF_HEAD_DIM = 64
MLA_HEADS = 16
MLA_Q_RANK = 256
MLA_KV_RANK = 128
MLA_NOPE = 64
MLA_ROPE = 32
MLA_V = 64
N_EXPERTS = 32
TOP_K = 4
D_EXPERT = 1024
SWIGLU_LIMIT = 7.0
SWIGLU_ALPHA = 1.702
ROPE_BASE = 10000.0
LN_EPS = 1e-5
RMS_EPS = 1e-6
NEG_INF = -1e30
DEEPNORM_ALPHA = (2 * DEPTH) ** 0.25

LANES = 128
TM = 256
N_PROMPT = BATCH * SEQ
N_SAMPLE = DEC_BATCH * DEC_SEQ
N_TOK = N_PROMPT + N_SAMPLE
PROMPT_TILES = N_PROMPT // TM
SAMPLE_TILES = N_SAMPLE // TM
TILES_PER_DEC = DEC_SEQ // TM
N_MOD_ROWS = 8
NA_QROWS = 4
NA_KROWS = NA_QROWS + WIN_R - 1
NA_BIAS_PAD = NA_QROWS - 1
NA_BIAS_ROWS = WIN_R - 1 + NA_BIAS_PAD + NA_KROWS
N_CHUNKS = N_TOK // TM
MOE_PIECE = 8
MAX_CHUNK_PIECES = (TM * TOP_K + N_EXPERTS * (MOE_PIECE - 1)) // MOE_PIECE
MOE_CHUNK_PIECES = 160
MOE_CHUNK_ROWS = MOE_CHUNK_PIECES * MOE_PIECE
EXP_TM = 512
EXP_PPT = EXP_TM // MOE_PIECE
MAX_TILES = N_CHUNKS * MAX_CHUNK_PIECES // EXP_PPT + N_EXPERTS
VMEM_LIMIT = 56 * 1024 * 1024


def _cparams(n_axes, vmem=VMEM_LIMIT):
    return pltpu.CompilerParams(dimension_semantics=("arbitrary",) * n_axes, vmem_limit_bytes=vmem)


def _mod_row(t):
    return jnp.where(t < PROMPT_TILES, 0, 1 + (t - PROMPT_TILES) // TILES_PER_DEC)


def _mod_spec(layer, chunk, tile_off, tile_axis):
    def index(*g):
        row = _mod_row(g[tile_axis] + tile_off)
        return ((layer * N_MOD_ROWS + row) * 6 + chunk, 0, 0)
    return pl.BlockSpec((None, 1, D_MODEL), index)


def _sigmoid(x):
    return 1.0 / (1.0 + jnp.exp(-x))


def _layernorm(z, g, b):
    mu = jnp.mean(z, axis=-1, keepdims=True)
    zc = z - mu
    var = jnp.mean(zc * zc, axis=-1, keepdims=True)
    return zc * lax.rsqrt(var + LN_EPS) * g + b


def _mod_kernel(cond_ref, w_ref, b_ref, o_ref):
    cnd = cond_ref[...]
    s = cnd * _sigmoid(cnd)
    o_ref[...] = jnp.dot(s, w_ref[...], precision=HIGHEST, preferred_element_type=F32) + b_ref[...]


def _modulation(cond, ada_w, ada_b):
    out = pl.pallas_call(
        _mod_kernel,
        grid=(DEPTH, 6),
        in_specs=[
            pl.BlockSpec((N_MOD_ROWS, D_MODEL), lambda l, j: (0, 0)),
            pl.BlockSpec((None, D_MODEL, D_MODEL), lambda l, j: (l, 0, j)),
            pl.BlockSpec((None, 1, D_MODEL), lambda l, j: (l, 0, j)),
        ],
        out_specs=pl.BlockSpec((None, N_MOD_ROWS, D_MODEL), lambda l, j: (l, 0, j)),
        out_shape=jax.ShapeDtypeStruct((DEPTH, N_MOD_ROWS, 6 * D_MODEL), F32),
        compiler_params=_cparams(2),
        name="adaln_modulation",
    )(cond, ada_w, ada_b.reshape(DEPTH, 1, 6 * D_MODEL))
    return out.reshape(DEPTH * N_MOD_ROWS * 6, 1, D_MODEL)


def _rope_swap(y, g):
    lane = lax.broadcasted_iota(jnp.int32, y.shape, 1)
    first = (lane % (2 * g)) < g
    return jnp.where(first, pltpu.roll(y, LANES - g, 1), pltpu.roll(y, g, 1))


def _proj_kernel(*refs, use_mod, rope_g, q_tiles, q_scale):
    it = iter(refs)
    x_ref = next(it)
    x = x_ref[...]
    if use_mod:
        shift_ref, scale_ref = next(it), next(it)
        x = x * (1.0 + scale_ref[...]) + shift_ref[...]
    w_ref = next(it)
    y = jnp.dot(x.astype(BF16), w_ref[...], preferred_element_type=F32)
    col_scale = jnp.where(pl.program_id(0) < q_tiles, q_scale, 1.0) if q_tiles else None
    if rope_g:
        cos_ref, sin_ref = next(it), next(it)
        o_ref = next(it)
        cos, sin = cos_ref[...], sin_ref[...]
        for j in range(y.shape[1] // LANES):
            yb = y[:, j * LANES:(j + 1) * LANES]
            yb = yb * cos + _rope_swap(yb, rope_g) * sin
            if q_tiles:
                yb = yb * col_scale
            o_ref[:, j * LANES:(j + 1) * LANES] = yb.astype(o_ref.dtype)
    else:
        o_ref = next(it)
        if q_tiles:
            y = y * col_scale
        o_ref[...] = y.astype(o_ref.dtype)


def _project(x, w_bf16, *, tile_off=0, n_tiles, mods=None, layer=0, chunks=(0, 1), rope=None, q_cols=0,
             q_scale=1.0, out_dtype=F32, name):
    k_dim, n = w_bf16.shape
    tn = min(n, 1024)
    assert q_cols % tn == 0
    grid = (n // tn, n_tiles)
    in_specs = [pl.BlockSpec((TM, k_dim), lambda j, i: (i + tile_off, 0))]
    args = [x]
    if mods is not None:
        in_specs += [_mod_spec(layer, chunks[0], tile_off, 1), _mod_spec(layer, chunks[1], tile_off, 1)]
        args += [mods, mods]
    in_specs.append(pl.BlockSpec((k_dim, tn), lambda j, i: (0, j)))
    args.append(w_bf16)
    rope_g = 0
    if rope is not None:
        rope_g, cos, sin = rope
        tab = pl.BlockSpec((TM, LANES), lambda j, i: (i % TILES_PER_DEC, 0))
        in_specs += [tab, tab]
        args += [cos, sin]
    return pl.pallas_call(
        functools.partial(_proj_kernel, use_mod=mods is not None, rope_g=rope_g, q_tiles=q_cols // tn,
                          q_scale=q_scale),
        grid=grid,
        in_specs=in_specs,
        out_specs=pl.BlockSpec((TM, tn), lambda j, i: (i, j)),
        out_shape=jax.ShapeDtypeStruct((n_tiles * TM, n), out_dtype),
        compiler_params=_cparams(2),
        name=name,
    )(*args)


def _context_qkv(x, w_bf16, mods, layer, qk_scale, tag):
    common = dict(n_tiles=PROMPT_TILES, mods=mods, layer=layer)
    q = _project(x, w_bf16[:, :D_MODEL], name=tag + "_q_context", **common, **qk_scale)
    k = _project(x, w_bf16[:, D_MODEL:2 * D_MODEL], name=tag + "_k_context", **common)
    v = _project(x, w_bf16[:, 2 * D_MODEL:], name=tag + "_v_context", **common)
    return q, k, v


def _axial_tables(group_lanes, nf, lane_lo, lane_hi):
    del group_lanes
    t = jnp.arange(DEC_SEQ)
    lane = jnp.arange(LANES)
    rel = lane - lane_lo
    active = (lane >= lane_lo) & (lane < lane_hi)
    grp = rel // (2 * nf)
    pos = jnp.where((grp % 2 == 0)[None, :], (t // GRID_W)[:, None], (t % GRID_W)[:, None])
    inv = ROPE_BASE ** (-jnp.arange(nf, dtype=F32) / nf)
    ang = pos.astype(F32) * inv[rel % nf][None, :]
    cos = jnp.cos(ang)
    sin = jnp.sin(ang)
    sign = jnp.where((rel % (2 * nf)) < nf, -1.0, 1.0)[None, :]
    cos = jnp.where(active[None, :], cos, 1.0).astype(F32)
    sin = jnp.where(active[None, :], sin * sign, 0.0).astype(F32)
    return cos, sin


def _nt_dot(a, b):
    return lax.dot_general(a, b, (((1,), (1,)), ((), ())), preferred_element_type=F32)


def _half_values(vs, half):
    out = []
    for v in vs:
        lane = lax.broadcasted_iota(jnp.int32, v.shape, 1)
        out.append(jnp.where((lane < 64) if half == 0 else (lane >= 64), v, jnp.ones_like(v)))
    return out


def _pair_softmax_pv(scores_a, scores_b, vs):
    outs = []
    for half, scores in enumerate((scores_a, scores_b)):
        m = scores[0].max(axis=-1, keepdims=True)
        for s in scores[1:]:
            m = jnp.maximum(m, s.max(axis=-1, keepdims=True))
        acc = None
        for s, v in zip(scores, _half_values(vs, half)):
            o = jnp.dot(jnp.exp(s - m).astype(BF16), v, preferred_element_type=F32)
            acc = o if acc is None else acc + o
        outs.append(acc / pltpu.roll(acc, 64, 1))
    lane = lax.broadcasted_iota(jnp.int32, outs[0].shape, 1)
    return jnp.where(lane < 64, outs[0], outs[1])


def _joint_softmax(scores):
    m = scores[0].max(axis=-1, keepdims=True)
    for s in scores[1:]:
        m = jnp.maximum(m, s.max(axis=-1, keepdims=True))
    es = [jnp.exp(s - m) for s in scores]
    den = es[0].sum(axis=-1, keepdims=True)
    for e in es[1:]:
        den = den + e.sum(axis=-1, keepdims=True)
    return [e / den for e in es]


def _attn_kernel(*refs, mode, has_ctx, lam_init, hb):
    it = iter(refs)
    q_ref, k_ref, v_ref = next(it), next(it), next(it)
    kc_ref = vc_ref = None
    if has_ctx:
        kc_ref, vc_ref = next(it), next(it)
    if mode == "diff":
        lam_ref, subln_ref = next(it), next(it)
    o_ref = next(it)

    wq = 2 * LANES if mode == "mla" else LANES
    if mode == "diff":
        lp = lam_ref[...]
        lam = (jnp.exp(jnp.sum(lp[0:1] * lp[1:2], axis=-1, keepdims=True))
               - jnp.exp(jnp.sum(lp[2:3] * lp[3:4], axis=-1, keepdims=True)) + lam_init)

    for hblk in range(hb):
        qk_cols = slice(hblk * wq, (hblk + 1) * wq)
        v_cols = slice(hblk * LANES, (hblk + 1) * LANES)
        q = q_ref[:, qk_cols].astype(BF16)
        ks = [k_ref[:, qk_cols].astype(BF16)]
        vs = [v_ref[:, v_cols].astype(BF16)]
        if has_ctx:
            ks.append(kc_ref[:, qk_cols].astype(BF16))
            vs.append(vc_ref[:, v_cols].astype(BF16))

        if mode == "mla":
            halves = [(q[:, :LANES], [k[:, :LANES] for k in ks]), (q[:, LANES:], [k[:, LANES:] for k in ks])]
        else:
            lane_q = lax.broadcasted_iota(jnp.int32, q.shape, 1)
            zero = jnp.zeros_like(q)
            halves = [(jnp.where(lane_q < 64, q, zero), ks), (jnp.where(lane_q >= 64, q, zero), ks)]

        scores = [[_nt_dot(qh, k) for k in kh] for qh, kh in halves]
        if mode == "diff":
            p1 = _joint_softmax(scores[0])
            p2 = _joint_softmax(scores[1])
            o = None
            for a, b, v in zip(p1, p2, vs):
                part = jnp.dot((a - lam * b).astype(BF16), v, preferred_element_type=F32)
                o = part if o is None else o + part
            o = o * lax.rsqrt(jnp.mean(o * o, axis=-1, keepdims=True) + RMS_EPS) * subln_ref[...]
            o_ref[:, v_cols] = o * (1.0 - lam_init)
        else:
            o_ref[:, v_cols] = _pair_softmax_pv(scores[0], scores[1], vs)


def _attention(q_arr, k_arr, v_arr, *, mode, n_batch, seq, tq, q_col, k_col, v_col, row_off=0, v_row_off=None,
               ctx=None, lam=None, subln=None, lam_init=0.0, hb=1, name):
    assert mode != "mla" or (q_col == 0 and k_col == 0)
    n_blocks = D_MODEL // LANES
    assert n_blocks % hb == 0 and q_col % hb == 0 and k_col % hb == 0 and v_col % hb == 0
    wq = 2 * LANES if mode == "mla" else LANES
    wqb, wvb = wq * hb, LANES * hb
    nqb = seq // tq
    qoff = row_off * nqb
    if v_row_off is None:
        v_row_off = row_off
    in_specs = [
        pl.BlockSpec((tq, wqb), lambda b, h, i: (qoff + b * nqb + i, q_col // hb + h)),
        pl.BlockSpec((seq, wqb), lambda b, h, i: (row_off + b, k_col // hb + h)),
        pl.BlockSpec((seq, wvb), lambda b, h, i: (v_row_off + b, v_col // hb + h)),
    ]
    args = [q_arr, k_arr, v_arr]
    if ctx is not None:
        kc, vc, j = ctx
        in_specs += [
            pl.BlockSpec((None, None, PAST_LEN, wqb), lambda b, h, i: (b, j, 0, h)),
            pl.BlockSpec((None, None, PAST_LEN, wvb), lambda b, h, i: (b, j, 0, h)),
        ]
        args += [kc, vc]
    if mode == "diff":
        in_specs += [pl.BlockSpec((4, DIFF_HEAD_DIM), lambda b, h, i: (0, 0)),
                     pl.BlockSpec((1, LANES), lambda b, h, i: (0, 0))]
        args += [lam, subln]
    return pl.pallas_call(
        functools.partial(_attn_kernel, mode=mode, has_ctx=ctx is not None, lam_init=lam_init, hb=hb),
        grid=(n_batch, n_blocks // hb, nqb),
        in_specs=in_specs,
        out_specs=pl.BlockSpec((tq, wvb), lambda b, h, i: (b * nqb + i, h)),
        out_shape=jax.ShapeDtypeStruct((n_batch * seq, D_MODEL), F32),
        compiler_params=_cparams(3),
        name=name,
    )(*args)


def _na_bias_tables(rpb):
    cols = jnp.arange(GRID_W)
    c0 = jnp.clip(cols - WIN_C // 2, 0, GRID_W - WIN_C)
    col_ok = (cols[None, :] >= c0[:, None]) & (cols[None, :] < c0[:, None] + WIN_C)
    col_off = jnp.clip(cols[None, :] - cols[:, None] + (WIN_C - 1), 0, 2 * WIN_C - 2)
    hot_c = (col_off[:, :, None] == jnp.arange(2 * WIN_C - 1)).astype(F32)
    core = jnp.einsum("hab,ckb->hcak", rpb, hot_c, precision=HIGHEST)
    core = jnp.where(col_ok[None, :, None, :], core, NEG_INF)
    pad_hi = NA_BIAS_ROWS - NA_BIAS_PAD - (2 * WIN_R - 1)
    tab = jnp.pad(core, ((0, 0), (0, 0), (NA_BIAS_PAD, pad_hi), (0, 0)), constant_values=NEG_INF)
    tab = tab.reshape(NA_HEADS, GRID_W, NA_BIAS_ROWS * GRID_W)
    shifted = jnp.pad(tab[:, :, GRID_W:], ((0, 0), (0, 0), (0, GRID_W)), constant_values=NEG_INF)
    return jnp.stack([tab, shifted]).astype(F32)


def _na_block_bias(bias_ref, half, kind):
    nk = NA_KROWS * GRID_W
    lane = lax.broadcasted_iota(jnp.int32, (GRID_W, nk), 1)
    slabs = []
    for i in range(NA_QROWS):
        if kind == 0:
            ext0, lo, hi = WIN_R - 1 + NA_BIAS_PAD - i, 0, WIN_R
        elif kind == 1:
            ext0, lo, hi = WIN_R - 1 - WIN_R // 2 + NA_BIAS_PAD - i, i, i + WIN_R
        else:
            ext0, lo, hi = NA_BIAS_PAD - i, NA_KROWS - WIN_R, NA_KROWS
        start = (ext0 - ext0 % 2) * GRID_W
        slab = bias_ref[ext0 % 2, half, :, start:start + nk]
        slabs.append(jnp.where((lane >= lo * GRID_W) & (lane < hi * GRID_W), slab, NEG_INF))
    return jnp.concatenate(slabs, axis=0)


def _na_latent_kernel(q_ref, k_ref, v_ref, kc_ref, vc_ref, bias_ref, o_ref):
    rows_n = DEC_SEQ // GRID_W
    kc = kc_ref[...].astype(BF16)
    vc = vc_ref[...].astype(BF16)
    tq = NA_QROWS * GRID_W
    nk = NA_KROWS * GRID_W
    lane = lax.broadcasted_iota(jnp.int32, (tq, LANES), 1)
    biases = {}
    for blk in range(rows_n // NA_QROWS):
        q_row0 = blk * NA_QROWS
        if blk == 0:
            kind, k_row0 = 0, 0
        elif blk == rows_n // NA_QROWS - 1:
            kind, k_row0 = 2, rows_n - NA_KROWS
        else:
            kind, k_row0 = 1, q_row0 - WIN_R // 2
        q = q_ref[q_row0 * GRID_W:q_row0 * GRID_W + tq, :].astype(BF16)
        k = k_ref[k_row0 * GRID_W:k_row0 * GRID_W + nk, :].astype(BF16)
        v = v_ref[k_row0 * GRID_W:k_row0 * GRID_W + nk, :].astype(BF16)
        zero = jnp.zeros_like(q)
        scores = []
        for half in range(2):
            qh = jnp.where((lane < 64) if half == 0 else (lane >= 64), q, zero)
            if (kind, half) not in biases:
                biases[kind, half] = _na_block_bias(bias_ref, half, kind)
            scores.append([_nt_dot(qh, k) + biases[kind, half], _nt_dot(qh, kc)])
        o_ref[q_row0 * GRID_W:q_row0 * GRID_W + tq, :] = _pair_softmax_pv(scores[0], scores[1], [v, vc])


def _na_latent_attention(qkv, cache_k, cache_v, j, bias):
    nb = D_MODEL // LANES
    return pl.pallas_call(
        _na_latent_kernel,
        grid=(nb, DEC_BATCH),
        in_specs=[
            pl.BlockSpec((DEC_SEQ, LANES), lambda h, b: (b, h)),
            pl.BlockSpec((DEC_SEQ, LANES), lambda h, b: (b, nb + h)),
            pl.BlockSpec((DEC_SEQ, LANES), lambda h, b: (b, 2 * nb + h)),
            pl.BlockSpec((None, None, PAST_LEN, LANES), lambda h, b: (b, j, 0, h)),
            pl.BlockSpec((None, None, PAST_LEN, LANES), lambda h, b: (b, j, 0, h)),
            pl.BlockSpec((2, 2, GRID_W, NA_BIAS_ROWS * GRID_W), lambda h, b: (0, h, 0, 0)),
        ],
        out_specs=pl.BlockSpec((DEC_SEQ, LANES), lambda h, b: (b, h)),
        out_shape=jax.ShapeDtypeStruct((N_SAMPLE, D_MODEL), F32),
        compiler_params=_cparams(2),
        name="na_latent_attention",
    )(qkv, qkv, qkv, cache_k, cache_v, bias)


def _out_ln_kernel(op_ref, os_ref, w_ref, x_ref, gate_ref, g_ref, b_ref, xo_ref):
    def finish(o_ref):
        y = jnp.dot(o_ref[...].astype(BF16), w_ref[...], preferred_element_type=F32)
        z = DEEPNORM_ALPHA * x_ref[...] + gate_ref[...] * y
        xo_ref[...] = _layernorm(z, g_ref[...], b_ref[...])

    @pl.when(pl.program_id(0) < PROMPT_TILES)
    def _():
        finish(op_ref)

    @pl.when(pl.program_id(0) >= PROMPT_TILES)
    def _():
        finish(os_ref)


def _out_proj_ln(o_p, o_s, w_bf16, x, mods, layer, ln_g, ln_b):
    k_dim = w_bf16.shape[0]
    row = pl.BlockSpec((1, D_MODEL), lambda i: (0, 0))
    return pl.pallas_call(
        _out_ln_kernel,
        grid=(N_TOK // TM,),
        in_specs=[
            pl.BlockSpec((TM, k_dim), lambda i: (jnp.minimum(i, PROMPT_TILES - 1), 0)),
            pl.BlockSpec((TM, k_dim), lambda i: (jnp.maximum(i - PROMPT_TILES, 0), 0)),
            pl.BlockSpec((k_dim, D_MODEL), lambda i: (0, 0)),
            pl.BlockSpec((TM, D_MODEL), lambda i: (i, 0)),
            _mod_spec(layer, 2, 0, 0),
            row, row,
        ],
        out_specs=pl.BlockSpec((TM, D_MODEL), lambda i: (i, 0)),
        out_shape=jax.ShapeDtypeStruct((N_TOK, D_MODEL), F32),
        compiler_params=_cparams(1),
        name="out_proj_layernorm",
    )(o_p, o_s, w_bf16, x, mods, ln_g, ln_b)


def _route_kernel(x_ref, shift_ref, scale_ref, rw_ref, rb_ref, xs_ref, meta_ref, cnt_ref):
    @pl.when(pl.program_id(0) < N_CHUNKS)
    def _():
        _route_chunk(x_ref, shift_ref, scale_ref, rw_ref, rb_ref, xs_ref, meta_ref, cnt_ref)

    @pl.when(pl.program_id(0) == N_CHUNKS)
    def _():
        xs_ref[...] = jnp.zeros_like(xs_ref)


def _route_chunk(x_ref, shift_ref, scale_ref, rw_ref, rb_ref, xs_ref, meta_ref, cnt_ref):
    h = x_ref[...] * (1.0 + scale_ref[...]) + shift_ref[...]
    logits = jnp.dot(h, rw_ref[...], precision=HIGHEST, preferred_element_type=F32) + rb_ref[...]
    lane = lax.broadcasted_iota(jnp.int32, logits.shape, 1)
    work = logits
    top_v, hots = [], []
    for _ in range(TOP_K):
        m = work.max(axis=-1, keepdims=True)
        first = jnp.where(work == m, lane, N_EXPERTS).min(axis=-1, keepdims=True)
        hot = lane == first
        top_v.append(m)
        hots.append(jnp.where(hot, 1.0, 0.0))
        work = jnp.where(hot, -jnp.inf, work)
    es = [jnp.exp(v - top_v[0]) for v in top_v]
    den = es[0] + es[1] + es[2] + es[3]
    gates = [e / den for e in es]

    cnts = [hf.sum(axis=0, keepdims=True) for hf in hots]
    n_e = cnts[0] + cnts[1] + cnts[2] + cnts[3]
    pieces = jnp.floor((n_e + (MOE_PIECE - 1)) * (1.0 / MOE_PIECE))
    er = lax.broadcasted_iota(jnp.int32, (N_EXPERTS, N_EXPERTS), 0)
    ec = lax.broadcasted_iota(jnp.int32, (N_EXPERTS, N_EXPERTS), 1)
    upper = jnp.where(er < ec, 1.0, 0.0).astype(BF16)
    poff = jnp.dot(jnp.broadcast_to(pieces, (8, N_EXPERTS)).astype(BF16), upper,
                   preferred_element_type=F32)[0:1]

    tr = lax.broadcasted_iota(jnp.int32, (TM, TM), 0)
    tc = lax.broadcasted_iota(jnp.int32, (TM, TM), 1)
    lower = jnp.where(tc < tr, 1.0, 0.0).astype(BF16)
    run = poff * MOE_PIECE
    slots = []
    for k in range(TOP_K):
        prefix = jnp.dot(lower, hots[k].astype(BF16), preferred_element_type=F32)
        slots.append(jnp.sum(hots[k] * (run + prefix), axis=-1, keepdims=True))
        run = run + cnts[k]

    lane_w = lax.broadcasted_iota(jnp.int32, (TM, LANES), 1)
    meta = jnp.full((TM, LANES), -1.0, F32)
    for k in range(TOP_K):
        meta = jnp.where(lane_w == k, slots[k], meta)
        meta = jnp.where(lane_w == TOP_K + k, gates[k], meta)
    meta_ref[...] = meta[:, :2 * TOP_K]
    cnt_ref[...] = pieces

    slot_rows = meta.T
    srow = lax.broadcasted_iota(jnp.int32, (MOE_CHUNK_ROWS, TM), 0).astype(F32)
    p = jnp.zeros((MOE_CHUNK_ROWS, TM), F32)
    for k in range(TOP_K):
        p = p + jnp.where(srow == slot_rows[k:k + 1, :], 1.0, 0.0)
    xs_ref[...] = jnp.dot(p.astype(BF16), h.astype(BF16), preferred_element_type=F32)


def _route(x, mods, layer, rw, rb):
    assert 2 * EXP_TM <= MOE_CHUNK_ROWS
    last = N_CHUNKS - 1
    return pl.pallas_call(
        _route_kernel,
        grid=(N_CHUNKS + 1,),
        in_specs=[
            pl.BlockSpec((TM, D_MODEL), lambda i: (jnp.minimum(i, last), 0)),
            _mod_spec(layer, 3, 0, 0),
            _mod_spec(layer, 4, 0, 0),
            pl.BlockSpec((None, D_MODEL, N_EXPERTS), lambda i: (layer, 0, 0)),
            pl.BlockSpec((None, 1, N_EXPERTS), lambda i: (layer, 0, 0)),
        ],
        out_specs=[
            pl.BlockSpec((MOE_CHUNK_ROWS, D_MODEL), lambda i: (i, 0)),
            pl.BlockSpec((TM, 2 * TOP_K), lambda i: (jnp.minimum(i, last), 0)),
            pl.BlockSpec((None, 1, N_EXPERTS), lambda i: (jnp.minimum(i, last), 0, 0)),
        ],
        out_shape=[
            jax.ShapeDtypeStruct(((N_CHUNKS + 1) * MOE_CHUNK_ROWS, D_MODEL), F32),
            jax.ShapeDtypeStruct((N_TOK, 2 * TOP_K), F32),
            jax.ShapeDtypeStruct((N_CHUNKS, 1, N_EXPERTS), F32),
        ],
        compiler_params=_cparams(1),
        name="moe_route",
    )(x, mods, mods, rw, rb.reshape(DEPTH, 1, N_EXPERTS))


def _moe_schedule(pieces):
    e_ids = jnp.arange(N_EXPERTS, dtype=jnp.int32)
    npe = pieces.sum(axis=0)
    tiles_e = (npe + EXP_PPT - 1) // EXP_PPT
    tile_end = jnp.cumsum(tiles_e)
    tile_start = tile_end - tiles_e
    g = jnp.arange(MAX_TILES, dtype=jnp.int32)
    last_e = jnp.max(jnp.where(tiles_e > 0, e_ids, 0))
    te = jnp.minimum(jnp.sum(tile_end[None, :] <= g[:, None], axis=1), last_e).astype(jnp.int32)
    seg_len = pieces.T
    seg_base = tile_start[:, None] * EXP_PPT + jnp.cumsum(seg_len, axis=1) - seg_len
    poff = jnp.cumsum(pieces, axis=1) - pieces
    seg_pos = (jnp.arange(N_CHUNKS, dtype=jnp.int32)[:, None] * MOE_CHUNK_PIECES + poff).T
    hot = (te[:, None] == e_ids[None, :])[:, :, None]
    base, length, first = (jnp.sum(jnp.where(hot, tab[None], 0), axis=1)
                           for tab in (seg_base, seg_len, seg_pos))
    lane = jnp.arange(EXP_PPT, dtype=jnp.int32)[None, :]
    pos = (g[:, None] * EXP_PPT + lane)[:, :, None]
    hit = (base[:, None, :] <= pos) & (pos < (base + length)[:, None, :])
    src = jnp.sum(jnp.where(hit, first[:, None, :] + pos - base[:, None, :], 0), axis=-1)
    valid = jnp.any(hit, axis=-1)
    scratch = N_CHUNKS * MOE_CHUNK_PIECES + (g[:, None] % 2) * EXP_PPT + lane
    dst = jnp.where(valid, src, scratch).reshape(-1).astype(jnp.int32)
    src = jnp.where(valid, src, src[:, :1]).reshape(-1).astype(jnp.int32)
    return te, tile_end[-1:].astype(jnp.int32), src, dst


def _expert_kernel(te_ref, nt_ref, src_ref, dst_ref, xs_hbm, w1_ref, b1_ref, w2_ref, b2_ref, out_hbm,
                   xbuf, ybuf, w1s, w2s, sem_in, sem_out):
    g = pl.program_id(0)
    nt = nt_ref[0]
    slot = g % 2

    def in_copy(tile, i, s):
        row = pl.multiple_of(src_ref[tile * EXP_PPT + i] * MOE_PIECE, MOE_PIECE)
        return pltpu.make_async_copy(xs_hbm.at[pl.ds(row, MOE_PIECE), :],
                                     xbuf.at[s, pl.ds(i * MOE_PIECE, MOE_PIECE), :], sem_in.at[s])

    def out_copy(tile, i, s):
        row = pl.multiple_of(dst_ref[tile * EXP_PPT + i] * MOE_PIECE, MOE_PIECE)
        return pltpu.make_async_copy(ybuf.at[s, pl.ds(i * MOE_PIECE, MOE_PIECE), :],
                                     out_hbm.at[pl.ds(row, MOE_PIECE), :], sem_out.at[s])

    def gather(tile, s, wait):
        for i in range(EXP_PPT):
            cp = in_copy(tile, i, s)
            cp.wait() if wait else cp.start()

    def scatter(tile, s, wait):
        for i in range(EXP_PPT):
            cp = out_copy(tile, i, s)
            cp.wait() if wait else cp.start()

    @pl.when(g < nt)
    def _():
        @pl.when(g == 0)
        def _():
            gather(0, 0, False)

        @pl.when(g + 1 < nt)
        def _():
            gather(g + 1, 1 - slot, False)

        gather(g, slot, True)

        @pl.when(g >= 2)
        def _():
            scatter(g - 2, slot, True)

        @pl.when((g == 0) | (te_ref[g] != te_ref[jnp.maximum(g - 1, 0)]))
        def _():
            w1s[...] = w1_ref[...].astype(BF16)
            w2s[...] = w2_ref[...].astype(BF16)

        gu = jnp.dot(xbuf[slot].astype(BF16), w1s[...], preferred_element_type=F32) + b1_ref[...]
        gl = jnp.minimum(gu[:, :D_EXPERT], SWIGLU_LIMIT)
        ul = jnp.clip(gu[:, D_EXPERT:], -SWIGLU_LIMIT, SWIGLU_LIMIT)
        act = (ul + 1.0) * (gl * _sigmoid(SWIGLU_ALPHA * gl))
        ybuf[slot] = jnp.dot(act.astype(BF16), w2s[...], preferred_element_type=F32) + b2_ref[...]
        scatter(g, slot, False)

        @pl.when(g == nt - 1)
        def _():
            @pl.when(g >= 1)
            def _():
                scatter(g - 1, 1 - slot, True)
            scatter(g, slot, True)


def _experts(xs, te, nt, src, dst, layer, w1, b1, w2, b2):
    def expert_block(g, te, nt, src, dst):
        return (layer, te[g], 0, 0)

    grid_spec = pltpu.PrefetchScalarGridSpec(
        num_scalar_prefetch=4,
        grid=(MAX_TILES,),
        in_specs=[
            pl.BlockSpec(memory_space=pl.ANY),
            pl.BlockSpec((None, None, D_MODEL, 2 * D_EXPERT), expert_block),
            pl.BlockSpec((None, None, 1, 2 * D_EXPERT), expert_block),
            pl.BlockSpec((None, None, D_EXPERT, D_MODEL), expert_block),
            pl.BlockSpec((None, None, 1, D_MODEL), expert_block),
        ],
        out_specs=pl.BlockSpec(memory_space=pl.ANY),
        scratch_shapes=[
            pltpu.VMEM((2, EXP_TM, D_MODEL), F32),
            pltpu.VMEM((2, EXP_TM, D_MODEL), F32),
            pltpu.VMEM((D_MODEL, 2 * D_EXPERT), BF16),
            pltpu.VMEM((D_EXPERT, D_MODEL), BF16),
            pltpu.SemaphoreType.DMA((2,)),
            pltpu.SemaphoreType.DMA((2,)),
        ],
    )
    return pl.pallas_call(
        _expert_kernel,
        grid_spec=grid_spec,
        out_shape=jax.ShapeDtypeStruct(xs.shape, F32),
        input_output_aliases={4: 0},
        compiler_params=_cparams(1),
        name="moe_experts",
    )(te, nt, src, dst, xs, w1, b1.reshape(DEPTH, N_EXPERTS, 1, 2 * D_EXPERT),
      w2, b2.reshape(DEPTH, N_EXPERTS, 1, D_MODEL))


def _combine_kernel(ys_ref, meta_ref, x_ref, gate_ref, g_ref, b_ref, xo_ref):
    meta = meta_ref[...]
    col = lax.broadcasted_iota(jnp.int32, (TM, MOE_CHUNK_ROWS), 1).astype(F32)
    gmat = jnp.zeros((TM, MOE_CHUNK_ROWS), F32)
    for k in range(TOP_K):
        gmat = gmat + jnp.where(col == meta[:, k:k + 1], meta[:, TOP_K + k:TOP_K + k + 1], 0.0)
    y = jnp.dot(gmat.astype(BF16), ys_ref[...].astype(BF16), preferred_element_type=F32)
    z = DEEPNORM_ALPHA * x_ref[...] + gate_ref[...] * y
    xo_ref[...] = _layernorm(z, g_ref[...], b_ref[...])


def _combine_ln(ys, meta, x, mods, layer, ln_g, ln_b):
    row = pl.BlockSpec((1, D_MODEL), lambda i: (0, 0))
    return pl.pallas_call(
        _combine_kernel,
        grid=(N_CHUNKS,),
        in_specs=[
            pl.BlockSpec((MOE_CHUNK_ROWS, D_MODEL), lambda i: (i, 0)),
            pl.BlockSpec((TM, 2 * TOP_K), lambda i: (i, 0)),
            pl.BlockSpec((TM, D_MODEL), lambda i: (i, 0)),
            _mod_spec(layer, 5, 0, 0),
            row, row,
        ],
        out_specs=pl.BlockSpec((TM, D_MODEL), lambda i: (i, 0)),
        out_shape=jax.ShapeDtypeStruct((N_TOK, D_MODEL), F32),
        compiler_params=_cparams(1),
        name="moe_combine_layernorm",
    )(ys, meta, x, mods, ln_g, ln_b)


def _moe(x, mods, layer, rw, rb, w1, b1, w2, b2, ln_g, ln_b):
    xs, meta, pieces = _route(x, mods, layer, rw, rb)
    te, nt, src, dst = _moe_schedule(pieces.reshape(N_CHUNKS, N_EXPERTS).astype(jnp.int32))
    ys = _experts(xs, te, nt, src, dst, layer, w1, b1, w2, b2)
    return _combine_ln(ys, meta, x, mods, layer, ln_g, ln_b)


def _mla_norm_kernel(p_ref, qn_ref, kvn_ref, cq_ref, ckvkr_ref):
    p = p_ref[...]
    cq = p[:, :MLA_Q_RANK]
    cq_ref[...] = cq * lax.rsqrt(jnp.mean(cq * cq, axis=-1, keepdims=True) + RMS_EPS) * qn_ref[...]
    ckv = p[:, MLA_Q_RANK:MLA_Q_RANK + MLA_KV_RANK]
    ckvkr_ref[:, :MLA_KV_RANK] = (ckv * lax.rsqrt(jnp.mean(ckv * ckv, axis=-1, keepdims=True) + RMS_EPS)
                                  * kvn_ref[...])
    ckvkr_ref[:, MLA_KV_RANK:] = p[:, MLA_Q_RANK + MLA_KV_RANK:]


def _mla_norm(proj, q_norm, kv_norm):
    return pl.pallas_call(
        _mla_norm_kernel,
        grid=(N_TOK // TM,),
        in_specs=[
            pl.BlockSpec((TM, 512), lambda i: (i, 0)),
            pl.BlockSpec((1, MLA_Q_RANK), lambda i: (0, 0)),
            pl.BlockSpec((1, MLA_KV_RANK), lambda i: (0, 0)),
        ],
        out_specs=[pl.BlockSpec((TM, MLA_Q_RANK), lambda i: (i, 0)),
                   pl.BlockSpec((TM, 2 * MLA_KV_RANK), lambda i: (i, 0))],
        out_shape=[jax.ShapeDtypeStruct((N_TOK, MLA_Q_RANK), F32),
                   jax.ShapeDtypeStruct((N_TOK, 2 * MLA_KV_RANK), F32)],
        compiler_params=_cparams(1),
        name="mla_norm",
    )(proj, q_norm, kv_norm)


def _mla_weights(w_in, w_uq, w_ukv):
    pad = 512 - w_in.shape[1]
    w_in_p = jnp.pad(w_in, ((0, 0), (0, pad)))
    wq = w_uq.reshape(MLA_Q_RANK, MLA_HEADS, MLA_NOPE + MLA_ROPE)
    wq = jnp.pad(wq, ((0, 0), (0, 0), (0, LANES - MLA_NOPE - MLA_ROPE))).reshape(MLA_Q_RANK, MLA_HEADS * LANES)
    wkv = w_ukv.reshape(MLA_KV_RANK, MLA_HEADS, MLA_NOPE + MLA_V)
    wkn = jnp.pad(wkv[:, :, :MLA_NOPE], ((0, 0), (0, 0), (0, LANES - MLA_NOPE)))
    eye = jnp.pad(jnp.eye(MLA_ROPE, dtype=F32), ((0, 0), (MLA_NOPE, LANES - MLA_NOPE - MLA_ROPE)))
    wkr = jnp.broadcast_to(eye[:, None, :], (MLA_ROPE, MLA_HEADS, LANES))
    zeros = jnp.zeros((2 * MLA_KV_RANK - MLA_KV_RANK - MLA_ROPE, MLA_HEADS, LANES), F32)
    wk = jnp.concatenate([wkn, wkr, zeros], axis=0).reshape(2 * MLA_KV_RANK, MLA_HEADS * LANES)
    wv = jnp.pad(wkv[:, :, MLA_NOPE:].reshape(MLA_KV_RANK, MLA_HEADS * MLA_V), ((0, MLA_KV_RANK), (0, 0)))
    return w_in_p.astype(BF16), wq.astype(BF16), wk.astype(BF16), wv.astype(BF16)


def kernel(x_prompt, x_sample, cache_na_k, cache_na_v, cache_diff_k, cache_diff_v, cache_mla_ckv, cache_mla_kr,
           c, c_ctx, ada_w, ada_b, ln_g, ln_b, na_w_qkv, na_rpb, na_w_o, diff_w_qkv, diff_lambda, diff_subln,
           diff_w_o, mla_w_in, mla_q_norm, mla_w_uq, mla_kv_norm, mla_w_ukv, mla_w_o, router_w, router_b,
           moe_w1, moe_b1, moe_w2, moe_b2):
    x = jnp.concatenate([x_prompt.reshape(N_PROMPT, D_MODEL), x_sample.reshape(N_SAMPLE, D_MODEL)], axis=0)
    cond = jnp.concatenate([c_ctx[None, :], c, jnp.zeros((N_MOD_ROWS - 1 - DEC_BATCH, D_MODEL), F32)], axis=0)
    mods = _modulation(cond, ada_w, ada_b)

    n_na = na_w_qkv.shape[0]
    n_diff = diff_w_qkv.shape[0]
    n_mla = mla_w_in.shape[0]
    cache_na_k = cache_na_k.reshape(DEC_BATCH, n_na, PAST_LEN, D_MODEL)
    cache_na_v = cache_na_v.reshape(DEC_BATCH, n_na, PAST_LEN, D_MODEL)
    cache_diff_k = cache_diff_k.reshape(DEC_BATCH, n_diff, PAST_LEN, D_MODEL)
    cache_diff_v = cache_diff_v.reshape(DEC_BATCH, n_diff, PAST_LEN, D_MODEL)
    all_tiles = N_TOK // TM
    nb = D_MODEL // LANES
    dec_off = N_PROMPT // DEC_SEQ

    st_na_k, st_na_v, st_diff_k, st_diff_v, st_mla_ckv, st_mla_kr = [], [], [], [], [], []
    for i in range(DEPTH):
        kind, j = i % N_MIXERS, i // N_MIXERS
        latent = dict(tile_off=PROMPT_TILES, mods=mods, layer=i)
        if kind == 0:
            w = na_w_qkv[j].astype(BF16)
            qk_scale = dict(q_cols=D_MODEL, q_scale=NA_HEAD_DIM ** -0.5, out_dtype=BF16)
            q_p, k_p, v_p = _context_qkv(x, w, mods, i, qk_scale, "na")
            st_na_k.append(k_p)
            st_na_v.append(v_p)
            qkv_s = _project(x, w, n_tiles=SAMPLE_TILES, name="na_qkv_latent", **latent, **qk_scale)
            o_p = _attention(q_p, k_p, v_p, mode="pair", n_batch=BATCH, seq=SEQ, tq=SEQ, q_col=0, k_col=0,
                             v_col=0, hb=nb, name="na_context_attention")
            o_s = _na_latent_attention(qkv_s, cache_na_k, cache_na_v, j, _na_bias_tables(na_rpb[j]))
            w_o = na_w_o[j]
        elif kind == 1:
            lam_init = 0.8 - 0.6 * math.exp(-0.3 * i)
            w = diff_w_qkv[j].astype(BF16)
            cos, sin = _axial_tables(LANES, DIFF_HEAD_DIM // 4, 0, LANES)
            qk_scale = dict(q_cols=D_MODEL, q_scale=DIFF_HEAD_DIM ** -0.5, out_dtype=BF16)
            q_p, k_p, v_p = _context_qkv(x, w, mods, i, qk_scale, "diff")
            st_diff_k.append(k_p)
            st_diff_v.append(v_p)
            qk_s = _project(x, w[:, :2 * D_MODEL], n_tiles=SAMPLE_TILES, rope=(DIFF_HEAD_DIM // 4, cos, sin),
                            name="diff_qk_latent", **latent, **qk_scale)
            v_s = _project(x, w[:, 2 * D_MODEL:], n_tiles=SAMPLE_TILES, out_dtype=BF16, name="diff_v_latent",
                           **latent)
            common = dict(mode="diff", lam=diff_lambda[j], subln=diff_subln[j][None, :], lam_init=lam_init)
            o_p = _attention(q_p, k_p, v_p, n_batch=BATCH, seq=SEQ, tq=SEQ, q_col=0, k_col=0, v_col=0, hb=nb,
                             name="diff_context_attention", **common)
            o_s = _attention(qk_s, qk_s, v_s, n_batch=DEC_BATCH, seq=DEC_SEQ, tq=TM, q_col=0, k_col=nb, v_col=0,
                             ctx=(cache_diff_k, cache_diff_v, j), hb=2, name="diff_latent_attention", **common)
            w_o = diff_w_o[j]
        else:
            w_in, wq, wk, wv = _mla_weights(mla_w_in[j], mla_w_uq[j], mla_w_ukv[j])
            cos, sin = _axial_tables(LANES, MLA_ROPE // 4, MLA_NOPE, MLA_NOPE + MLA_ROPE)
            rope = (MLA_ROPE // 4, cos, sin)
            proj = _project(x, w_in, n_tiles=all_tiles, mods=mods, layer=i, name="mla_down")
            cq, ckvkr = _mla_norm(proj, mla_q_norm[j][None, :], mla_kv_norm[j][None, :])
            st_mla_ckv.append(ckvkr[:N_PROMPT, :MLA_KV_RANK])
            st_mla_kr.append(ckvkr[:N_PROMPT, MLA_KV_RANK:MLA_KV_RANK + MLA_ROPE])
            q_scale = dict(q_cols=MLA_HEADS * LANES, q_scale=(MLA_NOPE + MLA_ROPE) ** -0.5, out_dtype=BF16)
            q_p = _project(cq, wq, n_tiles=PROMPT_TILES, name="mla_q_context", **q_scale)
            k_p = _project(ckvkr, wk, n_tiles=PROMPT_TILES, out_dtype=BF16, name="mla_k_context")
            q_s = _project(cq, wq, tile_off=PROMPT_TILES, n_tiles=SAMPLE_TILES, rope=rope, name="mla_q_latent",
                           **q_scale)
            k_s = _project(ckvkr, wk, tile_off=PROMPT_TILES, n_tiles=SAMPLE_TILES, rope=rope, out_dtype=BF16,
                           name="mla_k_latent")
            v_all = _project(ckvkr, wv, n_tiles=all_tiles, out_dtype=BF16, name="mla_v")
            ctx_in = jnp.concatenate(
                [cache_mla_ckv[:, j], cache_mla_kr[:, j],
                 jnp.zeros((DEC_BATCH, PAST_LEN, 2 * MLA_KV_RANK - MLA_KV_RANK - MLA_ROPE), F32)],
                axis=-1).reshape(DEC_BATCH * PAST_LEN, 2 * MLA_KV_RANK)
            n_ctx_tiles = DEC_BATCH * PAST_LEN // TM
            k_c = _project(ctx_in, wk, n_tiles=n_ctx_tiles, out_dtype=BF16, name="mla_k_cache")
            v_c = _project(ctx_in, wv, n_tiles=n_ctx_tiles, out_dtype=BF16, name="mla_v_cache")
            k_c = k_c.reshape(DEC_BATCH, 1, PAST_LEN, MLA_HEADS * LANES)
            v_c = v_c.reshape(DEC_BATCH, 1, PAST_LEN, D_MODEL)
            o_p = _attention(q_p, k_p, v_all, mode="mla", n_batch=BATCH, seq=SEQ, tq=SEQ, q_col=0, k_col=0,
                             v_col=0, hb=nb, name="mla_context_attention")
            o_s = _attention(q_s, k_s, v_all, mode="mla", n_batch=DEC_BATCH, seq=DEC_SEQ, tq=TM, q_col=0, k_col=0,
                             v_col=0, v_row_off=dec_off, ctx=(k_c, v_c, 0), hb=2, name="mla_latent_attention")
            w_o = mla_w_o[j]

        x = _out_proj_ln(o_p, o_s, w_o.astype(BF16), x, mods, i, ln_g[i, 0][None, :], ln_b[i, 0][None, :])
        x = _moe(x, mods, i, router_w, router_b, moe_w1, moe_b1, moe_w2, moe_b2,
                 ln_g[i, 1][None, :], ln_b[i, 1][None, :])

    y_prompt = x[:N_PROMPT].reshape(BATCH, SEQ, D_MODEL)
    y_sample = x[N_PROMPT:].reshape(DEC_BATCH, DEC_SEQ, D_MODEL)

    def stack(parts, *tail):
        return jnp.stack([p.reshape(BATCH, SEQ, *tail) for p in parts], axis=1)

    return (y_prompt, y_sample,
            stack(st_na_k, NA_HEADS, NA_HEAD_DIM), stack(st_na_v, NA_HEADS, NA_HEAD_DIM),
            stack(st_diff_k, DIFF_HEADS, 2 * DIFF_HEAD_DIM), stack(st_diff_v, DIFF_HEADS, 2 * DIFF_HEAD_DIM),
            stack(st_mla_ckv, MLA_KV_RANK), stack(st_mla_kr, MLA_ROPE))
```

```python
import functools
import math

import jax
import jax.numpy as jnp
from jax import lax
from jax.experimental import pallas as pl
from jax.experimental.pallas import tpu as pltpu

F32 = jnp.float32
BF16 = jnp.bfloat16
HIGHEST = lax.Precision.HIGHEST

D_MODEL = 1024
BATCH = 32
SEQ = 256
DEPTH = 4
DEC_BATCH = 4
DEC_SEQ = 2048
PAST_LEN = 512
GRID_W = 64
N_MIXERS = 3
NA_HEADS = 16
NA_HEAD_DIM = 64
WIN_R = 8
WIN_C = 16
DIFF_HEADS = 8
DIFF_HEAD_DIM = 64
MLA_HEADS = 16
MLA_Q_RANK = 256
MLA_KV_RANK = 128
MLA_NOPE = 64
MLA_ROPE = 32
MLA_V = 64
N_EXPERTS = 32
TOP_K = 4
D_EXPERT = 1024
SWIGLU_LIMIT = 7.0
SWIGLU_ALPHA = 1.702
ROPE_BASE = 10000.0
LN_EPS = 1e-5
RMS_EPS = 1e-6
NEG_INF = -1e30
DEEPNORM_ALPHA = (2 * DEPTH) ** 0.25

LANES = 128
TM = 256
N_PROMPT = BATCH * SEQ
N_SAMPLE = DEC_BATCH * DEC_SEQ
N_TOK = N_PROMPT + N_SAMPLE
PROMPT_TILES = N_PROMPT // TM
SAMPLE_TILES = N_SAMPLE // TM
TILES_PER_DEC = DEC_SEQ // TM
N_MOD_ROWS = 8
NA_QROWS = 4
NA_KROWS = NA_QROWS + WIN_R - 1
NA_BIAS_PAD = NA_QROWS - 1
NA_BIAS_ROWS = WIN_R - 1 + NA_BIAS_PAD + NA_KROWS
N_CHUNKS = N_TOK // TM
MOE_PIECE = 8
MAX_CHUNK_PIECES = (TM * TOP_K + N_EXPERTS * (MOE_PIECE - 1)) // MOE_PIECE
MOE_CHUNK_PIECES = 160
MOE_CHUNK_ROWS = MOE_CHUNK_PIECES * MOE_PIECE
EXP_TM = 512
EXP_PPT = EXP_TM // MOE_PIECE
MAX_TILES = N_CHUNKS * MAX_CHUNK_PIECES // EXP_PPT + N_EXPERTS
VMEM_LIMIT = 56 * 1024 * 1024


def _cparams(n_axes, vmem=VMEM_LIMIT):
    return pltpu.CompilerParams(dimension_semantics=("arbitrary",) * n_axes, vmem_limit_bytes=vmem)


def _mod_row(t):
    return jnp.where(t < PROMPT_TILES, 0, 1 + (t - PROMPT_TILES) // TILES_PER_DEC)


def _mod_spec(layer, chunk, tile_off, tile_axis):
    def index(*g):
        row = _mod_row(g[tile_axis] + tile_off)
        return ((layer * N_MOD_ROWS + row) * 6 + chunk, 0, 0)
    return pl.BlockSpec((None, 1, D_MODEL), index)


def _sigmoid(x):
    return 1.0 / (1.0 + jnp.exp(-x))


def _layernorm(z, g, b):
    mu = jnp.mean(z, axis=-1, keepdims=True)
    zc = z - mu
    var = jnp.mean(zc * zc, axis=-1, keepdims=True)
    return zc * lax.rsqrt(var + LN_EPS) * g + b


def _mod_kernel(cond_ref, w_ref, b_ref, o_ref):
    cnd = cond_ref[...]
    s = cnd * _sigmoid(cnd)
    o_ref[...] = jnp.dot(s, w_ref[...], precision=HIGHEST, preferred_element_type=F32) + b_ref[...]


def _modulation(cond, ada_w, ada_b):
    out = pl.pallas_call(
        _mod_kernel,
        grid=(DEPTH, 6),
        in_specs=[
            pl.BlockSpec((N_MOD_ROWS, D_MODEL), lambda l, j: (0, 0)),
            pl.BlockSpec((None, D_MODEL, D_MODEL), lambda l, j: (l, 0, j)),
            pl.BlockSpec((None, 1, D_MODEL), lambda l, j: (l, 0, j)),
        ],
        out_specs=pl.BlockSpec((None, N_MOD_ROWS, D_MODEL), lambda l, j: (l, 0, j)),
        out_shape=jax.ShapeDtypeStruct((DEPTH, N_MOD_ROWS, 6 * D_MODEL), F32),
        compiler_params=_cparams(2),
        name="adaln_modulation",
    )(cond, ada_w, ada_b.reshape(DEPTH, 1, 6 * D_MODEL))
    return out.reshape(DEPTH * N_MOD_ROWS * 6, 1, D_MODEL)


def _rope_swap(y, g):
    lane = lax.broadcasted_iota(jnp.int32, y.shape, 1)
    first = (lane % (2 * g)) < g
    return jnp.where(first, pltpu.roll(y, LANES - g, 1), pltpu.roll(y, g, 1))


def _proj_kernel(*refs, use_mod, rope_g, q_tiles, q_scale):
    it = iter(refs)
    x_ref = next(it)
    x = x_ref[...]
    if use_mod:
        shift_ref, scale_ref = next(it), next(it)
        x = x * (1.0 + scale_ref[...]) + shift_ref[...]
    w_ref = next(it)
    y = jnp.dot(x.astype(BF16), w_ref[...], preferred_element_type=F32)
    col_scale = jnp.where(pl.program_id(0) < q_tiles, q_scale, 1.0) if q_tiles else None
    if rope_g:
        cos_ref, sin_ref = next(it), next(it)
        o_ref = next(it)
        cos, sin = cos_ref[...], sin_ref[...]
        for j in range(y.shape[1] // LANES):
            yb = y[:, j * LANES:(j + 1) * LANES]
            yb = yb * cos + _rope_swap(yb, rope_g) * sin
            if q_tiles:
                yb = yb * col_scale
            o_ref[:, j * LANES:(j + 1) * LANES] = yb.astype(o_ref.dtype)
    else:
        o_ref = next(it)
        if q_tiles:
            y = y * col_scale
        o_ref[...] = y.astype(o_ref.dtype)


def _project(x, w_bf16, *, tile_off=0, n_tiles, mods=None, layer=0, chunks=(0, 1), rope=None, q_cols=0,
             q_scale=1.0, out_dtype=F32, name):
    k_dim, n = w_bf16.shape
    tn = min(n, 1024)
    assert q_cols % tn == 0
    grid = (n // tn, n_tiles)
    in_specs = [pl.BlockSpec((TM, k_dim), lambda j, i: (i + tile_off, 0))]
    args = [x]
    if mods is not None:
        in_specs += [_mod_spec(layer, chunks[0], tile_off, 1), _mod_spec(layer, chunks[1], tile_off, 1)]
        args += [mods, mods]
    in_specs.append(pl.BlockSpec((k_dim, tn), lambda j, i: (0, j)))
    args.append(w_bf16)
    rope_g = 0
    if rope is not None:
        rope_g, cos, sin = rope
        tab = pl.BlockSpec((TM, LANES), lambda j, i: (i % TILES_PER_DEC, 0))
        in_specs += [tab, tab]
        args += [cos, sin]
    return pl.pallas_call(
        functools.partial(_proj_kernel, use_mod=mods is not None, rope_g=rope_g, q_tiles=q_cols // tn,
                          q_scale=q_scale),
        grid=grid,
        in_specs=in_specs,
        out_specs=pl.BlockSpec((TM, tn), lambda j, i: (i, j)),
        out_shape=jax.ShapeDtypeStruct((n_tiles * TM, n), out_dtype),
        compiler_params=_cparams(2),
        name=name,
    )(*args)


def _context_qkv(x, w_bf16, mods, layer, qk_scale, tag):
    common = dict(n_tiles=PROMPT_TILES, mods=mods, layer=layer)
    q = _project(x, w_bf16[:, :D_MODEL], name=tag + "_q_context", **common, **qk_scale)
    k = _project(x, w_bf16[:, D_MODEL:2 * D_MODEL], name=tag + "_k_context", **common)
    v = _project(x, w_bf16[:, 2 * D_MODEL:], name=tag + "_v_context", **common)
    return q, k, v


def _axial_tables(group_lanes, nf, lane_lo, lane_hi):
    del group_lanes
    t = jnp.arange(DEC_SEQ)
    lane = jnp.arange(LANES)
    rel = lane - lane_lo
    active = (lane >= lane_lo) & (lane < lane_hi)
    grp = rel // (2 * nf)
    pos = jnp.where((grp % 2 == 0)[None, :], (t // GRID_W)[:, None], (t % GRID_W)[:, None])
    inv = ROPE_BASE ** (-jnp.arange(nf, dtype=F32) / nf)
    ang = pos.astype(F32) * inv[rel % nf][None, :]
    cos = jnp.cos(ang)
    sin = jnp.sin(ang)
    sign = jnp.where((rel % (2 * nf)) < nf, -1.0, 1.0)[None, :]
    cos = jnp.where(active[None, :], cos, 1.0).astype(F32)
    sin = jnp.where(active[None, :], sin * sign, 0.0).astype(F32)
    return cos, sin


def _nt_dot(a, b):
    return lax.dot_general(a, b, (((1,), (1,)), ((), ())), preferred_element_type=F32)


def _half_values(vs, half):
    lane = lax.broadcasted_iota(jnp.int32, (1, LANES), 1)
    keep = jnp.where((lane < 64) if half == 0 else (lane >= 64), 1.0, 0.0).astype(BF16)
    return [v * keep + (1 - keep) for v in vs]


def _pair_softmax_pv(scores_a, scores_b, vs):
    outs = []
    for half, scores in enumerate((scores_a, scores_b)):
        m = scores[0].max(axis=-1, keepdims=True)
        for s in scores[1:]:
            m = jnp.maximum(m, s.max(axis=-1, keepdims=True))
        acc = None
        for s, v in zip(scores, _half_values(vs, half)):
            o = jnp.dot(jnp.exp(s - m).astype(BF16), v, preferred_element_type=F32)
            acc = o if acc is None else acc + o
        outs.append(acc / pltpu.roll(acc, 64, 1))
    lane = lax.broadcasted_iota(jnp.int32, outs[0].shape, 1)
    return jnp.where(lane < 64, outs[0], outs[1])


def _softmax_pv(scores, vs):
    m = scores[0].max(axis=-1, keepdims=True)
    for s in scores[1:]:
        m = jnp.maximum(m, s.max(axis=-1, keepdims=True))
    acc, den = None, None
    for s, v in zip(scores, vs):
        e = jnp.exp(s - m)
        d = e.sum(axis=-1, keepdims=True)
        o = jnp.dot(e.astype(BF16), v, preferred_element_type=F32)
        acc = o if acc is None else acc + o
        den = d if den is None else den + d
    return acc / den


def _attn_kernel(*refs, mode, has_ctx, lam_init, hb):
    it = iter(refs)
    q_ref, k_ref, v_ref = next(it), next(it), next(it)
    kc_ref = vc_ref = None
    if has_ctx:
        kc_ref, vc_ref = next(it), next(it)
    if mode == "diff":
        lam_ref, subln_ref = next(it), next(it)
    o_ref = next(it)

    wq = 2 * LANES if mode == "mla" else LANES
    if mode == "diff":
        lp = lam_ref[...]
        lam = (jnp.exp(jnp.sum(lp[0:1] * lp[1:2], axis=-1, keepdims=True))
               - jnp.exp(jnp.sum(lp[2:3] * lp[3:4], axis=-1, keepdims=True)) + lam_init)

    for hblk in range(hb):
        qk_cols = slice(hblk * wq, (hblk + 1) * wq)
        v_cols = slice(hblk * LANES, (hblk + 1) * LANES)
        q = q_ref[:, qk_cols].astype(BF16)
        ks = [k_ref[:, qk_cols].astype(BF16)]
        vs = [v_ref[:, v_cols].astype(BF16)]
        if has_ctx:
            ks.append(kc_ref[:, qk_cols].astype(BF16))
            vs.append(vc_ref[:, v_cols].astype(BF16))

        if mode == "mla":
            halves = [(q[:, :LANES], [k[:, :LANES] for k in ks]), (q[:, LANES:], [k[:, LANES:] for k in ks])]
        else:
            lane_q = lax.broadcasted_iota(jnp.int32, q.shape, 1)
            zero = jnp.zeros_like(q)
            halves = [(jnp.where(lane_q < 64, q, zero), ks), (jnp.where(lane_q >= 64, q, zero), ks)]

        scores = [[_nt_dot(qh, k) for k in kh] for qh, kh in halves]
        if mode == "diff":
            o = _softmax_pv(scores[0], vs) - lam * _softmax_pv(scores[1], vs)
            o = o * lax.rsqrt(jnp.mean(o * o, axis=-1, keepdims=True) + RMS_EPS) * subln_ref[...]
            o_ref[:, v_cols] = o * (1.0 - lam_init)
        else:
            o_ref[:, v_cols] = _pair_softmax_pv(scores[0], scores[1], vs)


def _attention(q_arr, k_arr, v_arr, *, mode, n_batch, seq, tq, q_col, k_col, v_col, row_off=0, v_row_off=None,
               ctx=None, lam=None, subln=None, lam_init=0.0, hb=1, name):
    assert mode != "mla" or (q_col == 0 and k_col == 0)
    n_blocks = D_MODEL // LANES
    assert n_blocks % hb == 0 and q_col % hb == 0 and k_col % hb == 0 and v_col % hb == 0
    wq = 2 * LANES if mode == "mla" else LANES
    wqb, wvb = wq * hb, LANES * hb
    nqb = seq // tq
    qoff = row_off * nqb
    if v_row_off is None:
        v_row_off = row_off
    in_specs = [
        pl.BlockSpec((tq, wqb), lambda b, h, i: (qoff + b * nqb + i, q_col // hb + h)),
        pl.BlockSpec((seq, wqb), lambda b, h, i: (row_off + b, k_col // hb + h)),
        pl.BlockSpec((seq, wvb), lambda b, h, i: (v_row_off + b, v_col // hb + h)),
    ]
    args = [q_arr, k_arr, v_arr]
    if ctx is not None:
        kc, vc, j = ctx
        in_specs += [
            pl.BlockSpec((None, None, PAST_LEN, wqb), lambda b, h, i: (b, j, 0, h)),
            pl.BlockSpec((None, None, PAST_LEN, wvb), lambda b, h, i: (b, j, 0, h)),
        ]
        args += [kc, vc]
    if mode == "diff":
        in_specs += [pl.BlockSpec((4, DIFF_HEAD_DIM), lambda b, h, i: (0, 0)),
                     pl.BlockSpec((1, LANES), lambda b, h, i: (0, 0))]
        args += [lam, subln]
    return pl.pallas_call(
        functools.partial(_attn_kernel, mode=mode, has_ctx=ctx is not None, lam_init=lam_init, hb=hb),
        grid=(n_batch, n_blocks // hb, nqb),
        in_specs=in_specs,
        out_specs=pl.BlockSpec((tq, wvb), lambda b, h, i: (b * nqb + i, h)),
        out_shape=jax.ShapeDtypeStruct((n_batch * seq, D_MODEL), F32),
        compiler_params=_cparams(3),
        name=name,
    )(*args)


def _na_bias_tables(rpb):
    cols = jnp.arange(GRID_W)
    c0 = jnp.clip(cols - WIN_C // 2, 0, GRID_W - WIN_C)
    col_ok = (cols[None, :] >= c0[:, None]) & (cols[None, :] < c0[:, None] + WIN_C)
    col_off = jnp.clip(cols[None, :] - cols[:, None] + (WIN_C - 1), 0, 2 * WIN_C - 2)
    hot_c = (col_off[:, :, None] == jnp.arange(2 * WIN_C - 1)).astype(F32)
    core = jnp.einsum("hab,ckb->hcak", rpb, hot_c, precision=HIGHEST)
    core = jnp.where(col_ok[None, :, None, :], core, NEG_INF)
    pad_hi = NA_BIAS_ROWS - NA_BIAS_PAD - (2 * WIN_R - 1)
    tab = jnp.pad(core, ((0, 0), (0, 0), (NA_BIAS_PAD, pad_hi), (0, 0)), constant_values=NEG_INF)
    tab = tab.reshape(NA_HEADS, GRID_W, NA_BIAS_ROWS * GRID_W)
    shifted = jnp.pad(tab[:, :, GRID_W:], ((0, 0), (0, 0), (0, GRID_W)), constant_values=NEG_INF)
    return jnp.stack([tab, shifted]).astype(F32)


def _na_block_bias(bias_ref, half, kind):
    nk = NA_KROWS * GRID_W
    lane = lax.broadcasted_iota(jnp.int32, (GRID_W, nk), 1)
    slabs = []
    for i in range(NA_QROWS):
        if kind == 0:
            ext0, lo, hi = WIN_R - 1 + NA_BIAS_PAD - i, 0, WIN_R
        elif kind == 1:
            ext0, lo, hi = WIN_R - 1 - WIN_R // 2 + NA_BIAS_PAD - i, i, i + WIN_R
        else:
            ext0, lo, hi = NA_BIAS_PAD - i, NA_KROWS - WIN_R, NA_KROWS
        start = (ext0 - ext0 % 2) * GRID_W
        slab = bias_ref[ext0 % 2, half, :, start:start + nk]
        slabs.append(jnp.where((lane >= lo * GRID_W) & (lane < hi * GRID_W), slab, NEG_INF))
    return jnp.concatenate(slabs, axis=0)


def _na_latent_kernel(q_ref, k_ref, v_ref, kc_ref, vc_ref, bias_ref, o_ref):
    rows_n = DEC_SEQ // GRID_W
    kc = kc_ref[...].astype(BF16)
    vc = vc_ref[...].astype(BF16)
    tq = NA_QROWS * GRID_W
    nk = NA_KROWS * GRID_W
    lane = lax.broadcasted_iota(jnp.int32, (tq, LANES), 1)
    biases = {}
    for blk in range(rows_n // NA_QROWS):
        q_row0 = blk * NA_QROWS
        if blk == 0:
            kind, k_row0 = 0, 0
        elif blk == rows_n // NA_QROWS - 1:
            kind, k_row0 = 2, rows_n - NA_KROWS
        else:
            kind, k_row0 = 1, q_row0 - WIN_R // 2
        q = q_ref[q_row0 * GRID_W:q_row0 * GRID_W + tq, :].astype(BF16)
        k = k_ref[k_row0 * GRID_W:k_row0 * GRID_W + nk, :].astype(BF16)
        v = v_ref[k_row0 * GRID_W:k_row0 * GRID_W + nk, :].astype(BF16)
        zero = jnp.zeros_like(q)
        scores = []
        for half in range(2):
            qh = jnp.where((lane < 64) if half == 0 else (lane >= 64), q, zero)
            if (kind, half) not in biases:
                biases[kind, half] = _na_block_bias(bias_ref, half, kind)
            scores.append([_nt_dot(qh, k) + biases[kind, half], _nt_dot(qh, kc)])
        o_ref[q_row0 * GRID_W:q_row0 * GRID_W + tq, :] = _pair_softmax_pv(scores[0], scores[1], [v, vc])


def _na_latent_attention(qkv, cache_k, cache_v, j, bias):
    nb = D_MODEL // LANES
    return pl.pallas_call(
        _na_latent_kernel,
        grid=(nb, DEC_BATCH),
        in_specs=[
            pl.BlockSpec((DEC_SEQ, LANES), lambda h, b: (b, h)),
            pl.BlockSpec((DEC_SEQ, LANES), lambda h, b: (b, nb + h)),
            pl.BlockSpec((DEC_SEQ, LANES), lambda h, b: (b, 2 * nb + h)),
            pl.BlockSpec((None, None, PAST_LEN, LANES), lambda h, b: (b, j, 0, h)),
            pl.BlockSpec((None, None, PAST_LEN, LANES), lambda h, b: (b, j, 0, h)),
            pl.BlockSpec((2, 2, GRID_W, NA_BIAS_ROWS * GRID_W), lambda h, b: (0, h, 0, 0)),
        ],
        out_specs=pl.BlockSpec((DEC_SEQ, LANES), lambda h, b: (b, h)),
        out_shape=jax.ShapeDtypeStruct((N_SAMPLE, D_MODEL), F32),
        compiler_params=_cparams(2),
        name="na_latent_attention",
    )(qkv, qkv, qkv, cache_k, cache_v, bias)


def _out_ln_kernel(op_ref, os_ref, w_ref, x_ref, gate_ref, g_ref, b_ref, xo_ref):
    def finish(o_ref):
        y = jnp.dot(o_ref[...].astype(BF16), w_ref[...], preferred_element_type=F32)
        z = DEEPNORM_ALPHA * x_ref[...] + gate_ref[...] * y
        xo_ref[...] = _layernorm(z, g_ref[...], b_ref[...])

    @pl.when(pl.program_id(0) < PROMPT_TILES)
    def _():
        finish(op_ref)

    @pl.when(pl.program_id(0) >= PROMPT_TILES)
    def _():
        finish(os_ref)


def _out_proj_ln(o_p, o_s, w_bf16, x, mods, layer, ln_g, ln_b):
    k_dim = w_bf16.shape[0]
    row = pl.BlockSpec((1, D_MODEL), lambda i: (0, 0))
    return pl.pallas_call(
        _out_ln_kernel,
        grid=(N_TOK // TM,),
        in_specs=[
            pl.BlockSpec((TM, k_dim), lambda i: (jnp.minimum(i, PROMPT_TILES - 1), 0)),
            pl.BlockSpec((TM, k_dim), lambda i: (jnp.maximum(i - PROMPT_TILES, 0), 0)),
            pl.BlockSpec((k_dim, D_MODEL), lambda i: (0, 0)),
            pl.BlockSpec((TM, D_MODEL), lambda i: (i, 0)),
            _mod_spec(layer, 2, 0, 0),
            row, row,
        ],
        out_specs=pl.BlockSpec((TM, D_MODEL), lambda i: (i, 0)),
        out_shape=jax.ShapeDtypeStruct((N_TOK, D_MODEL), F32),
        compiler_params=_cparams(1),
        name="out_proj_layernorm",
    )(o_p, o_s, w_bf16, x, mods, ln_g, ln_b)


def _route_kernel(x_ref, shift_ref, scale_ref, rw_ref, rb_ref, xs_ref, meta_ref, cnt_ref):
    @pl.when(pl.program_id(0) < N_CHUNKS)
    def _():
        _route_chunk(x_ref, shift_ref, scale_ref, rw_ref, rb_ref, xs_ref, meta_ref, cnt_ref)

    @pl.when(pl.program_id(0) == N_CHUNKS)
    def _():
        xs_ref[...] = jnp.zeros_like(xs_ref)


def _route_chunk(x_ref, shift_ref, scale_ref, rw_ref, rb_ref, xs_ref, meta_ref, cnt_ref):
    h = x_ref[...] * (1.0 + scale_ref[...]) + shift_ref[...]
    h_hi = h.astype(BF16)
    h_lo = (h - h_hi.astype(F32)).astype(BF16)
    logits = (jnp.dot(h_hi, rw_ref[0], preferred_element_type=F32)
              + jnp.dot(h_lo, rw_ref[0], preferred_element_type=F32)
              + jnp.dot(h_hi, rw_ref[1], preferred_element_type=F32)) + rb_ref[...]
    lane = lax.broadcasted_iota(jnp.int32, logits.shape, 1)
    work = logits
    top_v, hots = [], []
    for _ in range(TOP_K):
        m = work.max(axis=-1, keepdims=True)
        first = jnp.where(work == m, lane, N_EXPERTS).min(axis=-1, keepdims=True)
        hot = lane == first
        top_v.append(m)
        hots.append(jnp.where(hot, 1.0, 0.0))
        work = jnp.where(hot, -jnp.inf, work)
    es = [jnp.exp(v - top_v[0]) for v in top_v]
    den = es[0] + es[1] + es[2] + es[3]
    gates = [e / den for e in es]

    cnts = [hf.sum(axis=0, keepdims=True) for hf in hots]
    n_e = cnts[0] + cnts[1] + cnts[2] + cnts[3]
    pieces = jnp.floor((n_e + (MOE_PIECE - 1)) * (1.0 / MOE_PIECE))
    er = lax.broadcasted_iota(jnp.int32, (N_EXPERTS, N_EXPERTS), 0)
    ec = lax.broadcasted_iota(jnp.int32, (N_EXPERTS, N_EXPERTS), 1)
    upper = jnp.where(er < ec, 1.0, 0.0).astype(BF16)
    poff = jnp.dot(jnp.broadcast_to(pieces, (8, N_EXPERTS)).astype(BF16), upper,
                   preferred_element_type=F32)[0:1]

    tr = lax.broadcasted_iota(jnp.int32, (TM, TM), 0)
    tc = lax.broadcasted_iota(jnp.int32, (TM, TM), 1)
    lower = jnp.where(tc < tr, 1.0, 0.0).astype(BF16)
    run = poff * MOE_PIECE
    slots = []
    for k in range(TOP_K):
        prefix = jnp.dot(lower, hots[k].astype(BF16), preferred_element_type=F32)
        slots.append(jnp.sum(hots[k] * (run + prefix), axis=-1, keepdims=True))
        run = run + cnts[k]

    lane_w = lax.broadcasted_iota(jnp.int32, (TM, LANES), 1)
    meta = jnp.full((TM, LANES), -1.0, F32)
    for k in range(TOP_K):
        meta = jnp.where(lane_w == k, slots[k], meta)
        meta = jnp.where(lane_w == TOP_K + k, gates[k], meta)
    meta_ref[...] = meta[:, :2 * TOP_K]
    cnt_ref[...] = pieces

    slot_rows = meta.T
    srow = lax.broadcasted_iota(jnp.int32, (MOE_CHUNK_ROWS, TM), 0).astype(F32)
    p = jnp.zeros((MOE_CHUNK_ROWS, TM), F32)
    for k in range(TOP_K):
        p = p + jnp.where(srow == slot_rows[k:k + 1, :], 1.0, 0.0)
    xs_ref[...] = jnp.dot(p.astype(BF16), h_hi, preferred_element_type=F32)


def _route(x, mods, layer, rw, rb):
    assert 2 * EXP_TM <= MOE_CHUNK_ROWS
    last = N_CHUNKS - 1
    rw_hi = rw.astype(BF16)
    rw_split = jnp.stack([rw_hi, (rw - rw_hi.astype(F32)).astype(BF16)], axis=1)
    return pl.pallas_call(
        _route_kernel,
        grid=(N_CHUNKS + 1,),
        in_specs=[
            pl.BlockSpec((TM, D_MODEL), lambda i: (jnp.minimum(i, last), 0)),
            _mod_spec(layer, 3, 0, 0),
            _mod_spec(layer, 4, 0, 0),
            pl.BlockSpec((None, 2, D_MODEL, N_EXPERTS), lambda i: (layer, 0, 0, 0)),
            pl.BlockSpec((None, 1, N_EXPERTS), lambda i: (layer, 0, 0)),
        ],
        out_specs=[
            pl.BlockSpec((MOE_CHUNK_ROWS, D_MODEL), lambda i: (i, 0)),
            pl.BlockSpec((TM, 2 * TOP_K), lambda i: (jnp.minimum(i, last), 0)),
            pl.BlockSpec((None, 1, N_EXPERTS), lambda i: (jnp.minimum(i, last), 0, 0)),
        ],
        out_shape=[
            jax.ShapeDtypeStruct(((N_CHUNKS + 1) * MOE_CHUNK_ROWS, D_MODEL), F32),
            jax.ShapeDtypeStruct((N_TOK, 2 * TOP_K), F32),
            jax.ShapeDtypeStruct((N_CHUNKS, 1, N_EXPERTS), F32),
        ],
        compiler_params=_cparams(1),
        name="moe_route",
    )(x, mods, mods, rw_split, rb.reshape(DEPTH, 1, N_EXPERTS))


def _moe_schedule(pieces):
    e_ids = jnp.arange(N_EXPERTS, dtype=jnp.int32)
    npe = pieces.sum(axis=0)
    tiles_e = (npe + EXP_PPT - 1) // EXP_PPT
    tile_end = jnp.cumsum(tiles_e)
    tile_start = tile_end - tiles_e
    g = jnp.arange(MAX_TILES, dtype=jnp.int32)
    last_e = jnp.max(jnp.where(tiles_e > 0, e_ids, 0))
    te = jnp.minimum(jnp.sum(tile_end[None, :] <= g[:, None], axis=1), last_e).astype(jnp.int32)
    seg_len = pieces.T
    seg_base = tile_start[:, None] * EXP_PPT + jnp.cumsum(seg_len, axis=1) - seg_len
    poff = jnp.cumsum(pieces, axis=1) - pieces
    seg_pos = (jnp.arange(N_CHUNKS, dtype=jnp.int32)[:, None] * MOE_CHUNK_PIECES + poff).T
    hot = (te[:, None] == e_ids[None, :])[:, :, None]
    base, length, first = (jnp.sum(jnp.where(hot, tab[None], 0), axis=1)
                           for tab in (seg_base, seg_len, seg_pos))
    lane = jnp.arange(EXP_PPT, dtype=jnp.int32)[None, :]
    pos = (g[:, None] * EXP_PPT + lane)[:, :, None]
    hit = (base[:, None, :] <= pos) & (pos < (base + length)[:, None, :])
    src = jnp.sum(jnp.where(hit, first[:, None, :] + pos - base[:, None, :], 0), axis=-1)
    valid = jnp.any(hit, axis=-1)
    scratch = N_CHUNKS * MOE_CHUNK_PIECES + (g[:, None] % 2) * EXP_PPT + lane
    dst = jnp.where(valid, src, scratch).reshape(-1).astype(jnp.int32)
    src = jnp.where(valid, src, src[:, :1]).reshape(-1).astype(jnp.int32)
    return te, tile_end[-1:].astype(jnp.int32), src, dst


def _expert_kernel(te_ref, nt_ref, src_ref, dst_ref, xs_hbm, w1_ref, b1_ref, w2_ref, b2_ref, out_hbm,
                   xbuf, ybuf, w1s, w2s, sem_in, sem_out):
    g = pl.program_id(0)
    nt = nt_ref[0]
    slot = g % 2

    def in_copy(tile, i, s):
        row = pl.multiple_of(src_ref[tile * EXP_PPT + i] * MOE_PIECE, MOE_PIECE)
        return pltpu.make_async_copy(xs_hbm.at[pl.ds(row, MOE_PIECE), :],
                                     xbuf.at[s, pl.ds(i * MOE_PIECE, MOE_PIECE), :], sem_in.at[s])

    def out_copy(tile, i, s):
        row = pl.multiple_of(dst_ref[tile * EXP_PPT + i] * MOE_PIECE, MOE_PIECE)
        return pltpu.make_async_copy(ybuf.at[s, pl.ds(i * MOE_PIECE, MOE_PIECE), :],
                                     out_hbm.at[pl.ds(row, MOE_PIECE), :], sem_out.at[s])

    def gather(tile, s, wait):
        for i in range(EXP_PPT):
            cp = in_copy(tile, i, s)
            cp.wait() if wait else cp.start()

    def scatter(tile, s, wait):
        for i in range(EXP_PPT):
            cp = out_copy(tile, i, s)
            cp.wait() if wait else cp.start()

    @pl.when(g < nt)
    def _():
        @pl.when(g == 0)
        def _():
            gather(0, 0, False)

        @pl.when(g + 1 < nt)
        def _():
            gather(g + 1, 1 - slot, False)

        gather(g, slot, True)

        @pl.when(g >= 2)
        def _():
            scatter(g - 2, slot, True)

        @pl.when((g == 0) | (te_ref[g] != te_ref[jnp.maximum(g - 1, 0)]))
        def _():
            w1s[...] = w1_ref[...].astype(BF16)
            w2s[...] = w2_ref[...].astype(BF16)

        gu = jnp.dot(xbuf[slot].astype(BF16), w1s[...], preferred_element_type=F32) + b1_ref[...]
        gl = jnp.minimum(gu[:, :D_EXPERT], SWIGLU_LIMIT)
        ul = jnp.clip(gu[:, D_EXPERT:], -SWIGLU_LIMIT, SWIGLU_LIMIT)
        act = (ul + 1.0) * (gl * _sigmoid(SWIGLU_ALPHA * gl))
        ybuf[slot] = jnp.dot(act.astype(BF16), w2s[...], preferred_element_type=F32) + b2_ref[...]
        scatter(g, slot, False)

        @pl.when(g == nt - 1)
        def _():
            @pl.when(g >= 1)
            def _():
                scatter(g - 1, 1 - slot, True)
            scatter(g, slot, True)


def _experts(xs, te, nt, src, dst, layer, w1, b1, w2, b2):
    def expert_block(g, te, nt, src, dst):
        return (layer, te[g], 0, 0)

    grid_spec = pltpu.PrefetchScalarGridSpec(
        num_scalar_prefetch=4,
        grid=(MAX_TILES,),
        in_specs=[
            pl.BlockSpec(memory_space=pl.ANY),
            pl.BlockSpec((None, None, D_MODEL, 2 * D_EXPERT), expert_block),
            pl.BlockSpec((None, None, 1, 2 * D_EXPERT), expert_block),
            pl.BlockSpec((None, None, D_EXPERT, D_MODEL), expert_block),
            pl.BlockSpec((None, None, 1, D_MODEL), expert_block),
        ],
        out_specs=pl.BlockSpec(memory_space=pl.ANY),
        scratch_shapes=[
            pltpu.VMEM((2, EXP_TM, D_MODEL), F32),
            pltpu.VMEM((2, EXP_TM, D_MODEL), F32),
            pltpu.VMEM((D_MODEL, 2 * D_EXPERT), BF16),
            pltpu.VMEM((D_EXPERT, D_MODEL), BF16),
            pltpu.SemaphoreType.DMA((2,)),
            pltpu.SemaphoreType.DMA((2,)),
        ],
    )
    return pl.pallas_call(
        _expert_kernel,
        grid_spec=grid_spec,
        out_shape=jax.ShapeDtypeStruct(xs.shape, F32),
        input_output_aliases={4: 0},
        compiler_params=_cparams(1),
        name="moe_experts",
    )(te, nt, src, dst, xs, w1, b1.reshape(DEPTH, N_EXPERTS, 1, 2 * D_EXPERT),
      w2, b2.reshape(DEPTH, N_EXPERTS, 1, D_MODEL))


def _combine_kernel(ys_ref, meta_ref, x_ref, gate_ref, g_ref, b_ref, xo_ref):
    meta = meta_ref[...]
    col = lax.broadcasted_iota(jnp.int32, (TM, MOE_CHUNK_ROWS), 1).astype(F32)
    gmat = jnp.zeros((TM, MOE_CHUNK_ROWS), F32)
    for k in range(TOP_K):
        gmat = gmat + jnp.where(col == meta[:, k:k + 1], meta[:, TOP_K + k:TOP_K + k + 1], 0.0)
    y = jnp.dot(gmat.astype(BF16), ys_ref[...].astype(BF16), preferred_element_type=F32)
    z = DEEPNORM_ALPHA * x_ref[...] + gate_ref[...] * y
    xo_ref[...] = _layernorm(z, g_ref[...], b_ref[...])


def _combine_ln(ys, meta, x, mods, layer, ln_g, ln_b):
    row = pl.BlockSpec((1, D_MODEL), lambda i: (0, 0))
    return pl.pallas_call(
        _combine_kernel,
        grid=(N_CHUNKS,),
        in_specs=[
            pl.BlockSpec((MOE_CHUNK_ROWS, D_MODEL), lambda i: (i, 0)),
            pl.BlockSpec((TM, 2 * TOP_K), lambda i: (i, 0)),
            pl.BlockSpec((TM, D_MODEL), lambda i: (i, 0)),
            _mod_spec(layer, 5, 0, 0),
            row, row,
        ],
        out_specs=pl.BlockSpec((TM, D_MODEL), lambda i: (i, 0)),
        out_shape=jax.ShapeDtypeStruct((N_TOK, D_MODEL), F32),
        compiler_params=_cparams(1),
        name="moe_combine_layernorm",
    )(ys, meta, x, mods, ln_g, ln_b)


def _moe(x, mods, layer, rw, rb, w1, b1, w2, b2, ln_g, ln_b):
    xs, meta, pieces = _route(x, mods, layer, rw, rb)
    te, nt, src, dst = _moe_schedule(pieces.reshape(N_CHUNKS, N_EXPERTS).astype(jnp.int32))
    ys = _experts(xs, te, nt, src, dst, layer, w1, b1, w2, b2)
    return _combine_ln(ys, meta, x, mods, layer, ln_g, ln_b)


def _mla_norm_kernel(p_ref, qn_ref, kvn_ref, cq_ref, ckvkr_ref):
    p = p_ref[...]
    cq = p[:, :MLA_Q_RANK]
    cq_ref[...] = cq * lax.rsqrt(jnp.mean(cq * cq, axis=-1, keepdims=True) + RMS_EPS) * qn_ref[...]
    ckv = p[:, MLA_Q_RANK:MLA_Q_RANK + MLA_KV_RANK]
    ckvkr_ref[:, :MLA_KV_RANK] = (ckv * lax.rsqrt(jnp.mean(ckv * ckv, axis=-1, keepdims=True) + RMS_EPS)
                                  * kvn_ref[...])
    ckvkr_ref[:, MLA_KV_RANK:] = p[:, MLA_Q_RANK + MLA_KV_RANK:]


def _mla_norm(proj, q_norm, kv_norm):
    return pl.pallas_call(
        _mla_norm_kernel,
        grid=(N_TOK // TM,),
        in_specs=[
            pl.BlockSpec((TM, 512), lambda i: (i, 0)),
            pl.BlockSpec((1, MLA_Q_RANK), lambda i: (0, 0)),
            pl.BlockSpec((1, MLA_KV_RANK), lambda i: (0, 0)),
        ],
        out_specs=[pl.BlockSpec((TM, MLA_Q_RANK), lambda i: (i, 0)),
                   pl.BlockSpec((TM, 2 * MLA_KV_RANK), lambda i: (i, 0))],
        out_shape=[jax.ShapeDtypeStruct((N_TOK, MLA_Q_RANK), F32),
                   jax.ShapeDtypeStruct((N_TOK, 2 * MLA_KV_RANK), F32)],
        compiler_params=_cparams(1),
        name="mla_norm",
    )(proj, q_norm, kv_norm)


def _mla_weights(w_in, w_uq, w_ukv):
    pad = 512 - w_in.shape[1]
    w_in_p = jnp.pad(w_in, ((0, 0), (0, pad)))
    wq = w_uq.reshape(MLA_Q_RANK, MLA_HEADS, MLA_NOPE + MLA_ROPE)
    wq = jnp.pad(wq, ((0, 0), (0, 0), (0, LANES - MLA_NOPE - MLA_ROPE))).reshape(MLA_Q_RANK, MLA_HEADS * LANES)
    wkv = w_ukv.reshape(MLA_KV_RANK, MLA_HEADS, MLA_NOPE + MLA_V)
    wkn = jnp.pad(wkv[:, :, :MLA_NOPE], ((0, 0), (0, 0), (0, LANES - MLA_NOPE)))
    eye = jnp.pad(jnp.eye(MLA_ROPE, dtype=F32), ((0, 0), (MLA_NOPE, LANES - MLA_NOPE - MLA_ROPE)))
    wkr = jnp.broadcast_to(eye[:, None, :], (MLA_ROPE, MLA_HEADS, LANES))
    zeros = jnp.zeros((2 * MLA_KV_RANK - MLA_KV_RANK - MLA_ROPE, MLA_HEADS, LANES), F32)
    wk = jnp.concatenate([wkn, wkr, zeros], axis=0).reshape(2 * MLA_KV_RANK, MLA_HEADS * LANES)
    wv = jnp.pad(wkv[:, :, MLA_NOPE:].reshape(MLA_KV_RANK, MLA_HEADS * MLA_V), ((0, MLA_KV_RANK), (0, 0)))
    return w_in_p.astype(BF16), wq.astype(BF16), wk.astype(BF16), wv.astype(BF16)


def kernel(x_prompt, x_sample, cache_na_k, cache_na_v, cache_diff_k, cache_diff_v, cache_mla_ckv, cache_mla_kr,
           c, c_ctx, ada_w, ada_b, ln_g, ln_b, na_w_qkv, na_rpb, na_w_o, diff_w_qkv, diff_lambda, diff_subln,
           diff_w_o, mla_w_in, mla_q_norm, mla_w_uq, mla_kv_norm, mla_w_ukv, mla_w_o, router_w, router_b,
           moe_w1, moe_b1, moe_w2, moe_b2):
    x = jnp.concatenate([x_prompt.reshape(N_PROMPT, D_MODEL), x_sample.reshape(N_SAMPLE, D_MODEL)], axis=0)
    cond = jnp.concatenate([c_ctx[None, :], c, jnp.zeros((N_MOD_ROWS - 1 - DEC_BATCH, D_MODEL), F32)], axis=0)
    mods = _modulation(cond, ada_w, ada_b)

    n_na = na_w_qkv.shape[0]
    n_diff = diff_w_qkv.shape[0]
    n_mla = mla_w_in.shape[0]
    cache_na_k = cache_na_k.reshape(DEC_BATCH, n_na, PAST_LEN, D_MODEL)
    cache_na_v = cache_na_v.reshape(DEC_BATCH, n_na, PAST_LEN, D_MODEL)
    cache_diff_k = cache_diff_k.reshape(DEC_BATCH, n_diff, PAST_LEN, D_MODEL)
    cache_diff_v = cache_diff_v.reshape(DEC_BATCH, n_diff, PAST_LEN, D_MODEL)
    all_tiles = N_TOK // TM
    nb = D_MODEL // LANES
    dec_off = N_PROMPT // DEC_SEQ

    st_na_k, st_na_v, st_diff_k, st_diff_v, st_mla_ckv, st_mla_kr = [], [], [], [], [], []
    for i in range(DEPTH):
        kind, j = i % N_MIXERS, i // N_MIXERS
        latent = dict(tile_off=PROMPT_TILES, mods=mods, layer=i)
        if kind == 0:
            w = na_w_qkv[j].astype(BF16)
            qk_scale = dict(q_cols=D_MODEL, q_scale=NA_HEAD_DIM ** -0.5, out_dtype=BF16)
            q_p, k_p, v_p = _context_qkv(x, w, mods, i, qk_scale, "na")
            st_na_k.append(k_p)
            st_na_v.append(v_p)
            qkv_s = _project(x, w, n_tiles=SAMPLE_TILES, name="na_qkv_latent", **latent, **qk_scale)
            o_p = _attention(q_p, k_p, v_p, mode="pair", n_batch=BATCH, seq=SEQ, tq=SEQ, q_col=0, k_col=0,
                             v_col=0, hb=nb, name="na_context_attention")
            o_s = _na_latent_attention(qkv_s, cache_na_k, cache_na_v, j, _na_bias_tables(na_rpb[j]))
            w_o = na_w_o[j]
        elif kind == 1:
            lam_init = 0.8 - 0.6 * math.exp(-0.3 * i)
            w = diff_w_qkv[j].astype(BF16)
            cos, sin = _axial_tables(LANES, DIFF_HEAD_DIM // 4, 0, LANES)
            qk_scale = dict(q_cols=D_MODEL, q_scale=DIFF_HEAD_DIM ** -0.5, out_dtype=BF16)
            q_p, k_p, v_p = _context_qkv(x, w, mods, i, qk_scale, "diff")
            st_diff_k.append(k_p)
            st_diff_v.append(v_p)
            qk_s = _project(x, w[:, :2 * D_MODEL], n_tiles=SAMPLE_TILES, rope=(DIFF_HEAD_DIM // 4, cos, sin),
                            name="diff_qk_latent", **latent, **qk_scale)
            v_s = _project(x, w[:, 2 * D_MODEL:], n_tiles=SAMPLE_TILES, out_dtype=BF16, name="diff_v_latent",
                           **latent)
            common = dict(mode="diff", lam=diff_lambda[j], subln=diff_subln[j][None, :], lam_init=lam_init)
            o_p = _attention(q_p, k_p, v_p, n_batch=BATCH, seq=SEQ, tq=SEQ, q_col=0, k_col=0, v_col=0, hb=nb,
                             name="diff_context_attention", **common)
            o_s = _attention(qk_s, qk_s, v_s, n_batch=DEC_BATCH, seq=DEC_SEQ, tq=TM, q_col=0, k_col=nb, v_col=0,
                             ctx=(cache_diff_k, cache_diff_v, j), hb=2, name="diff_latent_attention", **common)
            w_o = diff_w_o[j]
        else:
            w_in, wq, wk, wv = _mla_weights(mla_w_in[j], mla_w_uq[j], mla_w_ukv[j])
            cos, sin = _axial_tables(LANES, MLA_ROPE // 4, MLA_NOPE, MLA_NOPE + MLA_ROPE)
            rope = (MLA_ROPE // 4, cos, sin)
            proj = _project(x, w_in, n_tiles=all_tiles, mods=mods, layer=i, name="mla_down")
            cq, ckvkr = _mla_norm(proj, mla_q_norm[j][None, :], mla_kv_norm[j][None, :])
            st_mla_ckv.append(ckvkr[:N_PROMPT, :MLA_KV_RANK])
            st_mla_kr.append(ckvkr[:N_PROMPT, MLA_KV_RANK:MLA_KV_RANK + MLA_ROPE])
            q_scale = dict(q_cols=MLA_HEADS * LANES, q_scale=(MLA_NOPE + MLA_ROPE) ** -0.5, out_dtype=BF16)
            q_p = _project(cq, wq, n_tiles=PROMPT_TILES, name="mla_q_context", **q_scale)
            k_p = _project(ckvkr, wk, n_tiles=PROMPT_TILES, out_dtype=BF16, name="mla_k_context")
            q_s = _project(cq, wq, tile_off=PROMPT_TILES, n_tiles=SAMPLE_TILES, rope=rope, name="mla_q_latent",
                           **q_scale)
            k_s = _project(ckvkr, wk, tile_off=PROMPT_TILES, n_tiles=SAMPLE_TILES, rope=rope, out_dtype=BF16,
                           name="mla_k_latent")
            v_all = _project(ckvkr, wv, n_tiles=all_tiles, out_dtype=BF16, name="mla_v")
            ctx_in = jnp.concatenate(
                [cache_mla_ckv[:, j], cache_mla_kr[:, j],
                 jnp.zeros((DEC_BATCH, PAST_LEN, 2 * MLA_KV_RANK - MLA_KV_RANK - MLA_ROPE), F32)],
                axis=-1).reshape(DEC_BATCH * PAST_LEN, 2 * MLA_KV_RANK)
            n_ctx_tiles = DEC_BATCH * PAST_LEN // TM
            k_c = _project(ctx_in, wk, n_tiles=n_ctx_tiles, out_dtype=BF16, name="mla_k_cache")
            v_c = _project(ctx_in, wv, n_tiles=n_ctx_tiles, out_dtype=BF16, name="mla_v_cache")
            k_c = k_c.reshape(DEC_BATCH, 1, PAST_LEN, MLA_HEADS * LANES)
            v_c = v_c.reshape(DEC_BATCH, 1, PAST_LEN, D_MODEL)
            o_p = _attention(q_p, k_p, v_all, mode="mla", n_batch=BATCH, seq=SEQ, tq=SEQ, q_col=0, k_col=0,
                             v_col=0, hb=nb, name="mla_context_attention")
            o_s = _attention(q_s, k_s, v_all, mode="mla", n_batch=DEC_BATCH, seq=DEC_SEQ, tq=TM, q_col=0, k_col=0,
                             v_col=0, v_row_off=dec_off, ctx=(k_c, v_c, 0), hb=2, name="mla_latent_attention")
            w_o = mla_w_o[j]

        x = _out_proj_ln(o_p, o_s, w_o.astype(BF16), x, mods, i, ln_g[i, 0][None, :], ln_b[i, 0][None, :])
        x = _moe(x, mods, i, router_w, router_b, moe_w1, moe_b1, moe_w2, moe_b2,
                 ln_g[i, 1][None, :], ln_b[i, 1][None, :])

    y_prompt = x[:N_PROMPT].reshape(BATCH, SEQ, D_MODEL)
    y_sample = x[N_PROMPT:].reshape(DEC_BATCH, DEC_SEQ, D_MODEL)

    def stack(parts, *tail):
        return jnp.stack([p.reshape(BATCH, SEQ, *tail) for p in parts], axis=1)

    return (y_prompt, y_sample,
            stack(st_na_k, NA_HEADS, NA_HEAD_DIM), stack(st_na_v, NA_HEADS, NA_HEAD_DIM),
            stack(st_diff_k, DIFF_HEADS, 2 * DIFF_HEAD_DIM), stack(st_diff_v, DIFF_HEADS, 2 * DIFF_HEAD_DIM),
            stack(st_mla_ckv, MLA_KV_RANK), stack(st_mla_kr, MLA_ROPE))
```

```python
import functools
import math

import jax
import jax.numpy as jnp
from jax import lax
from jax.experimental import pallas as pl
from jax.experimental.pallas import tpu as pltpu

F32 = jnp.float32
BF16 = jnp.bfloat16
HIGHEST = lax.Precision.HIGHEST

D_MODEL = 1024
BATCH = 32
SEQ = 256
DEPTH = 4
DEC_BATCH = 4
DEC_SEQ = 2048
PAST_LEN = 512
GRID_W = 64
N_MIXERS = 3
NA_HEADS = 16
NA_HEAD_DIM = 64
WIN_R = 8
WIN_C = 16
DIFF_HEADS = 8
DIFF_HEAD_DIM = 64
MLA_HEADS = 16
MLA_Q_RANK = 256
MLA_KV_RANK = 128
MLA_NOPE = 64
MLA_ROPE = 32
MLA_V = 64
N_EXPERTS = 32
TOP_K = 4
D_EXPERT = 1024
SWIGLU_LIMIT = 7.0
SWIGLU_ALPHA = 1.702
ROPE_BASE = 10000.0
LN_EPS = 1e-5
RMS_EPS = 1e-6
NEG_INF = -1e30
DEEPNORM_ALPHA = (2 * DEPTH) ** 0.25

LANES = 128
TM = 256
N_PROMPT = BATCH * SEQ
N_SAMPLE = DEC_BATCH * DEC_SEQ
N_TOK = N_PROMPT + N_SAMPLE
PROMPT_TILES = N_PROMPT // TM
SAMPLE_TILES = N_SAMPLE // TM
TILES_PER_DEC = DEC_SEQ // TM
N_MOD_ROWS = 8
LATENT_TQ = 512
NA_QROWS = 4
NA_KROWS = NA_QROWS + WIN_R - 1
NA_BIAS_PAD = NA_QROWS - 1
NA_BIAS_ROWS = WIN_R - 1 + NA_BIAS_PAD + NA_KROWS
N_CHUNKS = N_TOK // TM
MOE_PIECE = 8
MAX_CHUNK_PIECES = (TM * TOP_K + N_EXPERTS * (MOE_PIECE - 1)) // MOE_PIECE
MOE_CHUNK_PIECES = 160
MOE_CHUNK_ROWS = MOE_CHUNK_PIECES * MOE_PIECE
EXP_TM = 512
EXP_PPT = EXP_TM // MOE_PIECE
MAX_TILES = N_CHUNKS * MAX_CHUNK_PIECES // EXP_PPT + N_EXPERTS
VMEM_LIMIT = 56 * 1024 * 1024


def _cparams(n_axes, vmem=VMEM_LIMIT):
    return pltpu.CompilerParams(dimension_semantics=("arbitrary",) * n_axes, vmem_limit_bytes=vmem)


def _mod_row(t):
    return jnp.where(t < PROMPT_TILES, 0, 1 + (t - PROMPT_TILES) // TILES_PER_DEC)


def _mod_spec(layer, chunk, tile_off, tile_axis):
    def index(*g):
        row = _mod_row(g[tile_axis] + tile_off)
        return ((layer * N_MOD_ROWS + row) * 6 + chunk, 0, 0)
    return pl.BlockSpec((None, 1, D_MODEL), index)


def _sigmoid(x):
    return 1.0 / (1.0 + jnp.exp(-x))


def _layernorm(z, g, b):
    mu = jnp.mean(z, axis=-1, keepdims=True)
    zc = z - mu
    var = jnp.mean(zc * zc, axis=-1, keepdims=True)
    return zc * lax.rsqrt(var + LN_EPS) * g + b


def _mod_kernel(cond_ref, w_ref, b_ref, o_ref):
    cnd = cond_ref[...]
    s = cnd * _sigmoid(cnd)
    o_ref[...] = jnp.dot(s, w_ref[...], precision=HIGHEST, preferred_element_type=F32) + b_ref[...]


def _modulation(cond, ada_w, ada_b):
    out = pl.pallas_call(
        _mod_kernel,
        grid=(DEPTH, 6),
        in_specs=[
            pl.BlockSpec((N_MOD_ROWS, D_MODEL), lambda l, j: (0, 0)),
            pl.BlockSpec((None, D_MODEL, D_MODEL), lambda l, j: (l, 0, j)),
            pl.BlockSpec((None, 1, D_MODEL), lambda l, j: (l, 0, j)),
        ],
        out_specs=pl.BlockSpec((None, N_MOD_ROWS, D_MODEL), lambda l, j: (l, 0, j)),
        out_shape=jax.ShapeDtypeStruct((DEPTH, N_MOD_ROWS, 6 * D_MODEL), F32),
        compiler_params=_cparams(2),
        name="adaln_modulation",
    )(cond, ada_w, ada_b.reshape(DEPTH, 1, 6 * D_MODEL))
    return out.reshape(DEPTH * N_MOD_ROWS * 6, 1, D_MODEL)


def _rope_swap(y, g):
    lane = lax.broadcasted_iota(jnp.int32, y.shape, 1)
    first = (lane % (2 * g)) < g
    return jnp.where(first, pltpu.roll(y, LANES - g, 1), pltpu.roll(y, g, 1))


def _proj_kernel(*refs, use_mod, rope_g, q_tiles, q_scale):
    it = iter(refs)
    x_ref = next(it)
    x = x_ref[...]
    if use_mod:
        shift_ref, scale_ref = next(it), next(it)
        x = x * (1.0 + scale_ref[...]) + shift_ref[...]
    w_ref = next(it)
    y = jnp.dot(x.astype(BF16), w_ref[...], preferred_element_type=F32)
    col_scale = jnp.where(pl.program_id(0) < q_tiles, q_scale, 1.0) if q_tiles else None
    if rope_g:
        cos_ref, sin_ref = next(it), next(it)
        o_ref = next(it)
        cos, sin = cos_ref[...], sin_ref[...]
        for j in range(y.shape[1] // LANES):
            yb = y[:, j * LANES:(j + 1) * LANES]
            yb = yb * cos + _rope_swap(yb, rope_g) * sin
            if q_tiles:
                yb = yb * col_scale
            o_ref[:, j * LANES:(j + 1) * LANES] = yb.astype(o_ref.dtype)
    else:
        o_ref = next(it)
        if q_tiles:
            y = y * col_scale
        o_ref[...] = y.astype(o_ref.dtype)


def _project(x, w_bf16, *, tile_off=0, n_tiles, mods=None, layer=0, chunks=(0, 1), rope=None, q_cols=0,
             q_scale=1.0, out_dtype=F32, name):
    k_dim, n = w_bf16.shape
    tn = min(n, 1024)
    assert q_cols % tn == 0
    grid = (n // tn, n_tiles)
    in_specs = [pl.BlockSpec((TM, k_dim), lambda j, i: (i + tile_off, 0))]
    args = [x]
    if mods is not None:
        in_specs += [_mod_spec(layer, chunks[0], tile_off, 1), _mod_spec(layer, chunks[1], tile_off, 1)]
        args += [mods, mods]
    in_specs.append(pl.BlockSpec((k_dim, tn), lambda j, i: (0, j)))
    args.append(w_bf16)
    rope_g = 0
    if rope is not None:
        rope_g, cos, sin = rope
        tab = pl.BlockSpec((TM, LANES), lambda j, i: (i % TILES_PER_DEC, 0))
        in_specs += [tab, tab]
        args += [cos, sin]
    return pl.pallas_call(
        functools.partial(_proj_kernel, use_mod=mods is not None, rope_g=rope_g, q_tiles=q_cols // tn,
                          q_scale=q_scale),
        grid=grid,
        in_specs=in_specs,
        out_specs=pl.BlockSpec((TM, tn), lambda j, i: (i, j)),
        out_shape=jax.ShapeDtypeStruct((n_tiles * TM, n), out_dtype),
        compiler_params=_cparams(2),
        name=name,
    )(*args)


def _context_qkv(x, w_bf16, mods, layer, qk_scale, tag):
    common = dict(n_tiles=PROMPT_TILES, mods=mods, layer=layer)
    q = _project(x, w_bf16[:, :D_MODEL], name=tag + "_q_context", **common, **qk_scale)
    k = _project(x, w_bf16[:, D_MODEL:2 * D_MODEL], name=tag + "_k_context", **common)
    v = _project(x, w_bf16[:, 2 * D_MODEL:], name=tag + "_v_context", **common)
    return q, k, v


def _axial_tables(group_lanes, nf, lane_lo, lane_hi):
    del group_lanes
    t = jnp.arange(DEC_SEQ)
    lane = jnp.arange(LANES)
    rel = lane - lane_lo
    active = (lane >= lane_lo) & (lane < lane_hi)
    grp = rel // (2 * nf)
    pos = jnp.where((grp % 2 == 0)[None, :], (t // GRID_W)[:, None], (t % GRID_W)[:, None])
    inv = ROPE_BASE ** (-jnp.arange(nf, dtype=F32) / nf)
    ang = pos.astype(F32) * inv[rel % nf][None, :]
    cos = jnp.cos(ang)
    sin = jnp.sin(ang)
    sign = jnp.where((rel % (2 * nf)) < nf, -1.0, 1.0)[None, :]
    cos = jnp.where(active[None, :], cos, 1.0).astype(F32)
    sin = jnp.where(active[None, :], sin * sign, 0.0).astype(F32)
    return cos, sin


def _nt_dot(a, b):
    return lax.dot_general(a, b, (((1,), (1,)), ((), ())), preferred_element_type=F32)


def _half_values(vs, half):
    lane = lax.broadcasted_iota(jnp.int32, (1, LANES), 1)
    keep = jnp.where((lane < 64) if half == 0 else (lane >= 64), 1.0, 0.0).astype(BF16)
    return [v * keep + (1 - keep) for v in vs]


def _pair_softmax_pv(scores_a, scores_b, vs):
    outs = []
    for half, scores in enumerate((scores_a, scores_b)):
        m = scores[0].max(axis=-1, keepdims=True)
        for s in scores[1:]:
            m = jnp.maximum(m, s.max(axis=-1, keepdims=True))
        acc = None
        for s, v in zip(scores, _half_values(vs, half)):
            o = jnp.dot(jnp.exp(s - m).astype(BF16), v, preferred_element_type=F32)
            acc = o if acc is None else acc + o
        outs.append(acc / pltpu.roll(acc, 64, 1))
    lane = lax.broadcasted_iota(jnp.int32, outs[0].shape, 1)
    return jnp.where(lane < 64, outs[0], outs[1])


def _softmax_pv(scores, vs):
    m = scores[0].max(axis=-1, keepdims=True)
    for s in scores[1:]:
        m = jnp.maximum(m, s.max(axis=-1, keepdims=True))
    acc, den = None, None
    for s, v in zip(scores, vs):
        e = jnp.exp(s - m)
        d = e.sum(axis=-1, keepdims=True)
        o = jnp.dot(e.astype(BF16), v, preferred_element_type=F32)
        acc = o if acc is None else acc + o
        den = d if den is None else den + d
    return acc / den


def _attn_kernel(*refs, mode, has_ctx, lam_init, hb):
    it = iter(refs)
    q_ref, k_ref, v_ref = next(it), next(it), next(it)
    kc_ref = vc_ref = None
    if has_ctx:
        kc_ref, vc_ref = next(it), next(it)
    if mode == "diff":
        lam_ref, subln_ref = next(it), next(it)
    o_ref = next(it)

    wq = 2 * LANES if mode == "mla" else LANES
    if mode == "diff":
        lp = lam_ref[...]
        lam = (jnp.exp(jnp.sum(lp[0:1] * lp[1:2], axis=-1, keepdims=True))
               - jnp.exp(jnp.sum(lp[2:3] * lp[3:4], axis=-1, keepdims=True)) + lam_init)

    for hblk in range(hb):
        qk_cols = slice(hblk * wq, (hblk + 1) * wq)
        v_cols = slice(hblk * LANES, (hblk + 1) * LANES)
        q = q_ref[:, qk_cols].astype(BF16)
        ks = [k_ref[:, qk_cols].astype(BF16)]
        vs = [v_ref[:, v_cols].astype(BF16)]
        if has_ctx:
            ks.append(kc_ref[:, qk_cols].astype(BF16))
            vs.append(vc_ref[:, v_cols].astype(BF16))

        if mode == "mla":
            halves = [(q[:, :LANES], [k[:, :LANES] for k in ks]), (q[:, LANES:], [k[:, LANES:] for k in ks])]
        else:
            lane_q = lax.broadcasted_iota(jnp.int32, q.shape, 1)
            zero = jnp.zeros_like(q)
            halves = [(jnp.where(lane_q < 64, q, zero), ks), (jnp.where(lane_q >= 64, q, zero), ks)]

        scores = [[_nt_dot(qh, k) for k in kh] for qh, kh in halves]
        if mode == "diff":
            o = _softmax_pv(scores[0], vs) - lam * _softmax_pv(scores[1], vs)
            o = o * lax.rsqrt(jnp.mean(o * o, axis=-1, keepdims=True) + RMS_EPS) * subln_ref[...]
            o_ref[:, v_cols] = o * (1.0 - lam_init)
        else:
            o_ref[:, v_cols] = _pair_softmax_pv(scores[0], scores[1], vs)


def _attention(q_arr, k_arr, v_arr, *, mode, n_batch, seq, tq, q_col, k_col, v_col, row_off=0, v_row_off=None,
               ctx=None, lam=None, subln=None, lam_init=0.0, hb=1, name):
    assert mode != "mla" or (q_col == 0 and k_col == 0)
    n_blocks = D_MODEL // LANES
    assert n_blocks % hb == 0 and q_col % hb == 0 and k_col % hb == 0 and v_col % hb == 0
    wq = 2 * LANES if mode == "mla" else LANES
    wqb, wvb = wq * hb, LANES * hb
    nqb = seq // tq
    qoff = row_off * nqb
    if v_row_off is None:
        v_row_off = row_off
    in_specs = [
        pl.BlockSpec((tq, wqb), lambda b, h, i: (qoff + b * nqb + i, q_col // hb + h)),
        pl.BlockSpec((seq, wqb), lambda b, h, i: (row_off + b, k_col // hb + h)),
        pl.BlockSpec((seq, wvb), lambda b, h, i: (v_row_off + b, v_col // hb + h)),
    ]
    args = [q_arr, k_arr, v_arr]
    if ctx is not None:
        kc, vc, j = ctx
        in_specs += [
            pl.BlockSpec((None, None, PAST_LEN, wqb), lambda b, h, i: (b, j, 0, h)),
            pl.BlockSpec((None, None, PAST_LEN, wvb), lambda b, h, i: (b, j, 0, h)),
        ]
        args += [kc, vc]
    if mode == "diff":
        in_specs += [pl.BlockSpec((4, DIFF_HEAD_DIM), lambda b, h, i: (0, 0)),
                     pl.BlockSpec((1, LANES), lambda b, h, i: (0, 0))]
        args += [lam, subln]
    return pl.pallas_call(
        functools.partial(_attn_kernel, mode=mode, has_ctx=ctx is not None, lam_init=lam_init, hb=hb),
        grid=(n_batch, n_blocks // hb, nqb),
        in_specs=in_specs,
        out_specs=pl.BlockSpec((tq, wvb), lambda b, h, i: (b * nqb + i, h)),
        out_shape=jax.ShapeDtypeStruct((n_batch * seq, D_MODEL), F32),
        compiler_params=_cparams(3),
        name=name,
    )(*args)


def _na_bias_tables(rpb):
    cols = jnp.arange(GRID_W)
    c0 = jnp.clip(cols - WIN_C // 2, 0, GRID_W - WIN_C)
    col_ok = (cols[None, :] >= c0[:, None]) & (cols[None, :] < c0[:, None] + WIN_C)
    col_off = jnp.clip(cols[None, :] - cols[:, None] + (WIN_C - 1), 0, 2 * WIN_C - 2)
    hot_c = (col_off[:, :, None] == jnp.arange(2 * WIN_C - 1)).astype(F32)
    core = jnp.einsum("hab,ckb->hcak", rpb, hot_c, precision=HIGHEST)
    core = jnp.where(col_ok[None, :, None, :], core, NEG_INF)
    pad_hi = NA_BIAS_ROWS - NA_BIAS_PAD - (2 * WIN_R - 1)
    tab = jnp.pad(core, ((0, 0), (0, 0), (NA_BIAS_PAD, pad_hi), (0, 0)), constant_values=NEG_INF)
    tab = tab.reshape(NA_HEADS, GRID_W, NA_BIAS_ROWS * GRID_W)
    shifted = jnp.pad(tab[:, :, GRID_W:], ((0, 0), (0, 0), (0, GRID_W)), constant_values=NEG_INF)
    return jnp.stack([tab, shifted]).astype(F32)


def _na_block_bias(bias_ref, half, kind):
    nk = NA_KROWS * GRID_W
    lane = lax.broadcasted_iota(jnp.int32, (GRID_W, nk), 1)
    slabs = []
    for i in range(NA_QROWS):
        if kind == 0:
            ext0, lo, hi = WIN_R - 1 + NA_BIAS_PAD - i, 0, WIN_R
        elif kind == 1:
            ext0, lo, hi = WIN_R - 1 - WIN_R // 2 + NA_BIAS_PAD - i, i, i + WIN_R
        else:
            ext0, lo, hi = NA_BIAS_PAD - i, NA_KROWS - WIN_R, NA_KROWS
        start = (ext0 - ext0 % 2) * GRID_W
        slab = bias_ref[ext0 % 2, half, :, start:start + nk]
        slabs.append(jnp.where((lane >= lo * GRID_W) & (lane < hi * GRID_W), slab, NEG_INF))
    return jnp.concatenate(slabs, axis=0)


def _na_latent_kernel(q_ref, k_ref, v_ref, kc_ref, vc_ref, bias_ref, o_ref):
    rows_n = DEC_SEQ // GRID_W
    kc = kc_ref[...].astype(BF16)
    vc = vc_ref[...].astype(BF16)
    tq = NA_QROWS * GRID_W
    nk = NA_KROWS * GRID_W
    lane = lax.broadcasted_iota(jnp.int32, (tq, LANES), 1)
    biases = {}
    for blk in range(rows_n // NA_QROWS):
        q_row0 = blk * NA_QROWS
        if blk == 0:
            kind, k_row0 = 0, 0
        elif blk == rows_n // NA_QROWS - 1:
            kind, k_row0 = 2, rows_n - NA_KROWS
        else:
            kind, k_row0 = 1, q_row0 - WIN_R // 2
        q = q_ref[q_row0 * GRID_W:q_row0 * GRID_W + tq, :].astype(BF16)
        k = k_ref[k_row0 * GRID_W:k_row0 * GRID_W + nk, :].astype(BF16)
        v = v_ref[k_row0 * GRID_W:k_row0 * GRID_W + nk, :].astype(BF16)
        zero = jnp.zeros_like(q)
        scores = []
        for half in range(2):
            qh = jnp.where((lane < 64) if half == 0 else (lane >= 64), q, zero)
            if (kind, half) not in biases:
                biases[kind, half] = _na_block_bias(bias_ref, half, kind)
            scores.append([_nt_dot(qh, k) + biases[kind, half], _nt_dot(qh, kc)])
        o_ref[q_row0 * GRID_W:q_row0 * GRID_W + tq, :] = _pair_softmax_pv(scores[0], scores[1], [v, vc])


def _na_latent_attention(qkv, cache_k, cache_v, j, bias):
    nb = D_MODEL // LANES
    return pl.pallas_call(
        _na_latent_kernel,
        grid=(nb, DEC_BATCH),
        in_specs=[
            pl.BlockSpec((DEC_SEQ, LANES), lambda h, b: (b, h)),
            pl.BlockSpec((DEC_SEQ, LANES), lambda h, b: (b, nb + h)),
            pl.BlockSpec((DEC_SEQ, LANES), lambda h, b: (b, 2 * nb + h)),
            pl.BlockSpec((None, None, PAST_LEN, LANES), lambda h, b: (b, j, 0, h)),
            pl.BlockSpec((None, None, PAST_LEN, LANES), lambda h, b: (b, j, 0, h)),
            pl.BlockSpec((2, 2, GRID_W, NA_BIAS_ROWS * GRID_W), lambda h, b: (0, h, 0, 0)),
        ],
        out_specs=pl.BlockSpec((DEC_SEQ, LANES), lambda h, b: (b, h)),
        out_shape=jax.ShapeDtypeStruct((N_SAMPLE, D_MODEL), F32),
        compiler_params=_cparams(2),
        name="na_latent_attention",
    )(qkv, qkv, qkv, cache_k, cache_v, bias)


def _out_ln_kernel(op_ref, os_ref, w_ref, x_ref, gate_ref, g_ref, b_ref, xo_ref):
    def finish(o_ref):
        y = jnp.dot(o_ref[...].astype(BF16), w_ref[...], preferred_element_type=F32)
        z = DEEPNORM_ALPHA * x_ref[...] + gate_ref[...] * y
        xo_ref[...] = _layernorm(z, g_ref[...], b_ref[...])

    @pl.when(pl.program_id(0) < PROMPT_TILES)
    def _():
        finish(op_ref)

    @pl.when(pl.program_id(0) >= PROMPT_TILES)
    def _():
        finish(os_ref)


def _out_proj_ln(o_p, o_s, w_bf16, x, mods, layer, ln_g, ln_b):
    k_dim = w_bf16.shape[0]
    row = pl.BlockSpec((1, D_MODEL), lambda i: (0, 0))
    return pl.pallas_call(
        _out_ln_kernel,
        grid=(N_TOK // TM,),
        in_specs=[
            pl.BlockSpec((TM, k_dim), lambda i: (jnp.minimum(i, PROMPT_TILES - 1), 0)),
            pl.BlockSpec((TM, k_dim), lambda i: (jnp.maximum(i - PROMPT_TILES, 0), 0)),
            pl.BlockSpec((k_dim, D_MODEL), lambda i: (0, 0)),
            pl.BlockSpec((TM, D_MODEL), lambda i: (i, 0)),
            _mod_spec(layer, 2, 0, 0),
            row, row,
        ],
        out_specs=pl.BlockSpec((TM, D_MODEL), lambda i: (i, 0)),
        out_shape=jax.ShapeDtypeStruct((N_TOK, D_MODEL), F32),
        compiler_params=_cparams(1),
        name="out_proj_layernorm",
    )(o_p, o_s, w_bf16, x, mods, ln_g, ln_b)


def _route_kernel(x_ref, shift_ref, scale_ref, rw_ref, rb_ref, xs_ref, meta_ref, cnt_ref):
    @pl.when(pl.program_id(0) < N_CHUNKS)
    def _():
        _route_chunk(x_ref, shift_ref, scale_ref, rw_ref, rb_ref, xs_ref, meta_ref, cnt_ref)

    @pl.when(pl.program_id(0) == N_CHUNKS)
    def _():
        xs_ref[...] = jnp.zeros_like(xs_ref)


def _route_chunk(x_ref, shift_ref, scale_ref, rw_ref, rb_ref, xs_ref, meta_ref, cnt_ref):
    h = x_ref[...] * (1.0 + scale_ref[...]) + shift_ref[...]
    h_hi = h.astype(BF16)
    h_lo = (h - h_hi.astype(F32)).astype(BF16)
    logits = (jnp.dot(h_hi, rw_ref[0], preferred_element_type=F32)
              + jnp.dot(h_lo, rw_ref[0], preferred_element_type=F32)
              + jnp.dot(h_hi, rw_ref[1], preferred_element_type=F32)) + rb_ref[...]
    lane = lax.broadcasted_iota(jnp.int32, logits.shape, 1)
    work = logits
    top_v, hots = [], []
    for _ in range(TOP_K):
        m = work.max(axis=-1, keepdims=True)
        first = jnp.where(work == m, lane, N_EXPERTS).min(axis=-1, keepdims=True)
        hot = lane == first
        top_v.append(m)
        hots.append(jnp.where(hot, 1.0, 0.0))
        work = jnp.where(hot, -jnp.inf, work)
    es = [jnp.exp(v - top_v[0]) for v in top_v]
    den = es[0] + es[1] + es[2] + es[3]
    gates = [e / den for e in es]

    cnts = [hf.sum(axis=0, keepdims=True) for hf in hots]
    n_e = cnts[0] + cnts[1] + cnts[2] + cnts[3]
    pieces = jnp.floor((n_e + (MOE_PIECE - 1)) * (1.0 / MOE_PIECE))
    er = lax.broadcasted_iota(jnp.int32, (N_EXPERTS, N_EXPERTS), 0)
    ec = lax.broadcasted_iota(jnp.int32, (N_EXPERTS, N_EXPERTS), 1)
    upper = jnp.where(er < ec, 1.0, 0.0).astype(BF16)
    poff = jnp.dot(jnp.broadcast_to(pieces, (8, N_EXPERTS)).astype(BF16), upper,
                   preferred_element_type=F32)[0:1]

    tr = lax.broadcasted_iota(jnp.int32, (TM, TM), 0)
    tc = lax.broadcasted_iota(jnp.int32, (TM, TM), 1)
    lower = jnp.where(tc < tr, 1.0, 0.0).astype(BF16)
    run = poff * MOE_PIECE
    slots = []
    for k in range(TOP_K):
        prefix = jnp.dot(lower, hots[k].astype(BF16), preferred_element_type=F32)
        slots.append(jnp.sum(hots[k] * (run + prefix), axis=-1, keepdims=True))
        run = run + cnts[k]

    lane_w = lax.broadcasted_iota(jnp.int32, (TM, LANES), 1)
    meta = jnp.full((TM, LANES), -1.0, F32)
    for k in range(TOP_K):
        meta = jnp.where(lane_w == k, slots[k], meta)
        meta = jnp.where(lane_w == TOP_K + k, gates[k], meta)
    meta_ref[...] = meta[:, :2 * TOP_K]
    cnt_ref[...] = pieces

    slot_rows = meta.T
    srow = lax.broadcasted_iota(jnp.int32, (MOE_CHUNK_ROWS, TM), 0).astype(F32)
    p = jnp.zeros((MOE_CHUNK_ROWS, TM), F32)
    for k in range(TOP_K):
        p = p + jnp.where(srow == slot_rows[k:k + 1, :], 1.0, 0.0)
    xs_ref[...] = jnp.dot(p.astype(BF16), h_hi, preferred_element_type=F32)


def _route(x, mods, layer, rw, rb):
    assert 2 * EXP_TM <= MOE_CHUNK_ROWS
    last = N_CHUNKS - 1
    rw_hi = rw.astype(BF16)
    rw_split = jnp.stack([rw_hi, (rw - rw_hi.astype(F32)).astype(BF16)], axis=1)
    return pl.pallas_call(
        _route_kernel,
        grid=(N_CHUNKS + 1,),
        in_specs=[
            pl.BlockSpec((TM, D_MODEL), lambda i: (jnp.minimum(i, last), 0)),
            _mod_spec(layer, 3, 0, 0),
            _mod_spec(layer, 4, 0, 0),
            pl.BlockSpec((None, 2, D_MODEL, N_EXPERTS), lambda i: (layer, 0, 0, 0)),
            pl.BlockSpec((None, 1, N_EXPERTS), lambda i: (layer, 0, 0)),
        ],
        out_specs=[
            pl.BlockSpec((MOE_CHUNK_ROWS, D_MODEL), lambda i: (i, 0)),
            pl.BlockSpec((TM, 2 * TOP_K), lambda i: (jnp.minimum(i, last), 0)),
            pl.BlockSpec((None, 1, N_EXPERTS), lambda i: (jnp.minimum(i, last), 0, 0)),
        ],
        out_shape=[
            jax.ShapeDtypeStruct(((N_CHUNKS + 1) * MOE_CHUNK_ROWS, D_MODEL), F32),
            jax.ShapeDtypeStruct((N_TOK, 2 * TOP_K), F32),
            jax.ShapeDtypeStruct((N_CHUNKS, 1, N_EXPERTS), F32),
        ],
        compiler_params=_cparams(1),
        name="moe_route",
    )(x, mods, mods, rw_split, rb.reshape(DEPTH, 1, N_EXPERTS))


def _moe_schedule(pieces):
    e_ids = jnp.arange(N_EXPERTS, dtype=jnp.int32)
    npe = pieces.sum(axis=0)
    tiles_e = (npe + EXP_PPT - 1) // EXP_PPT
    tile_end = jnp.cumsum(tiles_e)
    tile_start = tile_end - tiles_e
    g = jnp.arange(MAX_TILES, dtype=jnp.int32)
    last_e = jnp.max(jnp.where(tiles_e > 0, e_ids, 0))
    te = jnp.minimum(jnp.sum(tile_end[None, :] <= g[:, None], axis=1), last_e).astype(jnp.int32)
    seg_len = pieces.T
    seg_base = tile_start[:, None] * EXP_PPT + jnp.cumsum(seg_len, axis=1) - seg_len
    poff = jnp.cumsum(pieces, axis=1) - pieces
    seg_pos = (jnp.arange(N_CHUNKS, dtype=jnp.int32)[:, None] * MOE_CHUNK_PIECES + poff).T
    hot = (te[:, None] == e_ids[None, :])[:, :, None]
    base, length, first = (jnp.sum(jnp.where(hot, tab[None], 0), axis=1)
                           for tab in (seg_base, seg_len, seg_pos))
    lane = jnp.arange(EXP_PPT, dtype=jnp.int32)[None, :]
    pos = (g[:, None] * EXP_PPT + lane)[:, :, None]
    hit = (base[:, None, :] <= pos) & (pos < (base + length)[:, None, :])
    src = jnp.sum(jnp.where(hit, first[:, None, :] + pos - base[:, None, :], 0), axis=-1)
    valid = jnp.any(hit, axis=-1)
    scratch = N_CHUNKS * MOE_CHUNK_PIECES + (g[:, None] % 2) * EXP_PPT + lane
    dst = jnp.where(valid, src, scratch).reshape(-1).astype(jnp.int32)
    src = jnp.where(valid, src, src[:, :1]).reshape(-1).astype(jnp.int32)
    return te, tile_end[-1:].astype(jnp.int32), src, dst


def _expert_kernel(te_ref, nt_ref, src_ref, dst_ref, xs_hbm, w1_ref, b1_ref, w2_ref, b2_ref, out_hbm,
                   xbuf, ybuf, w1s, w2s, sem_in, sem_out):
    g = pl.program_id(0)
    nt = nt_ref[0]
    slot = g % 2

    def in_copy(tile, i, s):
        row = pl.multiple_of(src_ref[tile * EXP_PPT + i] * MOE_PIECE, MOE_PIECE)
        return pltpu.make_async_copy(xs_hbm.at[pl.ds(row, MOE_PIECE), :],
                                     xbuf.at[s, pl.ds(i * MOE_PIECE, MOE_PIECE), :], sem_in.at[s])

    def out_copy(tile, i, s):
        row = pl.multiple_of(dst_ref[tile * EXP_PPT + i] * MOE_PIECE, MOE_PIECE)
        return pltpu.make_async_copy(ybuf.at[s, pl.ds(i * MOE_PIECE, MOE_PIECE), :],
                                     out_hbm.at[pl.ds(row, MOE_PIECE), :], sem_out.at[s])

    def gather(tile, s, wait):
        for i in range(EXP_PPT):
            cp = in_copy(tile, i, s)
            cp.wait() if wait else cp.start()

    def scatter(tile, s, wait):
        for i in range(EXP_PPT):
            cp = out_copy(tile, i, s)
            cp.wait() if wait else cp.start()

    @pl.when(g < nt)
    def _():
        @pl.when(g == 0)
        def _():
            gather(0, 0, False)

        @pl.when(g + 1 < nt)
        def _():
            gather(g + 1, 1 - slot, False)

        gather(g, slot, True)

        @pl.when(g >= 2)
        def _():
            scatter(g - 2, slot, True)

        @pl.when((g == 0) | (te_ref[g] != te_ref[jnp.maximum(g - 1, 0)]))
        def _():
            w1s[...] = w1_ref[...].astype(BF16)
            w2s[...] = w2_ref[...].astype(BF16)

        gu = jnp.dot(xbuf[slot].astype(BF16), w1s[...], preferred_element_type=F32) + b1_ref[...]
        gl = jnp.minimum(gu[:, :D_EXPERT], SWIGLU_LIMIT)
        ul = jnp.clip(gu[:, D_EXPERT:], -SWIGLU_LIMIT, SWIGLU_LIMIT)
        act = (ul + 1.0) * (gl * _sigmoid(SWIGLU_ALPHA * gl))
        ybuf[slot] = jnp.dot(act.astype(BF16), w2s[...], preferred_element_type=F32) + b2_ref[...]
        scatter(g, slot, False)

        @pl.when(g == nt - 1)
        def _():
            @pl.when(g >= 1)
            def _():
                scatter(g - 1, 1 - slot, True)
            scatter(g, slot, True)


def _experts(xs, te, nt, src, dst, layer, w1, b1, w2, b2):
    def expert_block(g, te, nt, src, dst):
        return (layer, te[g], 0, 0)

    grid_spec = pltpu.PrefetchScalarGridSpec(
        num_scalar_prefetch=4,
        grid=(MAX_TILES,),
        in_specs=[
            pl.BlockSpec(memory_space=pl.ANY),
            pl.BlockSpec((None, None, D_MODEL, 2 * D_EXPERT), expert_block),
            pl.BlockSpec((None, None, 1, 2 * D_EXPERT), expert_block),
            pl.BlockSpec((None, None, D_EXPERT, D_MODEL), expert_block),
            pl.BlockSpec((None, None, 1, D_MODEL), expert_block),
        ],
        out_specs=pl.BlockSpec(memory_space=pl.ANY),
        scratch_shapes=[
            pltpu.VMEM((2, EXP_TM, D_MODEL), F32),
            pltpu.VMEM((2, EXP_TM, D_MODEL), F32),
            pltpu.VMEM((D_MODEL, 2 * D_EXPERT), BF16),
            pltpu.VMEM((D_EXPERT, D_MODEL), BF16),
            pltpu.SemaphoreType.DMA((2,)),
            pltpu.SemaphoreType.DMA((2,)),
        ],
    )
    return pl.pallas_call(
        _expert_kernel,
        grid_spec=grid_spec,
        out_shape=jax.ShapeDtypeStruct(xs.shape, F32),
        input_output_aliases={4: 0},
        compiler_params=_cparams(1),
        name="moe_experts",
    )(te, nt, src, dst, xs, w1, b1.reshape(DEPTH, N_EXPERTS, 1, 2 * D_EXPERT),
      w2, b2.reshape(DEPTH, N_EXPERTS, 1, D_MODEL))


def _combine_kernel(ys_ref, meta_ref, x_ref, gate_ref, g_ref, b_ref, xo_ref):
    meta = meta_ref[...]
    col = lax.broadcasted_iota(jnp.int32, (TM, MOE_CHUNK_ROWS), 1).astype(F32)
    gmat = jnp.zeros((TM, MOE_CHUNK_ROWS), F32)
    for k in range(TOP_K):
        gmat = gmat + jnp.where(col == meta[:, k:k + 1], meta[:, TOP_K + k:TOP_K + k + 1], 0.0)
    y = jnp.dot(gmat.astype(BF16), ys_ref[...].astype(BF16), preferred_element_type=F32)
    z = DEEPNORM_ALPHA * x_ref[...] + gate_ref[...] * y
    xo_ref[...] = _layernorm(z, g_ref[...], b_ref[...])


def _combine_ln(ys, meta, x, mods, layer, ln_g, ln_b, chunk_off=0, n_chunks=N_CHUNKS):
    row = pl.BlockSpec((1, D_MODEL), lambda i: (0, 0))
    return pl.pallas_call(
        _combine_kernel,
        grid=(n_chunks,),
        in_specs=[
            pl.BlockSpec((MOE_CHUNK_ROWS, D_MODEL), lambda i: (i + chunk_off, 0)),
            pl.BlockSpec((TM, 2 * TOP_K), lambda i: (i + chunk_off, 0)),
            pl.BlockSpec((TM, D_MODEL), lambda i: (i + chunk_off, 0)),
            _mod_spec(layer, 5, chunk_off, 0),
            row, row,
        ],
        out_specs=pl.BlockSpec((TM, D_MODEL), lambda i: (i, 0)),
        out_shape=jax.ShapeDtypeStruct((n_chunks * TM, D_MODEL), F32),
        compiler_params=_cparams(1),
        name="moe_combine_layernorm",
    )(ys, meta, x, mods, ln_g, ln_b)


def _moe(x, mods, layer, rw, rb, w1, b1, w2, b2, ln_g, ln_b, split_out=False):
    xs, meta, pieces = _route(x, mods, layer, rw, rb)
    te, nt, src, dst = _moe_schedule(pieces.reshape(N_CHUNKS, N_EXPERTS).astype(jnp.int32))
    ys = _experts(xs, te, nt, src, dst, layer, w1, b1, w2, b2)
    if not split_out:
        return _combine_ln(ys, meta, x, mods, layer, ln_g, ln_b)
    return (_combine_ln(ys, meta, x, mods, layer, ln_g, ln_b, 0, PROMPT_TILES),
            _combine_ln(ys, meta, x, mods, layer, ln_g, ln_b, PROMPT_TILES, SAMPLE_TILES))


def _mla_norm_kernel(p_ref, qn_ref, kvn_ref, cq_ref, ckvkr_ref):
    p = p_ref[...]
    cq = p[:, :MLA_Q_RANK]
    cq_ref[...] = cq * lax.rsqrt(jnp.mean(cq * cq, axis=-1, keepdims=True) + RMS_EPS) * qn_ref[...]
    ckv = p[:, MLA_Q_RANK:MLA_Q_RANK + MLA_KV_RANK]
    ckvkr_ref[:, :MLA_KV_RANK] = (ckv * lax.rsqrt(jnp.mean(ckv * ckv, axis=-1, keepdims=True) + RMS_EPS)
                                  * kvn_ref[...])
    ckvkr_ref[:, MLA_KV_RANK:] = p[:, MLA_Q_RANK + MLA_KV_RANK:]


def _mla_norm(proj, q_norm, kv_norm):
    return pl.pallas_call(
        _mla_norm_kernel,
        grid=(N_TOK // TM,),
        in_specs=[
            pl.BlockSpec((TM, 512), lambda i: (i, 0)),
            pl.BlockSpec((1, MLA_Q_RANK), lambda i: (0, 0)),
            pl.BlockSpec((1, MLA_KV_RANK), lambda i: (0, 0)),
        ],
        out_specs=[pl.BlockSpec((TM, MLA_Q_RANK), lambda i: (i, 0)),
                   pl.BlockSpec((TM, 2 * MLA_KV_RANK), lambda i: (i, 0))],
        out_shape=[jax.ShapeDtypeStruct((N_TOK, MLA_Q_RANK), F32),
                   jax.ShapeDtypeStruct((N_TOK, 2 * MLA_KV_RANK), F32)],
        compiler_params=_cparams(1),
        name="mla_norm",
    )(proj, q_norm, kv_norm)


def _mla_weights(w_in, w_uq, w_ukv):
    pad = 512 - w_in.shape[1]
    w_in_p = jnp.pad(w_in, ((0, 0), (0, pad)))
    wq = w_uq.reshape(MLA_Q_RANK, MLA_HEADS, MLA_NOPE + MLA_ROPE)
    wq = jnp.pad(wq, ((0, 0), (0, 0), (0, LANES - MLA_NOPE - MLA_ROPE))).reshape(MLA_Q_RANK, MLA_HEADS * LANES)
    wkv = w_ukv.reshape(MLA_KV_RANK, MLA_HEADS, MLA_NOPE + MLA_V)
    wkn = jnp.pad(wkv[:, :, :MLA_NOPE], ((0, 0), (0, 0), (0, LANES - MLA_NOPE)))
    eye = jnp.pad(jnp.eye(MLA_ROPE, dtype=F32), ((0, 0), (MLA_NOPE, LANES - MLA_NOPE - MLA_ROPE)))
    wkr = jnp.broadcast_to(eye[:, None, :], (MLA_ROPE, MLA_HEADS, LANES))
    zeros = jnp.zeros((2 * MLA_KV_RANK - MLA_KV_RANK - MLA_ROPE, MLA_HEADS, LANES), F32)
    wk = jnp.concatenate([wkn, wkr, zeros], axis=0).reshape(2 * MLA_KV_RANK, MLA_HEADS * LANES)
    wv = jnp.pad(wkv[:, :, MLA_NOPE:].reshape(MLA_KV_RANK, MLA_HEADS * MLA_V), ((0, MLA_KV_RANK), (0, 0)))
    return w_in_p.astype(BF16), wq.astype(BF16), wk.astype(BF16), wv.astype(BF16)


def kernel(x_prompt, x_sample, cache_na_k, cache_na_v, cache_diff_k, cache_diff_v, cache_mla_ckv, cache_mla_kr,
           c, c_ctx, ada_w, ada_b, ln_g, ln_b, na_w_qkv, na_rpb, na_w_o, diff_w_qkv, diff_lambda, diff_subln,
           diff_w_o, mla_w_in, mla_q_norm, mla_w_uq, mla_kv_norm, mla_w_ukv, mla_w_o, router_w, router_b,
           moe_w1, moe_b1, moe_w2, moe_b2):
    x = jnp.concatenate([x_prompt.reshape(N_PROMPT, D_MODEL), x_sample.reshape(N_SAMPLE, D_MODEL)], axis=0)
    cond = jnp.concatenate([c_ctx[None, :], c, jnp.zeros((N_MOD_ROWS - 1 - DEC_BATCH, D_MODEL), F32)], axis=0)
    mods = _modulation(cond, ada_w, ada_b)

    n_na = na_w_qkv.shape[0]
    n_diff = diff_w_qkv.shape[0]
    n_mla = mla_w_in.shape[0]
    cache_na_k = cache_na_k.reshape(DEC_BATCH, n_na, PAST_LEN, D_MODEL)
    cache_na_v = cache_na_v.reshape(DEC_BATCH, n_na, PAST_LEN, D_MODEL)
    cache_diff_k = cache_diff_k.reshape(DEC_BATCH, n_diff, PAST_LEN, D_MODEL)
    cache_diff_v = cache_diff_v.reshape(DEC_BATCH, n_diff, PAST_LEN, D_MODEL)
    all_tiles = N_TOK // TM
    nb = D_MODEL // LANES
    dec_off = N_PROMPT // DEC_SEQ

    st_na_k, st_na_v, st_diff_k, st_diff_v, st_mla_ckv, st_mla_kr = [], [], [], [], [], []
    for i in range(DEPTH):
        kind, j = i % N_MIXERS, i // N_MIXERS
        latent = dict(tile_off=PROMPT_TILES, mods=mods, layer=i)
        if kind == 0:
            w = na_w_qkv[j].astype(BF16)
            qk_scale = dict(q_cols=D_MODEL, q_scale=NA_HEAD_DIM ** -0.5, out_dtype=BF16)
            q_p, k_p, v_p = _context_qkv(x, w, mods, i, qk_scale, "na")
            st_na_k.append(k_p)
            st_na_v.append(v_p)
            qkv_s = _project(x, w, n_tiles=SAMPLE_TILES, name="na_qkv_latent", **latent, **qk_scale)
            o_p = _attention(q_p, k_p, v_p, mode="pair", n_batch=BATCH, seq=SEQ, tq=SEQ, q_col=0, k_col=0,
                             v_col=0, hb=nb, name="na_context_attention")
            o_s = _na_latent_attention(qkv_s, cache_na_k, cache_na_v, j, _na_bias_tables(na_rpb[j]))
            w_o = na_w_o[j]
        elif kind == 1:
            lam_init = 0.8 - 0.6 * math.exp(-0.3 * i)
            w = diff_w_qkv[j].astype(BF16)
            cos, sin = _axial_tables(LANES, DIFF_HEAD_DIM // 4, 0, LANES)
            qk_scale = dict(q_cols=D_MODEL, q_scale=DIFF_HEAD_DIM ** -0.5, out_dtype=BF16)
            q_p, k_p, v_p = _context_qkv(x, w, mods, i, qk_scale, "diff")
            st_diff_k.append(k_p)
            st_diff_v.append(v_p)
            qk_s = _project(x, w[:, :2 * D_MODEL], n_tiles=SAMPLE_TILES, rope=(DIFF_HEAD_DIM // 4, cos, sin),
                            name="diff_qk_latent", **latent, **qk_scale)
            v_s = _project(x, w[:, 2 * D_MODEL:], n_tiles=SAMPLE_TILES, out_dtype=BF16, name="diff_v_latent",
                           **latent)
            common = dict(mode="diff", lam=diff_lambda[j], subln=diff_subln[j][None, :], lam_init=lam_init)
            o_p = _attention(q_p, k_p, v_p, n_batch=BATCH, seq=SEQ, tq=SEQ, q_col=0, k_col=0, v_col=0, hb=nb,
                             name="diff_context_attention", **common)
            o_s = _attention(qk_s, qk_s, v_s, n_batch=DEC_BATCH, seq=DEC_SEQ, tq=LATENT_TQ,q_col=0, k_col=nb, v_col=0,
                             ctx=(cache_diff_k, cache_diff_v, j), hb=2, name="diff_latent_attention", **common)
            w_o = diff_w_o[j]
        else:
            w_in, wq, wk, wv = _mla_weights(mla_w_in[j], mla_w_uq[j], mla_w_ukv[j])
            cos, sin = _axial_tables(LANES, MLA_ROPE // 4, MLA_NOPE, MLA_NOPE + MLA_ROPE)
            rope = (MLA_ROPE // 4, cos, sin)
            proj = _project(x, w_in, n_tiles=all_tiles, mods=mods, layer=i, name="mla_down")
            cq, ckvkr = _mla_norm(proj, mla_q_norm[j][None, :], mla_kv_norm[j][None, :])
            st_mla_ckv.append(ckvkr[:N_PROMPT, :MLA_KV_RANK])
            st_mla_kr.append(ckvkr[:N_PROMPT, MLA_KV_RANK:MLA_KV_RANK + MLA_ROPE])
            q_scale = dict(q_cols=MLA_HEADS * LANES, q_scale=(MLA_NOPE + MLA_ROPE) ** -0.5, out_dtype=BF16)
            q_p = _project(cq, wq, n_tiles=PROMPT_TILES, name="mla_q_context", **q_scale)
            k_p = _project(ckvkr, wk, n_tiles=PROMPT_TILES, out_dtype=BF16, name="mla_k_context")
            q_s = _project(cq, wq, tile_off=PROMPT_TILES, n_tiles=SAMPLE_TILES, rope=rope, name="mla_q_latent",
                           **q_scale)
            k_s = _project(ckvkr, wk, tile_off=PROMPT_TILES, n_tiles=SAMPLE_TILES, rope=rope, out_dtype=BF16,
                           name="mla_k_latent")
            v_all = _project(ckvkr, wv, n_tiles=all_tiles, out_dtype=BF16, name="mla_v")
            ctx_in = jnp.concatenate(
                [cache_mla_ckv[:, j], cache_mla_kr[:, j],
                 jnp.zeros((DEC_BATCH, PAST_LEN, 2 * MLA_KV_RANK - MLA_KV_RANK - MLA_ROPE), F32)],
                axis=-1).reshape(DEC_BATCH * PAST_LEN, 2 * MLA_KV_RANK)
            n_ctx_tiles = DEC_BATCH * PAST_LEN // TM
            k_c = _project(ctx_in, wk, n_tiles=n_ctx_tiles, out_dtype=BF16, name="mla_k_cache")
            v_c = _project(ctx_in, wv, n_tiles=n_ctx_tiles, out_dtype=BF16, name="mla_v_cache")
            k_c = k_c.reshape(DEC_BATCH, 1, PAST_LEN, MLA_HEADS * LANES)
            v_c = v_c.reshape(DEC_BATCH, 1, PAST_LEN, D_MODEL)
            o_p = _attention(q_p, k_p, v_all, mode="mla", n_batch=BATCH, seq=SEQ, tq=SEQ, q_col=0, k_col=0,
                             v_col=0, hb=nb, name="mla_context_attention")
            o_s = _attention(q_s, k_s, v_all, mode="mla", n_batch=DEC_BATCH, seq=DEC_SEQ, tq=LATENT_TQ,q_col=0, k_col=0,
                             v_col=0, v_row_off=dec_off, ctx=(k_c, v_c, 0), hb=2, name="mla_latent_attention")
            w_o = mla_w_o[j]

        x = _out_proj_ln(o_p, o_s, w_o.astype(BF16), x, mods, i, ln_g[i, 0][None, :], ln_b[i, 0][None, :])
        x = _moe(x, mods, i, router_w, router_b, moe_w1, moe_b1, moe_w2, moe_b2,
                 ln_g[i, 1][None, :], ln_b[i, 1][None, :], split_out=i == DEPTH - 1)

    y_prompt = x[0].reshape(BATCH, SEQ, D_MODEL)
    y_sample = x[1].reshape(DEC_BATCH, DEC_SEQ, D_MODEL)

    def stack(parts, *tail):
        return jnp.stack([p.reshape(BATCH, SEQ, *tail) for p in parts], axis=1)

    return (y_prompt, y_sample,
            stack(st_na_k, NA_HEADS, NA_HEAD_DIM), stack(st_na_v, NA_HEADS, NA_HEAD_DIM),
            stack(st_diff_k, DIFF_HEADS, 2 * DIFF_HEAD_DIM), stack(st_diff_v, DIFF_HEADS, 2 * DIFF_HEAD_DIM),
            stack(st_mla_ckv, MLA_KV_RANK), stack(st_mla_kr, MLA_ROPE))
```

```python
import functools
import math

import jax
import jax.numpy as jnp
from jax import lax
from jax.experimental import pallas as pl
from jax.experimental.pallas import tpu as pltpu

F32 = jnp.float32
BF16 = jnp.bfloat16
HIGHEST = lax.Precision.HIGHEST

D_MODEL = 1024
BATCH = 32
SEQ = 256
DEPTH = 4
DEC_BATCH = 4
DEC_SEQ = 2048
PAST_LEN = 512
GRID_W = 64
N_MIXERS = 3
NA_HEADS = 16
NA_HEAD_DIM = 64
WIN_R = 8
WIN_C = 16
DIFF_HEADS = 8
DIFF_HEAD_DIM = 64
MLA_HEADS = 16
MLA_Q_RANK = 256
MLA_KV_RANK = 128
MLA_NOPE = 64
MLA_ROPE = 32
MLA_V = 64
N_EXPERTS = 32
TOP_K = 4
D_EXPERT = 1024
SWIGLU_LIMIT = 7.0
SWIGLU_ALPHA = 1.702
ROPE_BASE = 10000.0
LN_EPS = 1e-5
RMS_EPS = 1e-6
NEG_INF = -1e30
DEEPNORM_ALPHA = (2 * DEPTH) ** 0.25

LANES = 128
TM = 256
N_PROMPT = BATCH * SEQ
N_SAMPLE = DEC_BATCH * DEC_SEQ
N_TOK = N_PROMPT + N_SAMPLE
PROMPT_TILES = N_PROMPT // TM
SAMPLE_TILES = N_SAMPLE // TM
TILES_PER_DEC = DEC_SEQ // TM
N_MOD_ROWS = 8
LATENT_TQ = 512
NA_QROWS = 4
NA_KROWS = NA_QROWS + WIN_R - 1
NA_BIAS_PAD = NA_QROWS - 1
NA_BIAS_ROWS = WIN_R - 1 + NA_BIAS_PAD + NA_KROWS
N_CHUNKS = N_TOK // TM
MOE_PIECE = 8
MAX_CHUNK_PIECES = (TM * TOP_K + N_EXPERTS * (MOE_PIECE - 1)) // MOE_PIECE
MOE_CHUNK_PIECES = 160
MOE_CHUNK_ROWS = MOE_CHUNK_PIECES * MOE_PIECE
EXP_TM = 512
EXP_PPT = EXP_TM // MOE_PIECE
MAX_TILES = N_CHUNKS * MAX_CHUNK_PIECES // EXP_PPT + N_EXPERTS
VMEM_LIMIT = 56 * 1024 * 1024


def _cparams(n_axes, vmem=VMEM_LIMIT):
    return pltpu.CompilerParams(dimension_semantics=("arbitrary",) * n_axes, vmem_limit_bytes=vmem)


def _mod_row(t):
    return jnp.where(t < PROMPT_TILES, 0, 1 + (t - PROMPT_TILES) // TILES_PER_DEC)


def _mod_spec(layer, chunk, tile_off, tile_axis):
    def index(*g):
        row = _mod_row(g[tile_axis] + tile_off)
        return ((layer * N_MOD_ROWS + row) * 6 + chunk, 0, 0)
    return pl.BlockSpec((None, 1, D_MODEL), index)


def _sigmoid(x):
    return 1.0 / (1.0 + jnp.exp(-x))


def _layernorm(z, g, b):
    mu = jnp.mean(z, axis=-1, keepdims=True)
    zc = z - mu
    var = jnp.mean(zc * zc, axis=-1, keepdims=True)
    return zc * lax.rsqrt(var + LN_EPS) * g + b


def _mod_kernel(cond_ref, w_ref, b_ref, o_ref):
    cnd = cond_ref[...]
    s = cnd * _sigmoid(cnd)
    o_ref[...] = jnp.dot(s, w_ref[...], precision=HIGHEST, preferred_element_type=F32) + b_ref[...]


def _modulation(cond, ada_w, ada_b):
    out = pl.pallas_call(
        _mod_kernel,
        grid=(DEPTH, 6),
        in_specs=[
            pl.BlockSpec((N_MOD_ROWS, D_MODEL), lambda l, j: (0, 0)),
            pl.BlockSpec((None, D_MODEL, D_MODEL), lambda l, j: (l, 0, j)),
            pl.BlockSpec((None, 1, D_MODEL), lambda l, j: (l, 0, j)),
        ],
        out_specs=pl.BlockSpec((None, N_MOD_ROWS, D_MODEL), lambda l, j: (l, 0, j)),
        out_shape=jax.ShapeDtypeStruct((DEPTH, N_MOD_ROWS, 6 * D_MODEL), F32),
        compiler_params=_cparams(2),
        name="adaln_modulation",
    )(cond, ada_w, ada_b.reshape(DEPTH, 1, 6 * D_MODEL))
    return out.reshape(DEPTH * N_MOD_ROWS * 6, 1, D_MODEL)


def _rope_swap(y, g):
    lane = lax.broadcasted_iota(jnp.int32, y.shape, 1)
    first = (lane % (2 * g)) < g
    return jnp.where(first, pltpu.roll(y, LANES - g, 1), pltpu.roll(y, g, 1))


def _proj_kernel(*refs, use_mod, rope_g, q_tiles, q_scale):
    it = iter(refs)
    x_ref = next(it)
    x = x_ref[...]
    if use_mod:
        shift_ref, scale_ref = next(it), next(it)
        x = x * (1.0 + scale_ref[...]) + shift_ref[...]
    w_ref = next(it)
    y = jnp.dot(x.astype(BF16), w_ref[...], preferred_element_type=F32)
    col_scale = jnp.where(pl.program_id(0) < q_tiles, q_scale, 1.0) if q_tiles else None
    if rope_g:
        cos_ref, sin_ref = next(it), next(it)
        o_ref = next(it)
        cos, sin = cos_ref[...], sin_ref[...]
        for j in range(y.shape[1] // LANES):
            yb = y[:, j * LANES:(j + 1) * LANES]
            yb = yb * cos + _rope_swap(yb, rope_g) * sin
            if q_tiles:
                yb = yb * col_scale
            o_ref[:, j * LANES:(j + 1) * LANES] = yb.astype(o_ref.dtype)
    else:
        o_ref = next(it)
        if q_tiles:
            y = y * col_scale
        o_ref[...] = y.astype(o_ref.dtype)


def _project(x, w_bf16, *, tile_off=0, n_tiles, mods=None, layer=0, chunks=(0, 1), rope=None, q_cols=0,
             q_scale=1.0, out_dtype=F32, name):
    k_dim, n = w_bf16.shape
    tn = min(n, 1024)
    assert q_cols % tn == 0
    grid = (n // tn, n_tiles)
    in_specs = [pl.BlockSpec((TM, k_dim), lambda j, i: (i + tile_off, 0))]
    args = [x]
    if mods is not None:
        in_specs += [_mod_spec(layer, chunks[0], tile_off, 1), _mod_spec(layer, chunks[1], tile_off, 1)]
        args += [mods, mods]
    in_specs.append(pl.BlockSpec((k_dim, tn), lambda j, i: (0, j)))
    args.append(w_bf16)
    rope_g = 0
    if rope is not None:
        rope_g, cos, sin = rope
        tab = pl.BlockSpec((TM, LANES), lambda j, i: (i % TILES_PER_DEC, 0))
        in_specs += [tab, tab]
        args += [cos, sin]
    return pl.pallas_call(
        functools.partial(_proj_kernel, use_mod=mods is not None, rope_g=rope_g, q_tiles=q_cols // tn,
                          q_scale=q_scale),
        grid=grid,
        in_specs=in_specs,
        out_specs=pl.BlockSpec((TM, tn), lambda j, i: (i, j)),
        out_shape=jax.ShapeDtypeStruct((n_tiles * TM, n), out_dtype),
        compiler_params=_cparams(2),
        name=name,
    )(*args)


def _context_qkv(x, w_bf16, mods, layer, qk_scale, tag):
    common = dict(n_tiles=PROMPT_TILES, mods=mods, layer=layer)
    q = _project(x, w_bf16[:, :D_MODEL], name=tag + "_q_context", **common, **qk_scale)
    k = _project(x, w_bf16[:, D_MODEL:2 * D_MODEL], name=tag + "_k_context", **common)
    v = _project(x, w_bf16[:, 2 * D_MODEL:], name=tag + "_v_context", **common)
    return q, k, v


def _axial_tables(group_lanes, nf, lane_lo, lane_hi):
    del group_lanes
    t = jnp.arange(DEC_SEQ)
    lane = jnp.arange(LANES)
    rel = lane - lane_lo
    active = (lane >= lane_lo) & (lane < lane_hi)
    grp = rel // (2 * nf)
    pos = jnp.where((grp % 2 == 0)[None, :], (t // GRID_W)[:, None], (t % GRID_W)[:, None])
    inv = ROPE_BASE ** (-jnp.arange(nf, dtype=F32) / nf)
    ang = pos.astype(F32) * inv[rel % nf][None, :]
    cos = jnp.cos(ang)
    sin = jnp.sin(ang)
    sign = jnp.where((rel % (2 * nf)) < nf, -1.0, 1.0)[None, :]
    cos = jnp.where(active[None, :], cos, 1.0).astype(F32)
    sin = jnp.where(active[None, :], sin * sign, 0.0).astype(F32)
    return cos, sin


def _nt_dot(a, b):
    return lax.dot_general(a, b, (((1,), (1,)), ((), ())), preferred_element_type=F32)


def _half_values(vs, half):
    lane = lax.broadcasted_iota(jnp.int32, (1, LANES), 1)
    keep = jnp.where((lane < 64) if half == 0 else (lane >= 64), 1.0, 0.0).astype(BF16)
    return [v * keep + (1 - keep) for v in vs]


def _pair_softmax_pv(scores_a, scores_b, vs):
    outs = []
    for half, scores in enumerate((scores_a, scores_b)):
        m = scores[0].max(axis=-1, keepdims=True)
        for s in scores[1:]:
            m = jnp.maximum(m, s.max(axis=-1, keepdims=True))
        acc = None
        for s, v in zip(scores, _half_values(vs, half)):
            o = jnp.dot(jnp.exp(s - m).astype(BF16), v, preferred_element_type=F32)
            acc = o if acc is None else acc + o
        outs.append(acc / pltpu.roll(acc, 64, 1))
    lane = lax.broadcasted_iota(jnp.int32, outs[0].shape, 1)
    return jnp.where(lane < 64, outs[0], outs[1])


def _softmax_pv(scores, vs):
    m = scores[0].max(axis=-1, keepdims=True)
    for s in scores[1:]:
        m = jnp.maximum(m, s.max(axis=-1, keepdims=True))
    acc, den = None, None
    for s, v in zip(scores, vs):
        e = jnp.exp(s - m)
        d = e.sum(axis=-1, keepdims=True)
        o = jnp.dot(e.astype(BF16), v, preferred_element_type=F32)
        acc = o if acc is None else acc + o
        den = d if den is None else den + d
    return acc / den


def _attn_kernel(*refs, mode, has_ctx, lam_init, hb):
    it = iter(refs)
    q_ref, k_ref, v_ref = next(it), next(it), next(it)
    kc_ref = vc_ref = None
    if has_ctx:
        kc_ref, vc_ref = next(it), next(it)
    if mode == "diff":
        lam_ref, subln_ref = next(it), next(it)
    o_ref = next(it)

    wq = 2 * LANES if mode == "mla" else LANES
    if mode == "diff":
        lp = lam_ref[...]
        lam = (jnp.exp(jnp.sum(lp[0:1] * lp[1:2], axis=-1, keepdims=True))
               - jnp.exp(jnp.sum(lp[2:3] * lp[3:4], axis=-1, keepdims=True)) + lam_init)

    for hblk in range(hb):
        qk_cols = slice(hblk * wq, (hblk + 1) * wq)
        v_cols = slice(hblk * LANES, (hblk + 1) * LANES)
        q = q_ref[:, qk_cols].astype(BF16)
        ks = [k_ref[:, qk_cols].astype(BF16)]
        vs = [v_ref[:, v_cols].astype(BF16)]
        if has_ctx:
            ks.append(kc_ref[:, qk_cols].astype(BF16))
            vs.append(vc_ref[:, v_cols].astype(BF16))

        if mode == "mla":
            halves = [(q[:, :LANES], [k[:, :LANES] for k in ks]), (q[:, LANES:], [k[:, LANES:] for k in ks])]
        else:
            lane_q = lax.broadcasted_iota(jnp.int32, q.shape, 1)
            zero = jnp.zeros_like(q)
            halves = [(jnp.where(lane_q < 64, q, zero), ks), (jnp.where(lane_q >= 64, q, zero), ks)]

        scores = [[_nt_dot(qh, k) for k in kh] for qh, kh in halves]
        if mode == "diff":
            o = _softmax_pv(scores[0], vs) - lam * _softmax_pv(scores[1], vs)
            o = o * lax.rsqrt(jnp.mean(o * o, axis=-1, keepdims=True) + RMS_EPS) * subln_ref[...]
            o_ref[:, v_cols] = (o * (1.0 - lam_init)).astype(o_ref.dtype)
        else:
            o_ref[:, v_cols] = _pair_softmax_pv(scores[0], scores[1], vs).astype(o_ref.dtype)


def _attention(q_arr, k_arr, v_arr, *, mode, n_batch, seq, tq, q_col, k_col, v_col, row_off=0, v_row_off=None,
               ctx=None, lam=None, subln=None, lam_init=0.0, hb=1, name):
    assert mode != "mla" or (q_col == 0 and k_col == 0)
    n_blocks = D_MODEL // LANES
    assert n_blocks % hb == 0 and q_col % hb == 0 and k_col % hb == 0 and v_col % hb == 0
    wq = 2 * LANES if mode == "mla" else LANES
    wqb, wvb = wq * hb, LANES * hb
    nqb = seq // tq
    qoff = row_off * nqb
    if v_row_off is None:
        v_row_off = row_off
    in_specs = [
        pl.BlockSpec((tq, wqb), lambda b, h, i: (qoff + b * nqb + i, q_col // hb + h)),
        pl.BlockSpec((seq, wqb), lambda b, h, i: (row_off + b, k_col // hb + h)),
        pl.BlockSpec((seq, wvb), lambda b, h, i: (v_row_off + b, v_col // hb + h)),
    ]
    args = [q_arr, k_arr, v_arr]
    if ctx is not None:
        kc, vc, j = ctx
        in_specs += [
            pl.BlockSpec((None, None, PAST_LEN, wqb), lambda b, h, i: (b, j, 0, h)),
            pl.BlockSpec((None, None, PAST_LEN, wvb), lambda b, h, i: (b, j, 0, h)),
        ]
        args += [kc, vc]
    if mode == "diff":
        in_specs += [pl.BlockSpec((4, DIFF_HEAD_DIM), lambda b, h, i: (0, 0)),
                     pl.BlockSpec((1, LANES), lambda b, h, i: (0, 0))]
        args += [lam, subln]
    return pl.pallas_call(
        functools.partial(_attn_kernel, mode=mode, has_ctx=ctx is not None, lam_init=lam_init, hb=hb),
        grid=(n_batch, n_blocks // hb, nqb),
        in_specs=in_specs,
        out_specs=pl.BlockSpec((tq, wvb), lambda b, h, i: (b * nqb + i, h)),
        out_shape=jax.ShapeDtypeStruct((n_batch * seq, D_MODEL), BF16),
        compiler_params=_cparams(3),
        name=name,
    )(*args)


def _na_bias_tables(rpb):
    cols = jnp.arange(GRID_W)
    c0 = jnp.clip(cols - WIN_C // 2, 0, GRID_W - WIN_C)
    col_ok = (cols[None, :] >= c0[:, None]) & (cols[None, :] < c0[:, None] + WIN_C)
    col_off = jnp.clip(cols[None, :] - cols[:, None] + (WIN_C - 1), 0, 2 * WIN_C - 2)
    hot_c = (col_off[:, :, None] == jnp.arange(2 * WIN_C - 1)).astype(F32)
    core = jnp.einsum("hab,ckb->hcak", rpb, hot_c, precision=HIGHEST)
    core = jnp.where(col_ok[None, :, None, :], core, NEG_INF)
    pad_hi = NA_BIAS_ROWS - NA_BIAS_PAD - (2 * WIN_R - 1)
    tab = jnp.pad(core, ((0, 0), (0, 0), (NA_BIAS_PAD, pad_hi), (0, 0)), constant_values=NEG_INF)
    tab = tab.reshape(NA_HEADS, GRID_W, NA_BIAS_ROWS * GRID_W)
    shifted = jnp.pad(tab[:, :, GRID_W:], ((0, 0), (0, 0), (0, GRID_W)), constant_values=NEG_INF)
    return jnp.stack([tab, shifted]).astype(F32)


def _na_block_bias(bias_ref, half, kind):
    nk = NA_KROWS * GRID_W
    lane = lax.broadcasted_iota(jnp.int32, (GRID_W, nk), 1)
    slabs = []
    for i in range(NA_QROWS):
        if kind == 0:
            ext0, lo, hi = WIN_R - 1 + NA_BIAS_PAD - i, 0, WIN_R
        elif kind == 1:
            ext0, lo, hi = WIN_R - 1 - WIN_R // 2 + NA_BIAS_PAD - i, i, i + WIN_R
        else:
            ext0, lo, hi = NA_BIAS_PAD - i, NA_KROWS - WIN_R, NA_KROWS
        start = (ext0 - ext0 % 2) * GRID_W
        slab = bias_ref[ext0 % 2, half, :, start:start + nk]
        slabs.append(jnp.where((lane >= lo * GRID_W) & (lane < hi * GRID_W), slab, NEG_INF))
    return jnp.concatenate(slabs, axis=0)


def _na_latent_kernel(q_ref, k_ref, v_ref, kc_ref, vc_ref, bias_ref, o_ref):
    rows_n = DEC_SEQ // GRID_W
    kc = kc_ref[...].astype(BF16)
    vc = vc_ref[...].astype(BF16)
    tq = NA_QROWS * GRID_W
    nk = NA_KROWS * GRID_W
    lane = lax.broadcasted_iota(jnp.int32, (tq, LANES), 1)
    biases = {}
    for blk in range(rows_n // NA_QROWS):
        q_row0 = blk * NA_QROWS
        if blk == 0:
            kind, k_row0 = 0, 0
        elif blk == rows_n // NA_QROWS - 1:
            kind, k_row0 = 2, rows_n - NA_KROWS
        else:
            kind, k_row0 = 1, q_row0 - WIN_R // 2
        q = q_ref[q_row0 * GRID_W:q_row0 * GRID_W + tq, :].astype(BF16)
        k = k_ref[k_row0 * GRID_W:k_row0 * GRID_W + nk, :].astype(BF16)
        v = v_ref[k_row0 * GRID_W:k_row0 * GRID_W + nk, :].astype(BF16)
        zero = jnp.zeros_like(q)
        scores = []
        for half in range(2):
            qh = jnp.where((lane < 64) if half == 0 else (lane >= 64), q, zero)
            if (kind, half) not in biases:
                biases[kind, half] = _na_block_bias(bias_ref, half, kind)
            scores.append([_nt_dot(qh, k) + biases[kind, half], _nt_dot(qh, kc)])
        o_ref[q_row0 * GRID_W:q_row0 * GRID_W + tq, :] = _pair_softmax_pv(
            scores[0], scores[1], [v, vc]).astype(o_ref.dtype)


def _na_latent_attention(qkv, cache_k, cache_v, j, bias):
    nb = D_MODEL // LANES
    return pl.pallas_call(
        _na_latent_kernel,
        grid=(nb, DEC_BATCH),
        in_specs=[
            pl.BlockSpec((DEC_SEQ, LANES), lambda h, b: (b, h)),
            pl.BlockSpec((DEC_SEQ, LANES), lambda h, b: (b, nb + h)),
            pl.BlockSpec((DEC_SEQ, LANES), lambda h, b: (b, 2 * nb + h)),
            pl.BlockSpec((None, None, PAST_LEN, LANES), lambda h, b: (b, j, 0, h)),
            pl.BlockSpec((None, None, PAST_LEN, LANES), lambda h, b: (b, j, 0, h)),
            pl.BlockSpec((2, 2, GRID_W, NA_BIAS_ROWS * GRID_W), lambda h, b: (0, h, 0, 0)),
        ],
        out_specs=pl.BlockSpec((DEC_SEQ, LANES), lambda h, b: (b, h)),
        out_shape=jax.ShapeDtypeStruct((N_SAMPLE, D_MODEL), BF16),
        compiler_params=_cparams(2),
        name="na_latent_attention",
    )(qkv, qkv, qkv, cache_k, cache_v, bias)


def _out_route_kernel(op_ref, os_ref, w_ref, x_ref, gate_ref, g_ref, b_ref, shift_ref, scale_ref, rw_ref, rb_ref,
                      xo_ref, xs_ref, meta_ref, cnt_ref, o_sc):
    i = pl.program_id(0)

    @pl.when(i < PROMPT_TILES)
    def _():
        o_sc[...] = op_ref[...].astype(BF16)

    @pl.when((i >= PROMPT_TILES) & (i < N_CHUNKS))
    def _():
        o_sc[...] = os_ref[...].astype(BF16)

    @pl.when(i < N_CHUNKS)
    def _():
        y = jnp.dot(o_sc[...], w_ref[...], preferred_element_type=F32)
        z = DEEPNORM_ALPHA * x_ref[...] + gate_ref[...] * y
        x_new = _layernorm(z, g_ref[...], b_ref[...])
        xo_ref[...] = x_new
        _route_chunk(x_new, shift_ref, scale_ref, rw_ref, rb_ref, xs_ref, meta_ref, cnt_ref)

    @pl.when(i == N_CHUNKS)
    def _():
        xs_ref[...] = jnp.zeros_like(xs_ref)


def _route_chunk(x, shift_ref, scale_ref, rw_ref, rb_ref, xs_ref, meta_ref, cnt_ref):
    h = x * (1.0 + scale_ref[...]) + shift_ref[...]
    h_hi = h.astype(BF16)
    h_lo = (h - h_hi.astype(F32)).astype(BF16)
    logits = (jnp.dot(h_hi, rw_ref[0], preferred_element_type=F32)
              + jnp.dot(h_lo, rw_ref[0], preferred_element_type=F32)
              + jnp.dot(h_hi, rw_ref[1], preferred_element_type=F32)) + rb_ref[...]
    lane = lax.broadcasted_iota(jnp.int32, logits.shape, 1)
    work = logits
    top_v, hots = [], []
    for _ in range(TOP_K):
        m = work.max(axis=-1, keepdims=True)
        first = jnp.where(work == m, lane, N_EXPERTS).min(axis=-1, keepdims=True)
        hot = lane == first
        top_v.append(m)
        hots.append(jnp.where(hot, 1.0, 0.0))
        work = jnp.where(hot, -jnp.inf, work)
    es = [jnp.exp(v - top_v[0]) for v in top_v]
    den = es[0] + es[1] + es[2] + es[3]
    gates = [e / den for e in es]

    cnts = [hf.sum(axis=0, keepdims=True) for hf in hots]
    n_e = cnts[0] + cnts[1] + cnts[2] + cnts[3]
    pieces = jnp.floor((n_e + (MOE_PIECE - 1)) * (1.0 / MOE_PIECE))
    er = lax.broadcasted_iota(jnp.int32, (N_EXPERTS, N_EXPERTS), 0)
    ec = lax.broadcasted_iota(jnp.int32, (N_EXPERTS, N_EXPERTS), 1)
    upper = jnp.where(er < ec, 1.0, 0.0).astype(BF16)
    poff = jnp.dot(jnp.broadcast_to(pieces, (8, N_EXPERTS)).astype(BF16), upper,
                   preferred_element_type=F32)[0:1]

    tr = lax.broadcasted_iota(jnp.int32, (TM, TM), 0)
    tc = lax.broadcasted_iota(jnp.int32, (TM, TM), 1)
    lower = jnp.where(tc < tr, 1.0, 0.0).astype(BF16)
    run = poff * MOE_PIECE
    slots = []
    for k in range(TOP_K):
        prefix = jnp.dot(lower, hots[k].astype(BF16), preferred_element_type=F32)
        slots.append(jnp.sum(hots[k] * (run + prefix), axis=-1, keepdims=True))
        run = run + cnts[k]

    lane_w = lax.broadcasted_iota(jnp.int32, (TM, LANES), 1)
    meta = jnp.full((TM, LANES), -1.0, F32)
    for k in range(TOP_K):
        meta = jnp.where(lane_w == k, slots[k], meta)
        meta = jnp.where(lane_w == TOP_K + k, gates[k], meta)
    meta_ref[...] = meta[:, :2 * TOP_K]
    cnt_ref[...] = pieces

    slot_rows = meta.T
    srow = lax.broadcasted_iota(jnp.int32, (MOE_CHUNK_ROWS, TM), 0).astype(F32)
    p = jnp.zeros((MOE_CHUNK_ROWS, TM), F32)
    for k in range(TOP_K):
        p = p + jnp.where(srow == slot_rows[k:k + 1, :], 1.0, 0.0)
    xs_ref[...] = jnp.dot(p.astype(BF16), h_hi, preferred_element_type=F32)


def _out_route(o_p, o_s, w_bf16, x, mods, layer, ln_g, ln_b, rw, rb):
    assert 2 * EXP_TM <= MOE_CHUNK_ROWS
    k_dim = w_bf16.shape[0]
    last = N_CHUNKS - 1
    rw_hi = rw.astype(BF16)
    rw_split = jnp.stack([rw_hi, (rw - rw_hi.astype(F32)).astype(BF16)], axis=1)
    row = pl.BlockSpec((1, D_MODEL), lambda i: (0, 0))
    return pl.pallas_call(
        _out_route_kernel,
        grid=(N_CHUNKS + 1,),
        in_specs=[
            pl.BlockSpec((TM, k_dim), lambda i: (jnp.minimum(i, PROMPT_TILES - 1), 0)),
            pl.BlockSpec((TM, k_dim), lambda i: (jnp.clip(i - PROMPT_TILES, 0, SAMPLE_TILES - 1), 0)),
            pl.BlockSpec((k_dim, D_MODEL), lambda i: (0, 0)),
            pl.BlockSpec((TM, D_MODEL), lambda i: (jnp.minimum(i, last), 0)),
            _mod_spec(layer, 2, 0, 0),
            row, row,
            _mod_spec(layer, 3, 0, 0),
            _mod_spec(layer, 4, 0, 0),
            pl.BlockSpec((None, 2, D_MODEL, N_EXPERTS), lambda i: (layer, 0, 0, 0)),
            pl.BlockSpec((None, 1, N_EXPERTS), lambda i: (layer, 0, 0)),
        ],
        out_specs=[
            pl.BlockSpec((TM, D_MODEL), lambda i: (jnp.minimum(i, last), 0)),
            pl.BlockSpec((MOE_CHUNK_ROWS, D_MODEL), lambda i: (i, 0)),
            pl.BlockSpec((TM, 2 * TOP_K), lambda i: (jnp.minimum(i, last), 0)),
            pl.BlockSpec((None, 1, N_EXPERTS), lambda i: (jnp.minimum(i, last), 0, 0)),
        ],
        out_shape=[
            jax.ShapeDtypeStruct((N_TOK, D_MODEL), F32),
            jax.ShapeDtypeStruct(((N_CHUNKS + 1) * MOE_CHUNK_ROWS, D_MODEL), F32),
            jax.ShapeDtypeStruct((N_TOK, 2 * TOP_K), F32),
            jax.ShapeDtypeStruct((N_CHUNKS, 1, N_EXPERTS), F32),
        ],
        scratch_shapes=[pltpu.VMEM((TM, k_dim), BF16)],
        compiler_params=_cparams(1),
        name="out_proj_layernorm_route",
    )(o_p, o_s, w_bf16, x, mods, ln_g, ln_b, mods, mods, rw_split, rb.reshape(DEPTH, 1, N_EXPERTS))


def _moe_schedule(pieces):
    e_ids = jnp.arange(N_EXPERTS, dtype=jnp.int32)
    npe = pieces.sum(axis=0)
    tiles_e = (npe + EXP_PPT - 1) // EXP_PPT
    tile_end = jnp.cumsum(tiles_e)
    tile_start = tile_end - tiles_e
    g = jnp.arange(MAX_TILES, dtype=jnp.int32)
    last_e = jnp.max(jnp.where(tiles_e > 0, e_ids, 0))
    te = jnp.minimum(jnp.sum(tile_end[None, :] <= g[:, None], axis=1), last_e).astype(jnp.int32)
    seg_len = pieces.T
    seg_base = tile_start[:, None] * EXP_PPT + jnp.cumsum(seg_len, axis=1) - seg_len
    poff = jnp.cumsum(pieces, axis=1) - pieces
    seg_pos = (jnp.arange(N_CHUNKS, dtype=jnp.int32)[:, None] * MOE_CHUNK_PIECES + poff).T
    hot = (te[:, None] == e_ids[None, :])[:, :, None]
    base, length, first = (jnp.sum(jnp.where(hot, tab[None], 0), axis=1)
                           for tab in (seg_base, seg_len, seg_pos))
    lane = jnp.arange(EXP_PPT, dtype=jnp.int32)[None, :]
    pos = (g[:, None] * EXP_PPT + lane)[:, :, None]
    hit = (base[:, None, :] <= pos) & (pos < (base + length)[:, None, :])
    src = jnp.sum(jnp.where(hit, first[:, None, :] + pos - base[:, None, :], 0), axis=-1)
    valid = jnp.any(hit, axis=-1)
    scratch = N_CHUNKS * MOE_CHUNK_PIECES + (g[:, None] % 2) * EXP_PPT + lane
    dst = jnp.where(valid, src, scratch).reshape(-1).astype(jnp.int32)
    src = jnp.where(valid, src, src[:, :1]).reshape(-1).astype(jnp.int32)
    return te, tile_end[-1:].astype(jnp.int32), src, dst


def _expert_kernel(te_ref, nt_ref, src_ref, dst_ref, xs_hbm, w1_ref, b1_ref, w2_ref, b2_ref, out_hbm,
                   xbuf, ybuf, w1s, w2s, sem_in, sem_out):
    g = pl.program_id(0)
    nt = nt_ref[0]
    slot = g % 2

    def in_copy(tile, i, s):
        row = pl.multiple_of(src_ref[tile * EXP_PPT + i] * MOE_PIECE, MOE_PIECE)
        return pltpu.make_async_copy(xs_hbm.at[pl.ds(row, MOE_PIECE), :],
                                     xbuf.at[s, pl.ds(i * MOE_PIECE, MOE_PIECE), :], sem_in.at[s])

    def out_copy(tile, i, s):
        row = pl.multiple_of(dst_ref[tile * EXP_PPT + i] * MOE_PIECE, MOE_PIECE)
        return pltpu.make_async_copy(ybuf.at[s, pl.ds(i * MOE_PIECE, MOE_PIECE), :],
                                     out_hbm.at[pl.ds(row, MOE_PIECE), :], sem_out.at[s])

    def gather(tile, s, wait):
        for i in range(EXP_PPT):
            cp = in_copy(tile, i, s)
            cp.wait() if wait else cp.start()

    def scatter(tile, s, wait):
        for i in range(EXP_PPT):
            cp = out_copy(tile, i, s)
            cp.wait() if wait else cp.start()

    @pl.when(g < nt)
    def _():
        @pl.when(g == 0)
        def _():
            gather(0, 0, False)

        @pl.when(g + 1 < nt)
        def _():
            gather(g + 1, 1 - slot, False)

        gather(g, slot, True)

        @pl.when(g >= 2)
        def _():
            scatter(g - 2, slot, True)

        @pl.when((g == 0) | (te_ref[g] != te_ref[jnp.maximum(g - 1, 0)]))
        def _():
            w1s[...] = w1_ref[...].astype(BF16)
            w2s[...] = w2_ref[...].astype(BF16)

        gu = jnp.dot(xbuf[slot].astype(BF16), w1s[...], preferred_element_type=F32) + b1_ref[...]
        gl = jnp.minimum(gu[:, :D_EXPERT], SWIGLU_LIMIT)
        ul = jnp.clip(gu[:, D_EXPERT:], -SWIGLU_LIMIT, SWIGLU_LIMIT)
        act = (ul + 1.0) * (gl * _sigmoid(SWIGLU_ALPHA * gl))
        ybuf[slot] = jnp.dot(act.astype(BF16), w2s[...], preferred_element_type=F32) + b2_ref[...]
        scatter(g, slot, False)

        @pl.when(g == nt - 1)
        def _():
            @pl.when(g >= 1)
            def _():
                scatter(g - 1, 1 - slot, True)
            scatter(g, slot, True)


def _experts(xs, te, nt, src, dst, layer, w1, b1, w2, b2):
    def expert_block(g, te, nt, src, dst):
        return (layer, te[g], 0, 0)

    grid_spec = pltpu.PrefetchScalarGridSpec(
        num_scalar_prefetch=4,
        grid=(MAX_TILES,),
        in_specs=[
            pl.BlockSpec(memory_space=pl.ANY),
            pl.BlockSpec((None, None, D_MODEL, 2 * D_EXPERT), expert_block),
            pl.BlockSpec((None, None, 1, 2 * D_EXPERT), expert_block),
            pl.BlockSpec((None, None, D_EXPERT, D_MODEL), expert_block),
            pl.BlockSpec((None, None, 1, D_MODEL), expert_block),
        ],
        out_specs=pl.BlockSpec(memory_space=pl.ANY),
        scratch_shapes=[
            pltpu.VMEM((2, EXP_TM, D_MODEL), F32),
            pltpu.VMEM((2, EXP_TM, D_MODEL), F32),
            pltpu.VMEM((D_MODEL, 2 * D_EXPERT), BF16),
            pltpu.VMEM((D_EXPERT, D_MODEL), BF16),
            pltpu.SemaphoreType.DMA((2,)),
            pltpu.SemaphoreType.DMA((2,)),
        ],
    )
    return pl.pallas_call(
        _expert_kernel,
        grid_spec=grid_spec,
        out_shape=jax.ShapeDtypeStruct(xs.shape, F32),
        input_output_aliases={4: 0},
        compiler_params=_cparams(1),
        name="moe_experts",
    )(te, nt, src, dst, xs, w1, b1.reshape(DEPTH, N_EXPERTS, 1, 2 * D_EXPERT),
      w2, b2.reshape(DEPTH, N_EXPERTS, 1, D_MODEL))


def _combine_kernel(ys_ref, meta_ref, x_ref, gate_ref, g_ref, b_ref, xo_ref):
    meta = meta_ref[...]
    col = lax.broadcasted_iota(jnp.int32, (TM, MOE_CHUNK_ROWS), 1).astype(F32)
    gmat = jnp.zeros((TM, MOE_CHUNK_ROWS), F32)
    for k in range(TOP_K):
        gmat = gmat + jnp.where(col == meta[:, k:k + 1], meta[:, TOP_K + k:TOP_K + k + 1], 0.0)
    y = jnp.dot(gmat.astype(BF16), ys_ref[...].astype(BF16), preferred_element_type=F32)
    z = DEEPNORM_ALPHA * x_ref[...] + gate_ref[...] * y
    xo_ref[...] = _layernorm(z, g_ref[...], b_ref[...])


def _combine_ln(ys, meta, x, mods, layer, ln_g, ln_b, chunk_off=0, n_chunks=N_CHUNKS):
    row = pl.BlockSpec((1, D_MODEL), lambda i: (0, 0))
    return pl.pallas_call(
        _combine_kernel,
        grid=(n_chunks,),
        in_specs=[
            pl.BlockSpec((MOE_CHUNK_ROWS, D_MODEL), lambda i: (i + chunk_off, 0)),
            pl.BlockSpec((TM, 2 * TOP_K), lambda i: (i + chunk_off, 0)),
            pl.BlockSpec((TM, D_MODEL), lambda i: (i + chunk_off, 0)),
            _mod_spec(layer, 5, chunk_off, 0),
            row, row,
        ],
        out_specs=pl.BlockSpec((TM, D_MODEL), lambda i: (i, 0)),
        out_shape=jax.ShapeDtypeStruct((n_chunks * TM, D_MODEL), F32),
        compiler_params=_cparams(1),
        name="moe_combine_layernorm",
    )(ys, meta, x, mods, ln_g, ln_b)


def _moe(x, xs, meta, pieces, mods, layer, w1, b1, w2, b2, ln_g, ln_b, split_out=False):
    te, nt, src, dst = _moe_schedule(pieces.reshape(N_CHUNKS, N_EXPERTS).astype(jnp.int32))
    ys = _experts(xs, te, nt, src, dst, layer, w1, b1, w2, b2)
    if not split_out:
        return _combine_ln(ys, meta, x, mods, layer, ln_g, ln_b)
    return (_combine_ln(ys, meta, x, mods, layer, ln_g, ln_b, 0, PROMPT_TILES),
            _combine_ln(ys, meta, x, mods, layer, ln_g, ln_b, PROMPT_TILES, SAMPLE_TILES))


def _mla_norm_kernel(p_ref, qn_ref, kvn_ref, cq_ref, ckvkr_ref):
    p = p_ref[...]
    cq = p[:, :MLA_Q_RANK]
    cq_ref[...] = cq * lax.rsqrt(jnp.mean(cq * cq, axis=-1, keepdims=True) + RMS_EPS) * qn_ref[...]
    ckv = p[:, MLA_Q_RANK:MLA_Q_RANK + MLA_KV_RANK]
    ckvkr_ref[:, :MLA_KV_RANK] = (ckv * lax.rsqrt(jnp.mean(ckv * ckv, axis=-1, keepdims=True) + RMS_EPS)
                                  * kvn_ref[...])
    ckvkr_ref[:, MLA_KV_RANK:] = p[:, MLA_Q_RANK + MLA_KV_RANK:]


def _mla_norm(proj, q_norm, kv_norm):
    return pl.pallas_call(
        _mla_norm_kernel,
        grid=(N_TOK // TM,),
        in_specs=[
            pl.BlockSpec((TM, 512), lambda i: (i, 0)),
            pl.BlockSpec((1, MLA_Q_RANK), lambda i: (0, 0)),
            pl.BlockSpec((1, MLA_KV_RANK), lambda i: (0, 0)),
        ],
        out_specs=[pl.BlockSpec((TM, MLA_Q_RANK), lambda i: (i, 0)),
                   pl.BlockSpec((TM, 2 * MLA_KV_RANK), lambda i: (i, 0))],
        out_shape=[jax.ShapeDtypeStruct((N_TOK, MLA_Q_RANK), F32),
                   jax.ShapeDtypeStruct((N_TOK, 2 * MLA_KV_RANK), F32)],
        compiler_params=_cparams(1),
        name="mla_norm",
    )(proj, q_norm, kv_norm)


def _mla_weights(w_in, w_uq, w_ukv):
    pad = 512 - w_in.shape[1]
    w_in_p = jnp.pad(w_in, ((0, 0), (0, pad)))
    wq = w_uq.reshape(MLA_Q_RANK, MLA_HEADS, MLA_NOPE + MLA_ROPE)
    wq = jnp.pad(wq, ((0, 0), (0, 0), (0, LANES - MLA_NOPE - MLA_ROPE))).reshape(MLA_Q_RANK, MLA_HEADS * LANES)
    wkv = w_ukv.reshape(MLA_KV_RANK, MLA_HEADS, MLA_NOPE + MLA_V)
    wkn = jnp.pad(wkv[:, :, :MLA_NOPE], ((0, 0), (0, 0), (0, LANES - MLA_NOPE)))
    eye = jnp.pad(jnp.eye(MLA_ROPE, dtype=F32), ((0, 0), (MLA_NOPE, LANES - MLA_NOPE - MLA_ROPE)))
    wkr = jnp.broadcast_to(eye[:, None, :], (MLA_ROPE, MLA_HEADS, LANES))
    zeros = jnp.zeros((2 * MLA_KV_RANK - MLA_KV_RANK - MLA_ROPE, MLA_HEADS, LANES), F32)
    wk = jnp.concatenate([wkn, wkr, zeros], axis=0).reshape(2 * MLA_KV_RANK, MLA_HEADS * LANES)
    wv = jnp.pad(wkv[:, :, MLA_NOPE:].reshape(MLA_KV_RANK, MLA_HEADS * MLA_V), ((0, MLA_KV_RANK), (0, 0)))
    return w_in_p.astype(BF16), wq.astype(BF16), wk.astype(BF16), wv.astype(BF16)


def kernel(x_prompt, x_sample, cache_na_k, cache_na_v, cache_diff_k, cache_diff_v, cache_mla_ckv, cache_mla_kr,
           c, c_ctx, ada_w, ada_b, ln_g, ln_b, na_w_qkv, na_rpb, na_w_o, diff_w_qkv, diff_lambda, diff_subln,
           diff_w_o, mla_w_in, mla_q_norm, mla_w_uq, mla_kv_norm, mla_w_ukv, mla_w_o, router_w, router_b,
           moe_w1, moe_b1, moe_w2, moe_b2):
    x = jnp.concatenate([x_prompt.reshape(N_PROMPT, D_MODEL), x_sample.reshape(N_SAMPLE, D_MODEL)], axis=0)
    cond = jnp.concatenate([c_ctx[None, :], c, jnp.zeros((N_MOD_ROWS - 1 - DEC_BATCH, D_MODEL), F32)], axis=0)
    mods = _modulation(cond, ada_w, ada_b)

    n_na = na_w_qkv.shape[0]
    n_diff = diff_w_qkv.shape[0]
    n_mla = mla_w_in.shape[0]
    cache_na_k = cache_na_k.reshape(DEC_BATCH, n_na, PAST_LEN, D_MODEL)
    cache_na_v = cache_na_v.reshape(DEC_BATCH, n_na, PAST_LEN, D_MODEL)
    cache_diff_k = cache_diff_k.reshape(DEC_BATCH, n_diff, PAST_LEN, D_MODEL)
    cache_diff_v = cache_diff_v.reshape(DEC_BATCH, n_diff, PAST_LEN, D_MODEL)
    all_tiles = N_TOK // TM
    nb = D_MODEL // LANES
    dec_off = N_PROMPT // DEC_SEQ

    st_na_k, st_na_v, st_diff_k, st_diff_v, st_mla_ckv, st_mla_kr = [], [], [], [], [], []
    for i in range(DEPTH):
        kind, j = i % N_MIXERS, i // N_MIXERS
        latent = dict(tile_off=PROMPT_TILES, mods=mods, layer=i)
        if kind == 0:
            w = na_w_qkv[j].astype(BF16)
            qk_scale = dict(q_cols=D_MODEL, q_scale=NA_HEAD_DIM ** -0.5, out_dtype=BF16)
            q_p, k_p, v_p = _context_qkv(x, w, mods, i, qk_scale, "na")
            st_na_k.append(k_p)
            st_na_v.append(v_p)
            qkv_s = _project(x, w, n_tiles=SAMPLE_TILES, name="na_qkv_latent", **latent, **qk_scale)
            o_p = _attention(q_p, k_p, v_p, mode="pair", n_batch=BATCH, seq=SEQ, tq=SEQ, q_col=0, k_col=0,
                             v_col=0, hb=nb, name="na_context_attention")
            o_s = _na_latent_attention(qkv_s, cache_na_k, cache_na_v, j, _na_bias_tables(na_rpb[j]))
            w_o = na_w_o[j]
        elif kind == 1:
            lam_init = 0.8 - 0.6 * math.exp(-0.3 * i)
            w = diff_w_qkv[j].astype(BF16)
            cos, sin = _axial_tables(LANES, DIFF_HEAD_DIM // 4, 0, LANES)
            qk_scale = dict(q_cols=D_MODEL, q_scale=DIFF_HEAD_DIM ** -0.5, out_dtype=BF16)
            q_p, k_p, v_p = _context_qkv(x, w, mods, i, qk_scale, "diff")
            st_diff_k.append(k_p)
            st_diff_v.append(v_p)
            qk_s = _project(x, w[:, :2 * D_MODEL], n_tiles=SAMPLE_TILES, rope=(DIFF_HEAD_DIM // 4, cos, sin),
                            name="diff_qk_latent", **latent, **qk_scale)
            v_s = _project(x, w[:, 2 * D_MODEL:], n_tiles=SAMPLE_TILES, out_dtype=BF16, name="diff_v_latent",
                           **latent)
            common = dict(mode="diff", lam=diff_lambda[j], subln=diff_subln[j][None, :], lam_init=lam_init)
            o_p = _attention(q_p, k_p, v_p, n_batch=BATCH, seq=SEQ, tq=SEQ, q_col=0, k_col=0, v_col=0, hb=nb,
                             name="diff_context_attention", **common)
            o_s = _attention(qk_s, qk_s, v_s, n_batch=DEC_BATCH, seq=DEC_SEQ, tq=LATENT_TQ,q_col=0, k_col=nb, v_col=0,
                             ctx=(cache_diff_k, cache_diff_v, j), hb=2, name="diff_latent_attention", **common)
            w_o = diff_w_o[j]
        else:
            w_in, wq, wk, wv = _mla_weights(mla_w_in[j], mla_w_uq[j], mla_w_ukv[j])
            cos, sin = _axial_tables(LANES, MLA_ROPE // 4, MLA_NOPE, MLA_NOPE + MLA_ROPE)
            rope = (MLA_ROPE // 4, cos, sin)
            proj = _project(x, w_in, n_tiles=all_tiles, mods=mods, layer=i, name="mla_down")
            cq, ckvkr = _mla_norm(proj, mla_q_norm[j][None, :], mla_kv_norm[j][None, :])
            st_mla_ckv.append(ckvkr[:N_PROMPT, :MLA_KV_RANK])
            st_mla_kr.append(ckvkr[:N_PROMPT, MLA_KV_RANK:MLA_KV_RANK + MLA_ROPE])
            q_scale = dict(q_cols=MLA_HEADS * LANES, q_scale=(MLA_NOPE + MLA_ROPE) ** -0.5, out_dtype=BF16)
            q_p = _project(cq, wq, n_tiles=PROMPT_TILES, name="mla_q_context", **q_scale)
            k_p = _project(ckvkr, wk, n_tiles=PROMPT_TILES, out_dtype=BF16, name="mla_k_context")
            q_s = _project(cq, wq, tile_off=PROMPT_TILES, n_tiles=SAMPLE_TILES, rope=rope, name="mla_q_latent",
                           **q_scale)
            k_s = _project(ckvkr, wk, tile_off=PROMPT_TILES, n_tiles=SAMPLE_TILES, rope=rope, out_dtype=BF16,
                           name="mla_k_latent")
            v_all = _project(ckvkr, wv, n_tiles=all_tiles, out_dtype=BF16, name="mla_v")
            ctx_in = jnp.concatenate(
                [cache_mla_ckv[:, j], cache_mla_kr[:, j],
                 jnp.zeros((DEC_BATCH, PAST_LEN, 2 * MLA_KV_RANK - MLA_KV_RANK - MLA_ROPE), F32)],
                axis=-1).reshape(DEC_BATCH * PAST_LEN, 2 * MLA_KV_RANK)
            n_ctx_tiles = DEC_BATCH * PAST_LEN // TM
            k_c = _project(ctx_in, wk, n_tiles=n_ctx_tiles, out_dtype=BF16, name="mla_k_cache")
            v_c = _project(ctx_in, wv, n_tiles=n_ctx_tiles, out_dtype=BF16, name="mla_v_cache")
            k_c = k_c.reshape(DEC_BATCH, 1, PAST_LEN, MLA_HEADS * LANES)
            v_c = v_c.reshape(DEC_BATCH, 1, PAST_LEN, D_MODEL)
            o_p = _attention(q_p, k_p, v_all, mode="mla", n_batch=BATCH, seq=SEQ, tq=SEQ, q_col=0, k_col=0,
                             v_col=0, hb=nb, name="mla_context_attention")
            o_s = _attention(q_s, k_s, v_all, mode="mla", n_batch=DEC_BATCH, seq=DEC_SEQ, tq=LATENT_TQ,q_col=0, k_col=0,
                             v_col=0, v_row_off=dec_off, ctx=(k_c, v_c, 0), hb=2, name="mla_latent_attention")
            w_o = mla_w_o[j]

        x, xs, meta, pieces = _out_route(o_p, o_s, w_o.astype(BF16), x, mods, i, ln_g[i, 0][None, :],
                                         ln_b[i, 0][None, :], router_w, router_b)
        x = _moe(x, xs, meta, pieces, mods, i, moe_w1, moe_b1, moe_w2, moe_b2,
                 ln_g[i, 1][None, :], ln_b[i, 1][None, :], split_out=i == DEPTH - 1)

    y_prompt = x[0].reshape(BATCH, SEQ, D_MODEL)
    y_sample = x[1].reshape(DEC_BATCH, DEC_SEQ, D_MODEL)

    def stack(parts, *tail):
        return jnp.stack([p.reshape(BATCH, SEQ, *tail) for p in parts], axis=1)

    return (y_prompt, y_sample,
            stack(st_na_k, NA_HEADS, NA_HEAD_DIM), stack(st_na_v, NA_HEADS, NA_HEAD_DIM),
            stack(st_diff_k, DIFF_HEADS, 2 * DIFF_HEAD_DIM), stack(st_diff_v, DIFF_HEADS, 2 * DIFF_HEAD_DIM),
            stack(st_mla_ckv, MLA_KV_RANK), stack(st_mla_kr, MLA_ROPE))
```

```python
import functools
import math

import jax
import jax.numpy as jnp
from jax import lax
from jax.experimental import pallas as pl
from jax.experimental.pallas import tpu as pltpu

F32 = jnp.float32
BF16 = jnp.bfloat16
HIGHEST = lax.Precision.HIGHEST

D_MODEL = 1024
BATCH = 32
SEQ = 256
DEPTH = 4
DEC_BATCH = 4
DEC_SEQ = 2048
PAST_LEN = 512
GRID_W = 64
N_MIXERS = 3
NA_HEADS = 16
NA_HEAD_DIM = 64
WIN_R = 8
WIN_C = 16
DIFF_HEADS = 8
DIFF_HEAD_DIM = 64
MLA_HEADS = 16
MLA_Q_RANK = 256
MLA_KV_RANK = 128
MLA_NOPE = 64
MLA_ROPE = 32
MLA_V = 64
N_EXPERTS = 32
TOP_K = 4
D_EXPERT = 1024
SWIGLU_LIMIT = 7.0
SWIGLU_ALPHA = 1.702
ROPE_BASE = 10000.0
LN_EPS = 1e-5
RMS_EPS = 1e-6
NEG_INF = -1e30
DEEPNORM_ALPHA = (2 * DEPTH) ** 0.25

LANES = 128
TM = 256
N_PROMPT = BATCH * SEQ
N_SAMPLE = DEC_BATCH * DEC_SEQ
N_TOK = N_PROMPT + N_SAMPLE
PROMPT_TILES = N_PROMPT // TM
SAMPLE_TILES = N_SAMPLE // TM
TILES_PER_DEC = DEC_SEQ // TM
N_MOD_ROWS = 8
LATENT_TQ = 512
PROJ_TILES = 2
PROJ_TM = PROJ_TILES * TM
NA_QROWS = 4
NA_KROWS = NA_QROWS + WIN_R - 1
NA_BIAS_PAD = NA_QROWS - 1
NA_BIAS_ROWS = WIN_R - 1 + NA_BIAS_PAD + NA_KROWS
N_CHUNKS = N_TOK // TM
MOE_PIECE = 8
MAX_CHUNK_PIECES = (TM * TOP_K + N_EXPERTS * (MOE_PIECE - 1)) // MOE_PIECE
MOE_CHUNK_PIECES = 160
MOE_CHUNK_ROWS = MOE_CHUNK_PIECES * MOE_PIECE
EXP_TM = 512
EXP_PPT = EXP_TM // MOE_PIECE
MAX_TILES = N_CHUNKS * MAX_CHUNK_PIECES // EXP_PPT + N_EXPERTS
VMEM_LIMIT = 56 * 1024 * 1024


def _cparams(n_axes, vmem=VMEM_LIMIT):
    return pltpu.CompilerParams(dimension_semantics=("arbitrary",) * n_axes, vmem_limit_bytes=vmem)


def _mod_row(t):
    return jnp.where(t < PROMPT_TILES, 0, 1 + (t - PROMPT_TILES) // TILES_PER_DEC)


def _mod_spec(layer, chunk, tile_off, tile_axis, tiles_per_step=1):
    def index(*g):
        row = _mod_row((g[tile_axis] + tile_off) * tiles_per_step)
        return ((layer * N_MOD_ROWS + row) * 6 + chunk, 0, 0)
    return pl.BlockSpec((None, 1, D_MODEL), index)


def _sigmoid(x):
    return 1.0 / (1.0 + jnp.exp(-x))


def _layernorm(z, g, b):
    mu = jnp.mean(z, axis=-1, keepdims=True)
    zc = z - mu
    var = jnp.mean(zc * zc, axis=-1, keepdims=True)
    return zc * lax.rsqrt(var + LN_EPS) * g + b


def _mod_kernel(cond_ref, w_ref, b_ref, o_ref):
    cnd = cond_ref[...]
    s = cnd * _sigmoid(cnd)
    o_ref[...] = jnp.dot(s, w_ref[...], precision=HIGHEST, preferred_element_type=F32) + b_ref[...]


def _modulation(cond, ada_w, ada_b):
    out = pl.pallas_call(
        _mod_kernel,
        grid=(DEPTH, 6),
        in_specs=[
            pl.BlockSpec((N_MOD_ROWS, D_MODEL), lambda l, j: (0, 0)),
            pl.BlockSpec((None, D_MODEL, D_MODEL), lambda l, j: (l, 0, j)),
            pl.BlockSpec((None, 1, D_MODEL), lambda l, j: (l, 0, j)),
        ],
        out_specs=pl.BlockSpec((None, N_MOD_ROWS, D_MODEL), lambda l, j: (l, 0, j)),
        out_shape=jax.ShapeDtypeStruct((DEPTH, N_MOD_ROWS, 6 * D_MODEL), F32),
        compiler_params=_cparams(2),
        name="adaln_modulation",
    )(cond, ada_w, ada_b.reshape(DEPTH, 1, 6 * D_MODEL))
    return out.reshape(DEPTH * N_MOD_ROWS * 6, 1, D_MODEL)


def _rope_swap(y, g):
    lane = lax.broadcasted_iota(jnp.int32, y.shape, 1)
    first = (lane % (2 * g)) < g
    return jnp.where(first, pltpu.roll(y, LANES - g, 1), pltpu.roll(y, g, 1))


def _proj_kernel(*refs, use_mod, rope_g, q_tiles, q_scale):
    it = iter(refs)
    x_ref = next(it)
    x = x_ref[...]
    if use_mod:
        shift_ref, scale_ref = next(it), next(it)
        x = x * (1.0 + scale_ref[...]) + shift_ref[...]
    w_ref = next(it)
    y = jnp.dot(x.astype(BF16), w_ref[...], preferred_element_type=F32)
    col_scale = jnp.where(pl.program_id(0) < q_tiles, q_scale, 1.0) if q_tiles else None
    if rope_g:
        cos_ref, sin_ref = next(it), next(it)
        o_ref = next(it)
        cos, sin = cos_ref[...], sin_ref[...]
        for j in range(y.shape[1] // LANES):
            yb = y[:, j * LANES:(j + 1) * LANES]
            yb = yb * cos + _rope_swap(yb, rope_g) * sin
            if q_tiles:
                yb = yb * col_scale
            o_ref[:, j * LANES:(j + 1) * LANES] = yb.astype(o_ref.dtype)
    else:
        o_ref = next(it)
        if q_tiles:
            y = y * col_scale
        o_ref[...] = y.astype(o_ref.dtype)


def _project(x, w_bf16, *, tile_off=0, n_tiles, mods=None, layer=0, chunks=(0, 1), rope=None, q_cols=0,
             q_scale=1.0, out_dtype=F32, name):
    k_dim, n = w_bf16.shape
    tn = min(n, 1024)
    assert q_cols % tn == 0 and tile_off % PROJ_TILES == 0 and n_tiles % PROJ_TILES == 0
    off = tile_off // PROJ_TILES
    grid = (n // tn, n_tiles // PROJ_TILES)
    in_specs = [pl.BlockSpec((PROJ_TM, k_dim), lambda j, i: (i + off, 0))]
    args = [x]
    if mods is not None:
        in_specs += [_mod_spec(layer, chunks[0], off, 1, PROJ_TILES), _mod_spec(layer, chunks[1], off, 1, PROJ_TILES)]
        args += [mods, mods]
    in_specs.append(pl.BlockSpec((k_dim, tn), lambda j, i: (0, j)))
    args.append(w_bf16)
    rope_g = 0
    if rope is not None:
        rope_g, cos, sin = rope
        tab = pl.BlockSpec((PROJ_TM, LANES), lambda j, i: (i % (DEC_SEQ // PROJ_TM), 0))
        in_specs += [tab, tab]
        args += [cos, sin]
    return pl.pallas_call(
        functools.partial(_proj_kernel, use_mod=mods is not None, rope_g=rope_g, q_tiles=q_cols // tn,
                          q_scale=q_scale),
        grid=grid,
        in_specs=in_specs,
        out_specs=pl.BlockSpec((PROJ_TM, tn), lambda j, i: (i, j)),
        out_shape=jax.ShapeDtypeStruct((n_tiles * TM, n), out_dtype),
        compiler_params=_cparams(2),
        name=name,
    )(*args)


def _context_qkv(x, w_bf16, mods, layer, qk_scale, tag):
    common = dict(n_tiles=PROMPT_TILES, mods=mods, layer=layer)
    q = _project(x, w_bf16[:, :D_MODEL], name=tag + "_q_context", **common, **qk_scale)
    k = _project(x, w_bf16[:, D_MODEL:2 * D_MODEL], name=tag + "_k_context", **common)
    v = _project(x, w_bf16[:, 2 * D_MODEL:], name=tag + "_v_context", **common)
    return q, k, v


def _axial_tables(group_lanes, nf, lane_lo, lane_hi):
    del group_lanes
    t = jnp.arange(DEC_SEQ)
    lane = jnp.arange(LANES)
    rel = lane - lane_lo
    active = (lane >= lane_lo) & (lane < lane_hi)
    grp = rel // (2 * nf)
    pos = jnp.where((grp % 2 == 0)[None, :], (t // GRID_W)[:, None], (t % GRID_W)[:, None])
    inv = ROPE_BASE ** (-jnp.arange(nf, dtype=F32) / nf)
    ang = pos.astype(F32) * inv[rel % nf][None, :]
    cos = jnp.cos(ang)
    sin = jnp.sin(ang)
    sign = jnp.where((rel % (2 * nf)) < nf, -1.0, 1.0)[None, :]
    cos = jnp.where(active[None, :], cos, 1.0).astype(F32)
    sin = jnp.where(active[None, :], sin * sign, 0.0).astype(F32)
    return cos, sin


def _nt_dot(a, b):
    return lax.dot_general(a, b, (((1,), (1,)), ((), ())), preferred_element_type=F32)


def _half_values(vs, half):
    lane = lax.broadcasted_iota(jnp.int32, (1, LANES), 1)
    keep = jnp.where((lane < 64) if half == 0 else (lane >= 64), 1.0, 0.0).astype(BF16)
    return [v * keep + (1 - keep) for v in vs]


def _pair_softmax_pv(scores_a, scores_b, vs):
    outs = []
    for half, scores in enumerate((scores_a, scores_b)):
        m = scores[0].max(axis=-1, keepdims=True)
        for s in scores[1:]:
            m = jnp.maximum(m, s.max(axis=-1, keepdims=True))
        acc = None
        for s, v in zip(scores, _half_values(vs, half)):
            o = jnp.dot(jnp.exp(s - m).astype(BF16), v, preferred_element_type=F32)
            acc = o if acc is None else acc + o
        outs.append(acc / pltpu.roll(acc, 64, 1))
    lane = lax.broadcasted_iota(jnp.int32, outs[0].shape, 1)
    return jnp.where(lane < 64, outs[0], outs[1])


def _softmax_pv(scores, vs):
    m = scores[0].max(axis=-1, keepdims=True)
    for s in scores[1:]:
        m = jnp.maximum(m, s.max(axis=-1, keepdims=True))
    acc, den = None, None
    for s, v in zip(scores, vs):
        e = jnp.exp(s - m)
        d = e.sum(axis=-1, keepdims=True)
        o = jnp.dot(e.astype(BF16), v, preferred_element_type=F32)
        acc = o if acc is None else acc + o
        den = d if den is None else den + d
    return acc / den


def _attn_kernel(*refs, mode, has_ctx, lam_init, hb):
    it = iter(refs)
    q_ref, k_ref, v_ref = next(it), next(it), next(it)
    kc_ref = vc_ref = None
    if has_ctx:
        kc_ref, vc_ref = next(it), next(it)
    if mode == "diff":
        lam_ref, subln_ref = next(it), next(it)
    o_ref = next(it)

    wq = 2 * LANES if mode == "mla" else LANES
    if mode == "diff":
        lp = lam_ref[...]
        lam = (jnp.exp(jnp.sum(lp[0:1] * lp[1:2], axis=-1, keepdims=True))
               - jnp.exp(jnp.sum(lp[2:3] * lp[3:4], axis=-1, keepdims=True)) + lam_init)

    for hblk in range(hb):
        qk_cols = slice(hblk * wq, (hblk + 1) * wq)
        v_cols = slice(hblk * LANES, (hblk + 1) * LANES)
        q = q_ref[:, qk_cols].astype(BF16)
        ks = [k_ref[:, qk_cols].astype(BF16)]
        vs = [v_ref[:, v_cols].astype(BF16)]
        if has_ctx:
            ks.append(kc_ref[:, qk_cols].astype(BF16))
            vs.append(vc_ref[:, v_cols].astype(BF16))

        if mode == "mla":
            halves = [(q[:, :LANES], [k[:, :LANES] for k in ks]), (q[:, LANES:], [k[:, LANES:] for k in ks])]
        else:
            lane_q = lax.broadcasted_iota(jnp.int32, q.shape, 1)
            zero = jnp.zeros_like(q)
            halves = [(jnp.where(lane_q < 64, q, zero), ks), (jnp.where(lane_q >= 64, q, zero), ks)]

        scores = [[_nt_dot(qh, k) for k in kh] for qh, kh in halves]
        if mode == "diff":
            o = _softmax_pv(scores[0], vs) - lam * _softmax_pv(scores[1], vs)
            o = o * lax.rsqrt(jnp.mean(o * o, axis=-1, keepdims=True) + RMS_EPS) * subln_ref[...]
            o_ref[:, v_cols] = (o * (1.0 - lam_init)).astype(o_ref.dtype)
        else:
            o_ref[:, v_cols] = _pair_softmax_pv(scores[0], scores[1], vs).astype(o_ref.dtype)


def _attention(q_arr, k_arr, v_arr, *, mode, n_batch, seq, tq, q_col, k_col, v_col, row_off=0, v_row_off=None,
               ctx=None, lam=None, subln=None, lam_init=0.0, hb=1, name):
    assert mode != "mla" or (q_col == 0 and k_col == 0)
    n_blocks = D_MODEL // LANES
    assert n_blocks % hb == 0 and q_col % hb == 0 and k_col % hb == 0 and v_col % hb == 0
    wq = 2 * LANES if mode == "mla" else LANES
    wqb, wvb = wq * hb, LANES * hb
    nqb = seq // tq
    qoff = row_off * nqb
    if v_row_off is None:
        v_row_off = row_off
    in_specs = [
        pl.BlockSpec((tq, wqb), lambda b, h, i: (qoff + b * nqb + i, q_col // hb + h)),
        pl.BlockSpec((seq, wqb), lambda b, h, i: (row_off + b, k_col // hb + h)),
        pl.BlockSpec((seq, wvb), lambda b, h, i: (v_row_off + b, v_col // hb + h)),
    ]
    args = [q_arr, k_arr, v_arr]
    if ctx is not None:
        kc, vc, j = ctx
        in_specs += [
            pl.BlockSpec((None, None, PAST_LEN, wqb), lambda b, h, i: (b, j, 0, h)),
            pl.BlockSpec((None, None, PAST_LEN, wvb), lambda b, h, i: (b, j, 0, h)),
        ]
        args += [kc, vc]
    if mode == "diff":
        in_specs += [pl.BlockSpec((4, DIFF_HEAD_DIM), lambda b, h, i: (0, 0)),
                     pl.BlockSpec((1, LANES), lambda b, h, i: (0, 0))]
        args += [lam, subln]
    return pl.pallas_call(
        functools.partial(_attn_kernel, mode=mode, has_ctx=ctx is not None, lam_init=lam_init, hb=hb),
        grid=(n_batch, n_blocks // hb, nqb),
        in_specs=in_specs,
        out_specs=pl.BlockSpec((tq, wvb), lambda b, h, i: (b * nqb + i, h)),
        out_shape=jax.ShapeDtypeStruct((n_batch * seq, D_MODEL), BF16),
        compiler_params=_cparams(3),
        name=name,
    )(*args)


def _na_bias_tables(rpb):
    cols = jnp.arange(GRID_W)
    c0 = jnp.clip(cols - WIN_C // 2, 0, GRID_W - WIN_C)
    col_ok = (cols[None, :] >= c0[:, None]) & (cols[None, :] < c0[:, None] + WIN_C)
    col_off = jnp.clip(cols[None, :] - cols[:, None] + (WIN_C - 1), 0, 2 * WIN_C - 2)
    hot_c = (col_off[:, :, None] == jnp.arange(2 * WIN_C - 1)).astype(F32)
    core = jnp.einsum("hab,ckb->hcak", rpb, hot_c, precision=HIGHEST)
    core = jnp.where(col_ok[None, :, None, :], core, NEG_INF)
    pad_hi = NA_BIAS_ROWS - NA_BIAS_PAD - (2 * WIN_R - 1)
    tab = jnp.pad(core, ((0, 0), (0, 0), (NA_BIAS_PAD, pad_hi), (0, 0)), constant_values=NEG_INF)
    tab = tab.reshape(NA_HEADS, GRID_W, NA_BIAS_ROWS * GRID_W)
    shifted = jnp.pad(tab[:, :, GRID_W:], ((0, 0), (0, 0), (0, GRID_W)), constant_values=NEG_INF)
    return jnp.stack([tab, shifted]).astype(F32)


def _na_block_bias(bias_ref, half, kind):
    nk = NA_KROWS * GRID_W
    lane = lax.broadcasted_iota(jnp.int32, (GRID_W, nk), 1)
    slabs = []
    for i in range(NA_QROWS):
        if kind == 0:
            ext0, lo, hi = WIN_R - 1 + NA_BIAS_PAD - i, 0, WIN_R
        elif kind == 1:
            ext0, lo, hi = WIN_R - 1 - WIN_R // 2 + NA_BIAS_PAD - i, i, i + WIN_R
        else:
            ext0, lo, hi = NA_BIAS_PAD - i, NA_KROWS - WIN_R, NA_KROWS
        start = (ext0 - ext0 % 2) * GRID_W
        slab = bias_ref[ext0 % 2, half, :, start:start + nk]
        slabs.append(jnp.where((lane >= lo * GRID_W) & (lane < hi * GRID_W), slab, NEG_INF))
    return jnp.concatenate(slabs, axis=0)


def _na_latent_kernel(q_ref, k_ref, v_ref, kc_ref, vc_ref, bias_ref, o_ref):
    rows_n = DEC_SEQ // GRID_W
    kc = kc_ref[...].astype(BF16)
    vc = vc_ref[...].astype(BF16)
    tq = NA_QROWS * GRID_W
    nk = NA_KROWS * GRID_W
    lane = lax.broadcasted_iota(jnp.int32, (tq, LANES), 1)
    biases = {}
    for blk in range(rows_n // NA_QROWS):
        q_row0 = blk * NA_QROWS
        if blk == 0:
            kind, k_row0 = 0, 0
        elif blk == rows_n // NA_QROWS - 1:
            kind, k_row0 = 2, rows_n - NA_KROWS
        else:
            kind, k_row0 = 1, q_row0 - WIN_R // 2
        q = q_ref[q_row0 * GRID_W:q_row0 * GRID_W + tq, :].astype(BF16)
        k = k_ref[k_row0 * GRID_W:k_row0 * GRID_W + nk, :].astype(BF16)
        v = v_ref[k_row0 * GRID_W:k_row0 * GRID_W + nk, :].astype(BF16)
        zero = jnp.zeros_like(q)
        scores = []
        for half in range(2):
            qh = jnp.where((lane < 64) if half == 0 else (lane >= 64), q, zero)
            if (kind, half) not in biases:
                biases[kind, half] = _na_block_bias(bias_ref, half, kind)
            scores.append([_nt_dot(qh, k) + biases[kind, half], _nt_dot(qh, kc)])
        o_ref[q_row0 * GRID_W:q_row0 * GRID_W + tq, :] = _pair_softmax_pv(
            scores[0], scores[1], [v, vc]).astype(o_ref.dtype)


def _na_latent_attention(qkv, cache_k, cache_v, j, bias):
    nb = D_MODEL // LANES
    return pl.pallas_call(
        _na_latent_kernel,
        grid=(nb, DEC_BATCH),
        in_specs=[
            pl.BlockSpec((DEC_SEQ, LANES), lambda h, b: (b, h)),
            pl.BlockSpec((DEC_SEQ, LANES), lambda h, b: (b, nb + h)),
            pl.BlockSpec((DEC_SEQ, LANES), lambda h, b: (b, 2 * nb + h)),
            pl.BlockSpec((None, None, PAST_LEN, LANES), lambda h, b: (b, j, 0, h)),
            pl.BlockSpec((None, None, PAST_LEN, LANES), lambda h, b: (b, j, 0, h)),
            pl.BlockSpec((2, 2, GRID_W, NA_BIAS_ROWS * GRID_W), lambda h, b: (0, h, 0, 0)),
        ],
        out_specs=pl.BlockSpec((DEC_SEQ, LANES), lambda h, b: (b, h)),
        out_shape=jax.ShapeDtypeStruct((N_SAMPLE, D_MODEL), BF16),
        compiler_params=_cparams(2),
        name="na_latent_attention",
    )(qkv, qkv, qkv, cache_k, cache_v, bias)


def _out_route_kernel(op_ref, os_ref, w_ref, x_ref, gate_ref, g_ref, b_ref, shift_ref, scale_ref, rw_ref, rb_ref,
                      xo_ref, xs_ref, meta_ref, cnt_ref, o_sc):
    i = pl.program_id(0)

    @pl.when(i < PROMPT_TILES)
    def _():
        o_sc[...] = op_ref[...].astype(BF16)

    @pl.when((i >= PROMPT_TILES) & (i < N_CHUNKS))
    def _():
        o_sc[...] = os_ref[...].astype(BF16)

    @pl.when(i < N_CHUNKS)
    def _():
        y = jnp.dot(o_sc[...], w_ref[...], preferred_element_type=F32)
        z = DEEPNORM_ALPHA * x_ref[...] + gate_ref[...] * y
        x_new = _layernorm(z, g_ref[...], b_ref[...])
        xo_ref[...] = x_new
        _route_chunk(x_new, shift_ref, scale_ref, rw_ref, rb_ref, xs_ref, meta_ref, cnt_ref)

    @pl.when(i == N_CHUNKS)
    def _():
        xs_ref[...] = jnp.zeros_like(xs_ref)


def _route_chunk(x, shift_ref, scale_ref, rw_ref, rb_ref, xs_ref, meta_ref, cnt_ref):
    h = x * (1.0 + scale_ref[...]) + shift_ref[...]
    h_hi = h.astype(BF16)
    h_lo = (h - h_hi.astype(F32)).astype(BF16)
    logits = (jnp.dot(h_hi, rw_ref[0], preferred_element_type=F32)
              + jnp.dot(h_lo, rw_ref[0], preferred_element_type=F32)
              + jnp.dot(h_hi, rw_ref[1], preferred_element_type=F32)) + rb_ref[...]
    lane = lax.broadcasted_iota(jnp.int32, logits.shape, 1)
    work = logits
    top_v, hots = [], []
    for _ in range(TOP_K):
        m = work.max(axis=-1, keepdims=True)
        first = jnp.where(work == m, lane, N_EXPERTS).min(axis=-1, keepdims=True)
        hot = lane == first
        top_v.append(m)
        hots.append(jnp.where(hot, 1.0, 0.0))
        work = jnp.where(hot, -jnp.inf, work)
    es = [jnp.exp(v - top_v[0]) for v in top_v]
    den = es[0] + es[1] + es[2] + es[3]
    gates = [e / den for e in es]

    cnts = [hf.sum(axis=0, keepdims=True) for hf in hots]
    n_e = cnts[0] + cnts[1] + cnts[2] + cnts[3]
    pieces = jnp.floor((n_e + (MOE_PIECE - 1)) * (1.0 / MOE_PIECE))
    er = lax.broadcasted_iota(jnp.int32, (N_EXPERTS, N_EXPERTS), 0)
    ec = lax.broadcasted_iota(jnp.int32, (N_EXPERTS, N_EXPERTS), 1)
    upper = jnp.where(er < ec, 1.0, 0.0).astype(BF16)
    poff = jnp.dot(jnp.broadcast_to(pieces, (8, N_EXPERTS)).astype(BF16), upper,
                   preferred_element_type=F32)[0:1]

    tr = lax.broadcasted_iota(jnp.int32, (TM, TM), 0)
    tc = lax.broadcasted_iota(jnp.int32, (TM, TM), 1)
    lower = jnp.where(tc < tr, 1.0, 0.0).astype(BF16)
    run = poff * MOE_PIECE
    slots = []
    for k in range(TOP_K):
        prefix = jnp.dot(lower, hots[k].astype(BF16), preferred_element_type=F32)
        slots.append(jnp.sum(hots[k] * (run + prefix), axis=-1, keepdims=True))
        run = run + cnts[k]

    lane_w = lax.broadcasted_iota(jnp.int32, (TM, LANES), 1)
    meta = jnp.full((TM, LANES), -1.0, F32)
    for k in range(TOP_K):
        meta = jnp.where(lane_w == k, slots[k], meta)
        meta = jnp.where(lane_w == TOP_K + k, gates[k], meta)
    meta_ref[...] = meta[:, :2 * TOP_K]
    cnt_ref[...] = pieces

    slot_rows = meta.T
    srow = lax.broadcasted_iota(jnp.int32, (MOE_CHUNK_ROWS, TM), 0).astype(F32)
    p = jnp.zeros((MOE_CHUNK_ROWS, TM), F32)
    for k in range(TOP_K):
        p = p + jnp.where(srow == slot_rows[k:k + 1, :], 1.0, 0.0)
    xs_ref[...] = jnp.dot(p.astype(BF16), h_hi, preferred_element_type=F32)


def _out_route(o_p, o_s, w_bf16, x, mods, layer, ln_g, ln_b, rw, rb):
    assert 2 * EXP_TM <= MOE_CHUNK_ROWS
    k_dim = w_bf16.shape[0]
    last = N_CHUNKS - 1
    rw_hi = rw.astype(BF16)
    rw_split = jnp.stack([rw_hi, (rw - rw_hi.astype(F32)).astype(BF16)], axis=1)
    row = pl.BlockSpec((1, D_MODEL), lambda i: (0, 0))
    return pl.pallas_call(
        _out_route_kernel,
        grid=(N_CHUNKS + 1,),
        in_specs=[
            pl.BlockSpec((TM, k_dim), lambda i: (jnp.minimum(i, PROMPT_TILES - 1), 0)),
            pl.BlockSpec((TM, k_dim), lambda i: (jnp.clip(i - PROMPT_TILES, 0, SAMPLE_TILES - 1), 0)),
            pl.BlockSpec((k_dim, D_MODEL), lambda i: (0, 0)),
            pl.BlockSpec((TM, D_MODEL), lambda i: (jnp.minimum(i, last), 0)),
            _mod_spec(layer, 2, 0, 0),
            row, row,
            _mod_spec(layer, 3, 0, 0),
            _mod_spec(layer, 4, 0, 0),
            pl.BlockSpec((None, 2, D_MODEL, N_EXPERTS), lambda i: (layer, 0, 0, 0)),
            pl.BlockSpec((None, 1, N_EXPERTS), lambda i: (layer, 0, 0)),
        ],
        out_specs=[
            pl.BlockSpec((TM, D_MODEL), lambda i: (jnp.minimum(i, last), 0)),
            pl.BlockSpec((MOE_CHUNK_ROWS, D_MODEL), lambda i: (i, 0)),
            pl.BlockSpec((TM, 2 * TOP_K), lambda i: (jnp.minimum(i, last), 0)),
            pl.BlockSpec((None, 1, N_EXPERTS), lambda i: (jnp.minimum(i, last), 0, 0)),
        ],
        out_shape=[
            jax.ShapeDtypeStruct((N_TOK, D_MODEL), F32),
            jax.ShapeDtypeStruct(((N_CHUNKS + 1) * MOE_CHUNK_ROWS, D_MODEL), F32),
            jax.ShapeDtypeStruct((N_TOK, 2 * TOP_K), F32),
            jax.ShapeDtypeStruct((N_CHUNKS, 1, N_EXPERTS), F32),
        ],
        scratch_shapes=[pltpu.VMEM((TM, k_dim), BF16)],
        compiler_params=_cparams(1),
        name="out_proj_layernorm_route",
    )(o_p, o_s, w_bf16, x, mods, ln_g, ln_b, mods, mods, rw_split, rb.reshape(DEPTH, 1, N_EXPERTS))


def _moe_schedule(pieces):
    e_ids = jnp.arange(N_EXPERTS, dtype=jnp.int32)
    npe = pieces.sum(axis=0)
    tiles_e = (npe + EXP_PPT - 1) // EXP_PPT
    tile_end = jnp.cumsum(tiles_e)
    tile_start = tile_end - tiles_e
    g = jnp.arange(MAX_TILES, dtype=jnp.int32)
    last_e = jnp.max(jnp.where(tiles_e > 0, e_ids, 0))
    te = jnp.minimum(jnp.sum(tile_end[None, :] <= g[:, None], axis=1), last_e).astype(jnp.int32)
    seg_len = pieces.T
    seg_base = tile_start[:, None] * EXP_PPT + jnp.cumsum(seg_len, axis=1) - seg_len
    poff = jnp.cumsum(pieces, axis=1) - pieces
    seg_pos = (jnp.arange(N_CHUNKS, dtype=jnp.int32)[:, None] * MOE_CHUNK_PIECES + poff).T
    hot = (te[:, None] == e_ids[None, :])[:, :, None]
    base, length, first = (jnp.sum(jnp.where(hot, tab[None], 0), axis=1)
                           for tab in (seg_base, seg_len, seg_pos))
    lane = jnp.arange(EXP_PPT, dtype=jnp.int32)[None, :]
    pos = (g[:, None] * EXP_PPT + lane)[:, :, None]
    hit = (base[:, None, :] <= pos) & (pos < (base + length)[:, None, :])
    src = jnp.sum(jnp.where(hit, first[:, None, :] + pos - base[:, None, :], 0), axis=-1)
    valid = jnp.any(hit, axis=-1)
    scratch = N_CHUNKS * MOE_CHUNK_PIECES + (g[:, None] % 2) * EXP_PPT + lane
    dst = jnp.where(valid, src, scratch).reshape(-1).astype(jnp.int32)
    src = jnp.where(valid, src, src[:, :1]).reshape(-1).astype(jnp.int32)
    return te, tile_end[-1:].astype(jnp.int32), src, dst


def _expert_kernel(te_ref, nt_ref, src_ref, dst_ref, xs_hbm, w1_ref, b1_ref, w2_ref, b2_ref, out_hbm,
                   xbuf, ybuf, w1s, w2s, sem_in, sem_out):
    g = pl.program_id(0)
    nt = nt_ref[0]
    slot = g % 2

    def in_copy(tile, i, s):
        row = pl.multiple_of(src_ref[tile * EXP_PPT + i] * MOE_PIECE, MOE_PIECE)
        return pltpu.make_async_copy(xs_hbm.at[pl.ds(row, MOE_PIECE), :],
                                     xbuf.at[s, pl.ds(i * MOE_PIECE, MOE_PIECE), :], sem_in.at[s])

    def out_copy(tile, i, s):
        row = pl.multiple_of(dst_ref[tile * EXP_PPT + i] * MOE_PIECE, MOE_PIECE)
        return pltpu.make_async_copy(ybuf.at[s, pl.ds(i * MOE_PIECE, MOE_PIECE), :],
                                     out_hbm.at[pl.ds(row, MOE_PIECE), :], sem_out.at[s])

    def gather(tile, s, wait):
        for i in range(EXP_PPT):
            cp = in_copy(tile, i, s)
            cp.wait() if wait else cp.start()

    def scatter(tile, s, wait):
        for i in range(EXP_PPT):
            cp = out_copy(tile, i, s)
            cp.wait() if wait else cp.start()

    @pl.when(g < nt)
    def _():
        @pl.when(g == 0)
        def _():
            gather(0, 0, False)

        @pl.when(g + 1 < nt)
        def _():
            gather(g + 1, 1 - slot, False)

        gather(g, slot, True)

        @pl.when(g >= 2)
        def _():
            scatter(g - 2, slot, True)

        @pl.when((g == 0) | (te_ref[g] != te_ref[jnp.maximum(g - 1, 0)]))
        def _():
            w1s[...] = w1_ref[...].astype(BF16)
            w2s[...] = w2_ref[...].astype(BF16)

        gu = jnp.dot(xbuf[slot].astype(BF16), w1s[...], preferred_element_type=F32) + b1_ref[...]
        gl = jnp.minimum(gu[:, :D_EXPERT], SWIGLU_LIMIT)
        ul = jnp.clip(gu[:, D_EXPERT:], -SWIGLU_LIMIT, SWIGLU_LIMIT)
        act = (ul + 1.0) * (gl * _sigmoid(SWIGLU_ALPHA * gl))
        ybuf[slot] = jnp.dot(act.astype(BF16), w2s[...], preferred_element_type=F32) + b2_ref[...]
        scatter(g, slot, False)

        @pl.when(g == nt - 1)
        def _():
            @pl.when(g >= 1)
            def _():
                scatter(g - 1, 1 - slot, True)
            scatter(g, slot, True)


def _experts(xs, te, nt, src, dst, layer, w1, b1, w2, b2):
    def expert_block(g, te, nt, src, dst):
        return (layer, te[g], 0, 0)

    grid_spec = pltpu.PrefetchScalarGridSpec(
        num_scalar_prefetch=4,
        grid=(MAX_TILES,),
        in_specs=[
            pl.BlockSpec(memory_space=pl.ANY),
            pl.BlockSpec((None, None, D_MODEL, 2 * D_EXPERT), expert_block),
            pl.BlockSpec((None, None, 1, 2 * D_EXPERT), expert_block),
            pl.BlockSpec((None, None, D_EXPERT, D_MODEL), expert_block),
            pl.BlockSpec((None, None, 1, D_MODEL), expert_block),
        ],
        out_specs=pl.BlockSpec(memory_space=pl.ANY),
        scratch_shapes=[
            pltpu.VMEM((2, EXP_TM, D_MODEL), F32),
            pltpu.VMEM((2, EXP_TM, D_MODEL), F32),
            pltpu.VMEM((D_MODEL, 2 * D_EXPERT), BF16),
            pltpu.VMEM((D_EXPERT, D_MODEL), BF16),
            pltpu.SemaphoreType.DMA((2,)),
            pltpu.SemaphoreType.DMA((2,)),
        ],
    )
    return pl.pallas_call(
        _expert_kernel,
        grid_spec=grid_spec,
        out_shape=jax.ShapeDtypeStruct(xs.shape, F32),
        input_output_aliases={4: 0},
        compiler_params=_cparams(1),
        name="moe_experts",
    )(te, nt, src, dst, xs, w1, b1.reshape(DEPTH, N_EXPERTS, 1, 2 * D_EXPERT),
      w2, b2.reshape(DEPTH, N_EXPERTS, 1, D_MODEL))


def _combine_kernel(ys_ref, meta_ref, x_ref, gate_ref, g_ref, b_ref, xo_ref):
    meta = meta_ref[...]
    col = lax.broadcasted_iota(jnp.int32, (TM, MOE_CHUNK_ROWS), 1).astype(F32)
    gmat = jnp.zeros((TM, MOE_CHUNK_ROWS), F32)
    for k in range(TOP_K):
        gmat = gmat + jnp.where(col == meta[:, k:k + 1], meta[:, TOP_K + k:TOP_K + k + 1], 0.0)
    y = jnp.dot(gmat.astype(BF16), ys_ref[...].astype(BF16), preferred_element_type=F32)
    z = DEEPNORM_ALPHA * x_ref[...] + gate_ref[...] * y
    xo_ref[...] = _layernorm(z, g_ref[...], b_ref[...])


def _combine_ln(ys, meta, x, mods, layer, ln_g, ln_b, chunk_off=0, n_chunks=N_CHUNKS):
    row = pl.BlockSpec((1, D_MODEL), lambda i: (0, 0))
    return pl.pallas_call(
        _combine_kernel,
        grid=(n_chunks,),
        in_specs=[
            pl.BlockSpec((MOE_CHUNK_ROWS, D_MODEL), lambda i: (i + chunk_off, 0)),
            pl.BlockSpec((TM, 2 * TOP_K), lambda i: (i + chunk_off, 0)),
            pl.BlockSpec((TM, D_MODEL), lambda i: (i + chunk_off, 0)),
            _mod_spec(layer, 5, chunk_off, 0),
            row, row,
        ],
        out_specs=pl.BlockSpec((TM, D_MODEL), lambda i: (i, 0)),
        out_shape=jax.ShapeDtypeStruct((n_chunks * TM, D_MODEL), F32),
        compiler_params=_cparams(1),
        name="moe_combine_layernorm",
    )(ys, meta, x, mods, ln_g, ln_b)


def _moe(x, xs, meta, pieces, mods, layer, w1, b1, w2, b2, ln_g, ln_b, split_out=False):
    te, nt, src, dst = _moe_schedule(pieces.reshape(N_CHUNKS, N_EXPERTS).astype(jnp.int32))
    ys = _experts(xs, te, nt, src, dst, layer, w1, b1, w2, b2)
    if not split_out:
        return _combine_ln(ys, meta, x, mods, layer, ln_g, ln_b)
    return (_combine_ln(ys, meta, x, mods, layer, ln_g, ln_b, 0, PROMPT_TILES),
            _combine_ln(ys, meta, x, mods, layer, ln_g, ln_b, PROMPT_TILES, SAMPLE_TILES))


def _mla_norm_kernel(p_ref, qn_ref, kvn_ref, cq_ref, ckvkr_ref):
    p = p_ref[...]
    cq = p[:, :MLA_Q_RANK]
    cq_ref[...] = cq * lax.rsqrt(jnp.mean(cq * cq, axis=-1, keepdims=True) + RMS_EPS) * qn_ref[...]
    ckv = p[:, MLA_Q_RANK:MLA_Q_RANK + MLA_KV_RANK]
    ckvkr_ref[:, :MLA_KV_RANK] = (ckv * lax.rsqrt(jnp.mean(ckv * ckv, axis=-1, keepdims=True) + RMS_EPS)
                                  * kvn_ref[...])
    ckvkr_ref[:, MLA_KV_RANK:] = p[:, MLA_Q_RANK + MLA_KV_RANK:]


def _mla_norm(proj, q_norm, kv_norm):
    return pl.pallas_call(
        _mla_norm_kernel,
        grid=(N_TOK // TM,),
        in_specs=[
            pl.BlockSpec((TM, 512), lambda i: (i, 0)),
            pl.BlockSpec((1, MLA_Q_RANK), lambda i: (0, 0)),
            pl.BlockSpec((1, MLA_KV_RANK), lambda i: (0, 0)),
        ],
        out_specs=[pl.BlockSpec((TM, MLA_Q_RANK), lambda i: (i, 0)),
                   pl.BlockSpec((TM, 2 * MLA_KV_RANK), lambda i: (i, 0))],
        out_shape=[jax.ShapeDtypeStruct((N_TOK, MLA_Q_RANK), F32),
                   jax.ShapeDtypeStruct((N_TOK, 2 * MLA_KV_RANK), F32)],
        compiler_params=_cparams(1),
        name="mla_norm",
    )(proj, q_norm, kv_norm)


def _mla_weights(w_in, w_uq, w_ukv):
    pad = 512 - w_in.shape[1]
    w_in_p = jnp.pad(w_in, ((0, 0), (0, pad)))
    wq = w_uq.reshape(MLA_Q_RANK, MLA_HEADS, MLA_NOPE + MLA_ROPE)
    wq = jnp.pad(wq, ((0, 0), (0, 0), (0, LANES - MLA_NOPE - MLA_ROPE))).reshape(MLA_Q_RANK, MLA_HEADS * LANES)
    wkv = w_ukv.reshape(MLA_KV_RANK, MLA_HEADS, MLA_NOPE + MLA_V)
    wkn = jnp.pad(wkv[:, :, :MLA_NOPE], ((0, 0), (0, 0), (0, LANES - MLA_NOPE)))
    eye = jnp.pad(jnp.eye(MLA_ROPE, dtype=F32), ((0, 0), (MLA_NOPE, LANES - MLA_NOPE - MLA_ROPE)))
    wkr = jnp.broadcast_to(eye[:, None, :], (MLA_ROPE, MLA_HEADS, LANES))
    zeros = jnp.zeros((2 * MLA_KV_RANK - MLA_KV_RANK - MLA_ROPE, MLA_HEADS, LANES), F32)
    wk = jnp.concatenate([wkn, wkr, zeros], axis=0).reshape(2 * MLA_KV_RANK, MLA_HEADS * LANES)
    wv = jnp.pad(wkv[:, :, MLA_NOPE:].reshape(MLA_KV_RANK, MLA_HEADS * MLA_V), ((0, MLA_KV_RANK), (0, 0)))
    return w_in_p.astype(BF16), wq.astype(BF16), wk.astype(BF16), wv.astype(BF16)


def kernel(x_prompt, x_sample, cache_na_k, cache_na_v, cache_diff_k, cache_diff_v, cache_mla_ckv, cache_mla_kr,
           c, c_ctx, ada_w, ada_b, ln_g, ln_b, na_w_qkv, na_rpb, na_w_o, diff_w_qkv, diff_lambda, diff_subln,
           diff_w_o, mla_w_in, mla_q_norm, mla_w_uq, mla_kv_norm, mla_w_ukv, mla_w_o, router_w, router_b,
           moe_w1, moe_b1, moe_w2, moe_b2):
    x = jnp.concatenate([x_prompt.reshape(N_PROMPT, D_MODEL), x_sample.reshape(N_SAMPLE, D_MODEL)], axis=0)
    cond = jnp.concatenate([c_ctx[None, :], c, jnp.zeros((N_MOD_ROWS - 1 - DEC_BATCH, D_MODEL), F32)], axis=0)
    mods = _modulation(cond, ada_w, ada_b)

    n_na = na_w_qkv.shape[0]
    n_diff = diff_w_qkv.shape[0]
    n_mla = mla_w_in.shape[0]
    cache_na_k = cache_na_k.reshape(DEC_BATCH, n_na, PAST_LEN, D_MODEL)
    cache_na_v = cache_na_v.reshape(DEC_BATCH, n_na, PAST_LEN, D_MODEL)
    cache_diff_k = cache_diff_k.reshape(DEC_BATCH, n_diff, PAST_LEN, D_MODEL)
    cache_diff_v = cache_diff_v.reshape(DEC_BATCH, n_diff, PAST_LEN, D_MODEL)
    all_tiles = N_TOK // TM
    nb = D_MODEL // LANES
    dec_off = N_PROMPT // DEC_SEQ

    st_na_k, st_na_v, st_diff_k, st_diff_v, st_mla_ckv, st_mla_kr = [], [], [], [], [], []
    for i in range(DEPTH):
        kind, j = i % N_MIXERS, i // N_MIXERS
        latent = dict(tile_off=PROMPT_TILES, mods=mods, layer=i)
        if kind == 0:
            w = na_w_qkv[j].astype(BF16)
            qk_scale = dict(q_cols=D_MODEL, q_scale=NA_HEAD_DIM ** -0.5, out_dtype=BF16)
            q_p, k_p, v_p = _context_qkv(x, w, mods, i, qk_scale, "na")
            st_na_k.append(k_p)
            st_na_v.append(v_p)
            qkv_s = _project(x, w, n_tiles=SAMPLE_TILES, name="na_qkv_latent", **latent, **qk_scale)
            o_p = _attention(q_p, k_p, v_p, mode="pair", n_batch=BATCH, seq=SEQ, tq=SEQ, q_col=0, k_col=0,
                             v_col=0, hb=nb, name="na_context_attention")
            o_s = _na_latent_attention(qkv_s, cache_na_k, cache_na_v, j, _na_bias_tables(na_rpb[j]))
            w_o = na_w_o[j]
        elif kind == 1:
            lam_init = 0.8 - 0.6 * math.exp(-0.3 * i)
            w = diff_w_qkv[j].astype(BF16)
            cos, sin = _axial_tables(LANES, DIFF_HEAD_DIM // 4, 0, LANES)
            qk_scale = dict(q_cols=D_MODEL, q_scale=DIFF_HEAD_DIM ** -0.5, out_dtype=BF16)
            q_p, k_p, v_p = _context_qkv(x, w, mods, i, qk_scale, "diff")
            st_diff_k.append(k_p)
            st_diff_v.append(v_p)
            qk_s = _project(x, w[:, :2 * D_MODEL], n_tiles=SAMPLE_TILES, rope=(DIFF_HEAD_DIM // 4, cos, sin),
                            name="diff_qk_latent", **latent, **qk_scale)
            v_s = _project(x, w[:, 2 * D_MODEL:], n_tiles=SAMPLE_TILES, out_dtype=BF16, name="diff_v_latent",
                           **latent)
            common = dict(mode="diff", lam=diff_lambda[j], subln=diff_subln[j][None, :], lam_init=lam_init)
            o_p = _attention(q_p, k_p, v_p, n_batch=BATCH, seq=SEQ, tq=SEQ, q_col=0, k_col=0, v_col=0, hb=nb,
                             name="diff_context_attention", **common)
            o_s = _attention(qk_s, qk_s, v_s, n_batch=DEC_BATCH, seq=DEC_SEQ, tq=LATENT_TQ,q_col=0, k_col=nb, v_col=0,
                             ctx=(cache_diff_k, cache_diff_v, j), hb=2, name="diff_latent_attention", **common)
            w_o = diff_w_o[j]
        else:
            w_in, wq, wk, wv = _mla_weights(mla_w_in[j], mla_w_uq[j], mla_w_ukv[j])
            cos, sin = _axial_tables(LANES, MLA_ROPE // 4, MLA_NOPE, MLA_NOPE + MLA_ROPE)
            rope = (MLA_ROPE // 4, cos, sin)
            proj = _project(x, w_in, n_tiles=all_tiles, mods=mods, layer=i, name="mla_down")
            cq, ckvkr = _mla_norm(proj, mla_q_norm[j][None, :], mla_kv_norm[j][None, :])
            st_mla_ckv.append(ckvkr[:N_PROMPT, :MLA_KV_RANK])
            st_mla_kr.append(ckvkr[:N_PROMPT, MLA_KV_RANK:MLA_KV_RANK + MLA_ROPE])
            q_scale = dict(q_cols=MLA_HEADS * LANES, q_scale=(MLA_NOPE + MLA_ROPE) ** -0.5, out_dtype=BF16)
            q_p = _project(cq, wq, n_tiles=PROMPT_TILES, name="mla_q_context", **q_scale)
            k_p = _project(ckvkr, wk, n_tiles=PROMPT_TILES, out_dtype=BF16, name="mla_k_context")
            q_s = _project(cq, wq, tile_off=PROMPT_TILES, n_tiles=SAMPLE_TILES, rope=rope, name="mla_q_latent",
                           **q_scale)
            k_s = _project(ckvkr, wk, tile_off=PROMPT_TILES, n_tiles=SAMPLE_TILES, rope=rope, out_dtype=BF16,
                           name="mla_k_latent")
            v_all = _project(ckvkr, wv, n_tiles=all_tiles, out_dtype=BF16, name="mla_v")
            ctx_in = jnp.concatenate(
                [cache_mla_ckv[:, j], cache_mla_kr[:, j],
                 jnp.zeros((DEC_BATCH, PAST_LEN, 2 * MLA_KV_RANK - MLA_KV_RANK - MLA_ROPE), F32)],
                axis=-1).reshape(DEC_BATCH * PAST_LEN, 2 * MLA_KV_RANK)
            n_ctx_tiles = DEC_BATCH * PAST_LEN // TM
            k_c = _project(ctx_in, wk, n_tiles=n_ctx_tiles, out_dtype=BF16, name="mla_k_cache")
            v_c = _project(ctx_in, wv, n_tiles=n_ctx_tiles, out_dtype=BF16, name="mla_v_cache")
            k_c = k_c.reshape(DEC_BATCH, 1, PAST_LEN, MLA_HEADS * LANES)
            v_c = v_c.reshape(DEC_BATCH, 1, PAST_LEN, D_MODEL)
            o_p = _attention(q_p, k_p, v_all, mode="mla", n_batch=BATCH, seq=SEQ, tq=SEQ, q_col=0, k_col=0,
                             v_col=0, hb=nb, name="mla_context_attention")
            o_s = _attention(q_s, k_s, v_all, mode="mla", n_batch=DEC_BATCH, seq=DEC_SEQ, tq=LATENT_TQ,q_col=0, k_col=0,
                             v_col=0, v_row_off=dec_off, ctx=(k_c, v_c, 0), hb=2, name="mla_latent_attention")
            w_o = mla_w_o[j]

        x, xs, meta, pieces = _out_route(o_p, o_s, w_o.astype(BF16), x, mods, i, ln_g[i, 0][None, :],
                                         ln_b[i, 0][None, :], router_w, router_b)
        x = _moe(x, xs, meta, pieces, mods, i, moe_w1, moe_b1, moe_w2, moe_b2,
                 ln_g[i, 1][None, :], ln_b[i, 1][None, :], split_out=i == DEPTH - 1)

    y_prompt = x[0].reshape(BATCH, SEQ, D_MODEL)
    y_sample = x[1].reshape(DEC_BATCH, DEC_SEQ, D_MODEL)

    def stack(parts, *tail):
        return jnp.stack([p.reshape(BATCH, SEQ, *tail) for p in parts], axis=1)

    return (y_prompt, y_sample,
            stack(st_na_k, NA_HEADS, NA_HEAD_DIM), stack(st_na_v, NA_HEADS, NA_HEAD_DIM),
            stack(st_diff_k, DIFF_HEADS, 2 * DIFF_HEAD_DIM), stack(st_diff_v, DIFF_HEADS, 2 * DIFF_HEAD_DIM),
            stack(st_mla_ckv, MLA_KV_RANK), stack(st_mla_kr, MLA_ROPE))
```
